```python
import math
import jax, jax.numpy as jnp
from jax import lax
import numpy as np

D_MODEL = 2048
BATCH = 1
SEQ = 16384
DEPTH = 2

MIX_WIDTH = D_MODEL
POOL_WIDTH = MIX_WIDTH // 2
POOL_WINDOWS = (2, 4, 8, 16)
N_POOL_GROUPS = len(POOL_WINDOWS)
POOL_GROUP = POOL_WIDTH // N_POOL_GROUPS
GLA_WIDTH = MIX_WIDTH - POOL_WIDTH
GLA_HEADS = 4
GLA_DV = GLA_WIDTH // GLA_HEADS
GLA_KEY_WIDTH = GLA_WIDTH // 2
GLA_DK = GLA_KEY_WIDTH // GLA_HEADS
GATE_RANK = 16
GATE_TAU = 16.0
CHUNK = 64
IN_PROJ_WIDTH = POOL_WIDTH + 2 * GLA_KEY_WIDTH + GLA_WIDTH + GATE_RANK + GLA_WIDTH
D_FF_DENSE = 5632
N_EXPERTS = 8
TOP_K = 2
D_FF_EXPERT = 7168
MOE_BLOCK = 256
EPS = 1e-6
N_DENSE = (DEPTH + 1) // 2
N_MOE = DEPTH // 2

kernel_name = "hybrid_pool_gla_moe_trunk"


def rmsnorm(x, g):
    xf = x.astype(jnp.float32)
    y = xf * lax.rsqrt(jnp.mean(xf * xf, axis=-1, keepdims=True) + EPS)
    return (y * g.astype(jnp.float32)).astype(x.dtype)


def pool_mixer(u, w_pool, pool_scale):
    B, S, _ = u.shape
    ug = u.reshape(B, S, N_POOL_GROUPS, POOL_GROUP)
    c = jnp.cumsum(ug.astype(jnp.float32), axis=1)
    c = jnp.pad(c, ((0, 0), (1, 0), (0, 0), (0, 0)))
    t1 = jnp.arange(1, S + 1)
    diffs = []
    for gi, w in enumerate(POOL_WINDOWS):
        lo = jnp.maximum(t1 - w, 0)
        cg = c[:, :, gi]
        win_sum = cg[:, 1:] - cg[:, lo]
        count = (t1 - lo).astype(jnp.float32)[None, :, None]
        diffs.append(win_sum / count - ug[:, :, gi].astype(jnp.float32))
    d = jnp.stack(diffs, axis=2)
    y = jnp.einsum('bsgc,gcd->bsgd', d, w_pool.astype(jnp.float32)) * pool_scale.astype(jnp.float32)
    return y.reshape(B, S, POOL_WIDTH).astype(u.dtype)


def gla_mixer(q, k, v, gate_lr, r, w_gate_up, b_gate, gla_norm):
    B, S, _ = q.shape
    H, C, N = GLA_HEADS, CHUNK, S // CHUNK
    f32 = jnp.float32
    q = q.astype(f32).reshape(B, N, C, H, GLA_DK) * (GLA_DK ** -0.5)
    k = k.astype(f32).reshape(B, N, C, H, GLA_DK)
    v = v.astype(f32).reshape(B, N, C, H, GLA_DV)
    g = jax.nn.log_sigmoid(gate_lr.astype(f32) @ w_gate_up.astype(f32) + b_gate.astype(f32)) / GATE_TAU
    g = g.reshape(B, N, C, H, GLA_DK)
    bc = jnp.cumsum(g, axis=2)
    b_last = bc[:, :, -1]
    q_dec = q * jnp.exp(bc)
    k_dec = k * jnp.exp(-bc)
    k_to_end = k * jnp.exp(b_last[:, :, None] - bc)
    chunk_state = jnp.einsum('bnchk,bnchv->bnhkv', k_to_end, v)
    decay = jnp.exp(b_last)

    def step(state, inp):
        dec, cs = inp
        return dec[..., None] * state + cs, state

    init = jnp.zeros((B, H, GLA_DK, GLA_DV), f32)
    _, s_prev = lax.scan(step, init, (jnp.moveaxis(decay, 1, 0), jnp.moveaxis(chunk_state, 1, 0)))
    s_prev = jnp.moveaxis(s_prev, 0, 1)
    o_inter = jnp.einsum('bnchk,bnhkv->bnchv', q_dec, s_prev)
    att = jnp.einsum('bnihk,bnjhk->bnhij', q_dec, k_dec)
    causal = jnp.tril(jnp.ones((C, C), dtype=bool))
    att = jnp.where(causal, att, 0.0)
    o_intra = jnp.einsum('bnhij,bnjhv->bnihv', att, v)
    o = (o_inter + o_intra).reshape(B, S, H, GLA_DV)
    o = o * lax.rsqrt(jnp.mean(o * o, axis=-1, keepdims=True) + EPS) * gla_norm.astype(f32).reshape(H, GLA_DV)
    o = o.reshape(B, S, GLA_WIDTH) * jax.nn.silu(r.astype(f32))
    return o.astype(r.dtype)


def swiglu(h, w_gate, w_up, w_down):
    return (jax.nn.silu(h @ w_gate) * (h @ w_up)) @ w_down


def moe_swiglu(h, w_router, e_gate, e_up, e_down):
    B, S, D = h.shape
    T = B * S
    A = T * TOP_K
    hf = h.reshape(T, D)
    logits = (hf @ w_router).astype(jnp.float32)
    top_vals, top_idx = lax.top_k(logits, TOP_K)
    top_w = jax.nn.softmax(top_vals, axis=-1)
    flat_e = top_idx.reshape(-1)
    flat_w = top_w.reshape(-1)
    flat_tok = jnp.arange(A, dtype=jnp.int32) // TOP_K
    order = jnp.argsort(flat_e)
    e_sorted = flat_e[order]
    counts = jnp.bincount(flat_e, length=N_EXPERTS)
    padded = ((counts + MOE_BLOCK - 1) // MOE_BLOCK) * MOE_BLOCK
    grp_start = jnp.cumsum(counts) - counts
    cum_padded = jnp.cumsum(padded)
    pad_start = cum_padded - padded
    dest = pad_start[e_sorted] + (jnp.arange(A) - grp_start[e_sorted])
    n_blocks = (A + MOE_BLOCK - 1) // MOE_BLOCK + N_EXPERTS
    L = n_blocks * MOE_BLOCK
    slot_tok = jnp.full((L,), T, jnp.int32).at[dest].set(flat_tok[order])
    slot_w = jnp.zeros((L,), jnp.float32).at[dest].set(flat_w[order])
    block_expert = jnp.minimum(
        jnp.searchsorted(cum_padded, jnp.arange(n_blocks) * MOE_BLOCK, side='right'), N_EXPERTS - 1)
    x_pad = jnp.concatenate([hf, jnp.zeros((1, D), hf.dtype)], axis=0)
    xb = x_pad[slot_tok].reshape(n_blocks, MOE_BLOCK, D)

    def run_block(args):
        xblk, e = args
        return (jax.nn.silu(xblk @ e_gate[e]) * (xblk @ e_up[e])) @ e_down[e]

    yb = lax.map(run_block, (xb, block_expert))
    y = yb.reshape(L, D) * slot_w.astype(yb.dtype)[:, None]
    out = jnp.zeros((T + 1, D), y.dtype).at[slot_tok].add(y)[:T]
    return out.reshape(B, S, D)


def setup_inputs(seed: int = 0) -> dict:
    key = jax.random.key(seed)
    ks = jax.random.split(key, 20)
    nrm = lambda k, shape, fan_in: jax.random.normal(k, shape, jnp.float32) * (fan_in ** -0.5)
    gain = lambda k, shape: 1.0 + 0.02 * jax.random.normal(k, shape, jnp.float32)
    return {
        "x": jax.random.normal(ks[0], (BATCH, SEQ, D_MODEL), jnp.float32),
        "mix_norm": gain(ks[1], (DEPTH, D_MODEL)),
        "w_in": nrm(ks[2], (DEPTH, D_MODEL, IN_PROJ_WIDTH), D_MODEL),
        "w_pool": nrm(ks[3], (DEPTH, N_POOL_GROUPS, POOL_GROUP, POOL_GROUP), POOL_GROUP),
        "pool_scale": gain(ks[4], (DEPTH, N_POOL_GROUPS, POOL_GROUP)),
        "w_gate_up": nrm(ks[5], (DEPTH, GATE_RANK, GLA_KEY_WIDTH), GATE_RANK),
        "b_gate": 0.1 * jax.random.normal(ks[6], (DEPTH, GLA_KEY_WIDTH), jnp.float32),
        "gla_norm": gain(ks[7], (DEPTH, GLA_WIDTH)),
        "w_out": nrm(ks[8], (DEPTH, MIX_WIDTH, D_MODEL), MIX_WIDTH),
        "ffn_norm": gain(ks[9], (DEPTH, D_MODEL)),
        "dense_w_gate": nrm(ks[10], (N_DENSE, D_MODEL, D_FF_DENSE), D_MODEL),
        "dense_w_up": nrm(ks[11], (N_DENSE, D_MODEL, D_FF_DENSE), D_MODEL),
        "dense_w_down": nrm(ks[12], (N_DENSE, D_FF_DENSE, D_MODEL), D_FF_DENSE),
        "w_router": nrm(ks[13], (N_MOE, D_MODEL, N_EXPERTS), D_MODEL),
        "exp_w_gate": nrm(ks[14], (N_MOE, N_EXPERTS, D_MODEL, D_FF_EXPERT), D_MODEL),
        "exp_w_up": nrm(ks[15], (N_MOE, N_EXPERTS, D_MODEL, D_FF_EXPERT), D_MODEL),
        "exp_w_down": nrm(ks[16], (N_MOE, N_EXPERTS, D_FF_EXPERT, D_MODEL), D_FF_EXPERT),
        "final_norm": gain(ks[17], (D_MODEL,)),
    }


def reference(x, mix_norm, w_in, w_pool, pool_scale, w_gate_up, b_gate, gla_norm, w_out,
              ffn_norm, dense_w_gate, dense_w_up, dense_w_down, w_router,
              exp_w_gate, exp_w_up, exp_w_down, final_norm):
    splits = np.cumsum([POOL_WIDTH, GLA_KEY_WIDTH, GLA_KEY_WIDTH, GLA_WIDTH, GATE_RANK]).tolist()
    for l in range(DEPTH):
        h = rmsnorm(x, mix_norm[l])
        z = h @ w_in[l]
        u_pool, q, k, v, gate_lr, r = jnp.split(z, splits, axis=-1)
        pool_out = pool_mixer(u_pool, w_pool[l], pool_scale[l])
        gla_out = gla_mixer(q, k, v, gate_lr, r, w_gate_up[l], b_gate[l], gla_norm[l])
        x = x + jnp.concatenate([pool_out, gla_out], axis=-1) @ w_out[l]
        h = rmsnorm(x, ffn_norm[l])
        if l % 2 == 0:
            i = l // 2
            x = x + swiglu(h, dense_w_gate[i], dense_w_up[i], dense_w_down[i])
        else:
            i = l // 2
            x = x + moe_swiglu(h, w_router[i], exp_w_gate[i], exp_w_up[i], exp_w_down[i])
    return rmsnorm(x, final_norm)
```

```python
import functools

import jax
import jax.numpy as jnp
from jax import lax
from jax.experimental import pallas as pl
from jax.experimental.pallas import tpu as pltpu

EPS = 1e-6
POOL_WINDOWS = (2, 4, 8, 16)
GLA_HEADS = 4
GATE_TAU = 16.0
CHUNK = 64
TOP_K = 2

LANES = 128
V7X_VMEM_BYTES = 64 * 1024 * 1024
VMEM_CAP_BYTES = V7X_VMEM_BYTES - 8 * 1024 * 1024

F32 = jnp.float32
BF16 = jnp.bfloat16
HIGHEST = lax.Precision.HIGHEST


def _tile(n, pref):
    t = min(n, pref)
    while n % t:
        t -= 1
    return t


def _params(vmem_estimate_bytes, n_axes):
    limit = min(VMEM_CAP_BYTES, max(32 * 1024 * 1024, int(vmem_estimate_bytes * 1.25)))
    return pltpu.CompilerParams(
        dimension_semantics=("arbitrary",) * n_axes, vmem_limit_bytes=limit)


def _rmsnorm_rows(x_ref, g_ref, dst_ref):
    rows = x_ref.shape[0]
    chunk = _tile(rows, 128)

    def body(c, carry):
        r0 = pl.multiple_of(c * chunk, chunk)
        x = x_ref[pl.ds(r0, chunk), :]
        ms = jnp.mean(x * x, axis=-1, keepdims=True)
        dst_ref[pl.ds(r0, chunk), :] = (x * lax.rsqrt(ms + EPS) * g_ref[...]).astype(dst_ref.dtype)
        return carry

    lax.fori_loop(0, rows // chunk, body, 0)


def _silu(a):
    return a * (1.0 / (1.0 + jnp.exp(-a)))


def _inproj_kernel(x_ref, g_ref, w_ref, wgl_ref, z_ref, zg_ref, h_scr):
    @pl.when(pl.program_id(1) == 0)
    def _():
        _rmsnorm_rows(x_ref, g_ref, h_scr)
        zg_ref[...] = jnp.dot(h_scr[...], wgl_ref[...], preferred_element_type=F32)

    z_ref[...] = jnp.dot(h_scr[...], w_ref[...], preferred_element_type=F32).astype(z_ref.dtype)


def _inproj(x, g, w_main, w_gl):
    T, D = x.shape
    N = w_main.shape[1]
    tm, tn = _tile(T, 1024), _tile(N, 1024)
    est = 2 * tm * D * 4 + tm * D * 2 + 2 * D * tn * 2 + 2 * tm * tn * 2 + 2 * D * LANES * 2 + 2 * tm * LANES * 4
    return pl.pallas_call(
        _inproj_kernel,
        out_shape=(jax.ShapeDtypeStruct((T, N), BF16), jax.ShapeDtypeStruct((T, LANES), F32)),
        grid=(T // tm, N // tn),
        in_specs=[
            pl.BlockSpec((tm, D), lambda i, j: (i, 0)),
            pl.BlockSpec((1, D), lambda i, j: (0, 0)),
            pl.BlockSpec((D, tn), lambda i, j: (0, j)),
            pl.BlockSpec((D, LANES), lambda i, j: (0, 0)),
        ],
        out_specs=(
            pl.BlockSpec((tm, tn), lambda i, j: (i, j)),
            pl.BlockSpec((tm, LANES), lambda i, j: (i, 0)),
        ),
        scratch_shapes=[pltpu.VMEM((tm, D), BF16)],
        compiler_params=_params(est, 2),
        name="inproj",
    )(x, g, w_main, w_gl)


POOL_HALO = 128


def _pool_kernel(u_ref, halo_ref, wp_ref, ps_ref, o_ref):
    i = pl.program_id(0)
    tp = u_ref.shape[0]
    C = wp_ref.shape[1]
    r = lax.broadcasted_iota(jnp.int32, (tp, tp), 0)
    c = lax.broadcasted_iota(jnp.int32, (tp, tp), 1)
    rh = lax.broadcasted_iota(jnp.int32, (tp, POOL_HALO), 0)
    ch = lax.broadcasted_iota(jnp.int32, (tp, POOL_HALO), 1)
    t1 = i * tp + lax.broadcasted_iota(jnp.int32, (tp, 1), 0) + 1
    for gi, w in enumerate(POOL_WINDOWS):
        cols = slice(gi * C, (gi + 1) * C)
        u = u_ref[:, cols]
        halo = halo_ref[:, cols]
        halo = jnp.where(i > 0, halo, jnp.zeros_like(halo))
        band = jnp.where((c <= r) & (c > r - w), 1.0, 0.0).astype(BF16)
        band_h = jnp.where(ch >= rh + (POOL_HALO + 1 - w), 1.0, 0.0).astype(BF16)
        win_sum = (jnp.dot(band, u, preferred_element_type=F32)
                   + jnp.dot(band_h, halo, preferred_element_type=F32))
        count = jnp.minimum(t1, w).astype(F32)
        d = win_sum / count - u.astype(F32)
        y = jnp.dot(d.astype(BF16), wp_ref[gi], preferred_element_type=F32) * ps_ref[:, cols]
        o_ref[:, cols] = y.astype(o_ref.dtype)


def _pool(z, w_pool, pool_scale):
    T = z.shape[0]
    G, C, _ = w_pool.shape
    W = G * C
    tp = _tile(T, 256)
    assert tp % POOL_HALO == 0 and POOL_HALO >= max(POOL_WINDOWS)
    hb = tp // POOL_HALO
    est = 2 * (tp + POOL_HALO) * W * 2 + 2 * G * C * C * 2 + 2 * tp * W * 2 + 4 * tp * tp * 4
    return pl.pallas_call(
        _pool_kernel,
        out_shape=jax.ShapeDtypeStruct((T, W), BF16),
        grid=(T // tp,),
        in_specs=[
            pl.BlockSpec((tp, W), lambda i: (i, 0)),
            pl.BlockSpec((POOL_HALO, W), lambda i: (jnp.maximum(i * hb - 1, 0), 0)),
            pl.BlockSpec((G, C, C), lambda i: (0, 0, 0)),
            pl.BlockSpec((1, W), lambda i: (0, 0)),
        ],
        out_specs=pl.BlockSpec((tp, W), lambda i: (i, 0)),
        compiler_params=_params(est, 1),
        name="pool",
    )(z, z, w_pool, pool_scale)


def _log_sigmoid(x):
    return jnp.minimum(x, 0.0) - jnp.log1p(jnp.exp(-jnp.abs(x)))


def _gla_kernel(q_ref, k_ref, v_ref, r_ref, zg_ref, wgu_ref, bg_ref, gn_ref, o_ref, st_ref, *, dk, dv):
    @pl.when(pl.program_id(0) == 0)
    def _():
        st_ref[...] = jnp.zeros_like(st_ref)

    tg = q_ref.shape[0]
    logit = jnp.dot(zg_ref[...], wgu_ref[...], preferred_element_type=F32, precision=HIGHEST) + bg_ref[...]
    g = _log_sigmoid(logit) * (1.0 / GATE_TAU)
    rr = lax.broadcasted_iota(jnp.int32, (tg, tg), 0)
    cc = lax.broadcasted_iota(jnp.int32, (tg, tg), 1)
    tri = jnp.where((cc <= rr) & (cc // CHUNK == rr // CHUNK), 1.0, 0.0).astype(F32)
    bc = jnp.dot(tri, g, preferred_element_type=F32, precision=HIGHEST)
    ri = lax.broadcasted_iota(jnp.int32, (CHUNK, CHUNK), 0)
    ci = lax.broadcasted_iota(jnp.int32, (CHUNK, CHUNK), 1)
    causal = ci <= ri
    scale = dk ** -0.5
    nt = (((1,), (1,)), ((), ()))
    tn = (((0,), (0,)), ((), ()))
    for c in range(tg // CHUNK):
        rows = slice(c * CHUNK, (c + 1) * CHUNK)
        bcc = bc[rows, :]
        b_last = bcc[CHUNK - 1:CHUNK, :]
        kf = k_ref[rows, :].astype(F32)
        q_dec = (q_ref[rows, :].astype(F32) * scale * jnp.exp(bcc)).astype(BF16)
        k_dec = (kf * jnp.exp(-bcc)).astype(BF16)
        k_end = (kf * jnp.exp(b_last - bcc)).astype(BF16)
        decay = jnp.exp(b_last)
        for h in range(GLA_HEADS):
            ks = slice(h * dk, (h + 1) * dk)
            vs = slice(h * dv, (h + 1) * dv)
            v = v_ref[rows, vs]
            s_t = st_ref[h]
            o_inter = lax.dot_general(q_dec[:, ks], s_t.astype(BF16), nt, preferred_element_type=F32)
            att = lax.dot_general(q_dec[:, ks], k_dec[:, ks], nt, preferred_element_type=F32)
            att = jnp.where(causal, att, 0.0)
            o_intra = jnp.dot(att.astype(BF16), v, preferred_element_type=F32)
            st_ref[h] = s_t * decay[:, ks] + lax.dot_general(v, k_end[:, ks], tn, preferred_element_type=F32)
            o = o_inter + o_intra
            o = o * lax.rsqrt(jnp.mean(o * o, axis=-1, keepdims=True) + EPS) * gn_ref[:, vs]
            o_ref[rows, vs] = (o * _silu(r_ref[rows, vs].astype(F32))).astype(o_ref.dtype)


def _gla(z, zg, wgu, bg, gn, *, q_off, k_off, v_off, r_off, key, width):
    T = z.shape[0]
    dk, dv = key // GLA_HEADS, width // GLA_HEADS
    tg = _tile(T, 256)
    assert tg % CHUNK == 0 and T % CHUNK == 0
    assert q_off % key == 0 and k_off % key == 0 and v_off % width == 0 and r_off % width == 0
    est = 2 * tg * (2 * key + 3 * width) * 2 + 2 * tg * LANES * 4 + GLA_HEADS * dv * dk * 4 + 6 * tg * key * 4
    return pl.pallas_call(
        functools.partial(_gla_kernel, dk=dk, dv=dv),
        out_shape=jax.ShapeDtypeStruct((T, width), BF16),
        grid=(T // tg,),
        in_specs=[
            pl.BlockSpec((tg, key), lambda i: (i, q_off // key)),
            pl.BlockSpec((tg, key), lambda i: (i, k_off // key)),
            pl.BlockSpec((tg, width), lambda i: (i, v_off // width)),
            pl.BlockSpec((tg, width), lambda i: (i, r_off // width)),
            pl.BlockSpec((tg, LANES), lambda i: (i, 0)),
            pl.BlockSpec((LANES, key), lambda i: (0, 0)),
            pl.BlockSpec((1, key), lambda i: (0, 0)),
            pl.BlockSpec((1, width), lambda i: (0, 0)),
        ],
        out_specs=pl.BlockSpec((tg, width), lambda i: (i, 0)),
        scratch_shapes=[pltpu.VMEM((GLA_HEADS, dv, dk), F32)],
        compiler_params=_params(est, 1),
        name="gla",
    )(z, z, z, z, zg, wgu, bg, gn)


def _outproj_kernel(x_ref, p_ref, a_ref, wp_ref, wa_ref, o_ref):
    o_ref[...] = (x_ref[...]
                  + jnp.dot(p_ref[...], wp_ref[...], preferred_element_type=F32)
                  + jnp.dot(a_ref[...], wa_ref[...], preferred_element_type=F32))


def _outproj(x, pool_out, gla_out, w_out):
    T, D = x.shape
    wp_rows, wa_rows = pool_out.shape[1], gla_out.shape[1]
    tm, tn = _tile(T, 1024), _tile(D, 1024)
    assert wp_rows % tn == 0 or wp_rows == w_out.shape[0]
    est = 4 * tm * tn * 4 + 2 * tm * (wp_rows + wa_rows) * 2 + 2 * (wp_rows + wa_rows) * tn * 2
    return pl.pallas_call(
        _outproj_kernel,
        out_shape=jax.ShapeDtypeStruct((T, D), F32),
        grid=(T // tm, D // tn),
        in_specs=[
            pl.BlockSpec((tm, tn), lambda i, j: (i, j)),
            pl.BlockSpec((tm, wp_rows), lambda i, j: (i, 0)),
            pl.BlockSpec((tm, wa_rows), lambda i, j: (i, 0)),
            pl.BlockSpec((wp_rows, tn), lambda i, j: (0, j)),
            pl.BlockSpec((wa_rows, tn), lambda i, j: (wp_rows // wa_rows, j)),
        ],
        out_specs=pl.BlockSpec((tm, tn), lambda i, j: (i, j)),
        compiler_params=_params(est, 2),
        name="outproj",
    )(x, pool_out, gla_out, w_out, w_out)


def _ffn_kernel(x_ref, g_ref, wg_ref, wu_ref, wd_ref, fg_ref, o_ref, h_scr, *, final_norm):
    f = pl.program_id(1)

    @pl.when(f == 0)
    def _():
        _rmsnorm_rows(x_ref, g_ref, h_scr)
        o_ref[...] = x_ref[...]

    h = h_scr[...]
    a = jnp.dot(h, wg_ref[...], preferred_element_type=F32)
    b = jnp.dot(h, wu_ref[...], preferred_element_type=F32)
    o_ref[...] += jnp.dot((_silu(a) * b).astype(BF16), wd_ref[...], preferred_element_type=F32)

    if final_norm:
        @pl.when(f == pl.num_programs(1) - 1)
        def _():
            _rmsnorm_rows(o_ref, fg_ref, o_ref)


def _dense_ffn(x, g, wg, wu, wd, fg, final_norm):
    T, D = x.shape
    F = wg.shape[1]
    tm, tf = _tile(T, 512), _tile(F, 512)
    est = 4 * tm * D * 4 + tm * D * 2 + 2 * 3 * D * tf * 2 + 3 * tm * tf * 4
    return pl.pallas_call(
        functools.partial(_ffn_kernel, final_norm=final_norm),
        out_shape=jax.ShapeDtypeStruct((T, D), F32),
        grid=(T // tm, F // tf),
        in_specs=[
            pl.BlockSpec((tm, D), lambda i, f: (i, 0)),
            pl.BlockSpec((1, D), lambda i, f: (0, 0)),
            pl.BlockSpec((D, tf), lambda i, f: (0, f)),
            pl.BlockSpec((D, tf), lambda i, f: (0, f)),
            pl.BlockSpec((tf, D), lambda i, f: (f, 0)),
            pl.BlockSpec((1, D), lambda i, f: (0, 0)),
        ],
        out_specs=pl.BlockSpec((tm, D), lambda i, f: (i, 0)),
        scratch_shapes=[pltpu.VMEM((tm, D), BF16)],
        compiler_params=_params(est, 2),
        name="dense_ffn",
    )(x, g, wg, wu, wd, fg)


LANE_E1, LANE_E2, LANE_RANK1, LANE_RANK2 = 0, 1, 2, 3


def _pack_bf16_pairs(lo, hi):
    lo_bits = pltpu.bitcast(lo.astype(BF16).astype(F32), jnp.uint32)
    hi_bits = pltpu.bitcast(hi.astype(BF16).astype(F32), jnp.uint32)
    return (hi_bits & jnp.uint32(0xFFFF0000)) | (lo_bits >> 16)


def _unpack_bf16_pairs(p):
    lo = pltpu.bitcast(p << 16, F32).astype(BF16)
    hi = pltpu.bitcast(p & jnp.uint32(0xFFFF0000), F32).astype(BF16)
    return lo, hi


def _router_kernel(x_ref, g_ref, wr_ref, hp_ref, mi_ref, mf_ref, cnt_ref, h_scr, run_ref, *, n_experts):
    @pl.when(pl.program_id(0) == 0)
    def _():
        run_ref[...] = jnp.zeros_like(run_ref)

    tm, D = x_ref.shape
    half = D // 2
    _rmsnorm_rows(x_ref, g_ref, h_scr)
    h = h_scr[...]
    hp_ref[...] = _pack_bf16_pairs(h[:, :half], h[:, half:])
    logits = jnp.dot(h.astype(BF16), wr_ref[...], preferred_element_type=F32)
    lane = lax.broadcasted_iota(jnp.int32, (tm, LANES), 1)
    neg = jnp.float32(-jnp.inf)
    l1 = jnp.where(lane < n_experts, logits, neg)
    m1 = jnp.max(l1, axis=-1, keepdims=True)
    e1 = jnp.min(jnp.where(l1 == m1, lane, LANES), axis=-1, keepdims=True)
    l2 = jnp.where(lane == e1, neg, l1)
    m2 = jnp.max(l2, axis=-1, keepdims=True)
    e2 = jnp.min(jnp.where(l2 == m2, lane, LANES), axis=-1, keepdims=True)
    ex = jnp.exp(m2 - m1)
    w1 = 1.0 / (1.0 + ex)
    w2 = ex / (1.0 + ex)
    onehot = jnp.where((lane == e1) | (lane == e2), 1.0, 0.0)
    rr = lax.broadcasted_iota(jnp.int32, (tm, tm), 0)
    cc = lax.broadcasted_iota(jnp.int32, (tm, tm), 1)
    strict = jnp.where(cc < rr, 1.0, 0.0).astype(BF16)
    before = jnp.dot(strict, onehot.astype(BF16), preferred_element_type=F32) + run_ref[...]
    rank1 = jnp.sum(jnp.where(lane == e1, before, 0.0), axis=-1, keepdims=True).astype(jnp.int32)
    rank2 = jnp.sum(jnp.where(lane == e2, before, 0.0), axis=-1, keepdims=True).astype(jnp.int32)
    run_ref[...] += jnp.sum(onehot, axis=0, keepdims=True)
    mi_ref[...] = jnp.where(lane == LANE_E1, e1, jnp.where(lane == LANE_E2, e2, jnp.where(
        lane == LANE_RANK1, rank1, jnp.where(lane == LANE_RANK2, rank2, 0))))
    mf_ref[...] = jnp.where(lane == 0, w1, jnp.where(lane == 1, w2, 0.0))
    cnt_ref[...] = jnp.broadcast_to(run_ref[...], cnt_ref.shape)


def _router(x, g, wr, n_experts):
    T, D = x.shape
    tm = _tile(T, 512)
    est = 2 * tm * D * 4 + tm * D * 4 + 2 * tm * (D // 2) * 4 + 4 * tm * tm * 4 + 2 * D * LANES * 2
    return pl.pallas_call(
        functools.partial(_router_kernel, n_experts=n_experts),
        out_shape=(
            jax.ShapeDtypeStruct((T, D // 2), jnp.uint32),
            jax.ShapeDtypeStruct((T, LANES), jnp.int32),
            jax.ShapeDtypeStruct((T, LANES), F32),
            jax.ShapeDtypeStruct((8, LANES), F32),
        ),
        grid=(T // tm,),
        in_specs=[
            pl.BlockSpec((tm, D), lambda i: (i, 0)),
            pl.BlockSpec((1, D), lambda i: (0, 0)),
            pl.BlockSpec((D, LANES), lambda i: (0, 0)),
        ],
        out_specs=(
            pl.BlockSpec((tm, D // 2), lambda i: (i, 0)),
            pl.BlockSpec((tm, LANES), lambda i: (i, 0)),
            pl.BlockSpec((tm, LANES), lambda i: (i, 0)),
            pl.BlockSpec((8, LANES), lambda i: (0, 0)),
        ),
        scratch_shapes=[pltpu.VMEM((tm, D), F32), pltpu.VMEM((1, LANES), F32)],
        compiler_params=_params(est, 1),
        name="router",
    )(x, g, wr)


def _dispatch_kernel(d1_ref, d2_ref, h_ref, xs_in_ref, xs_ref, sem):
    del xs_in_ref
    td = h_ref.shape[0]
    base = pl.program_id(0) * td

    def issue(r, carry):
        for d_ref in (d1_ref, d2_ref):
            dst = d_ref[base + r]
            pltpu.make_async_copy(h_ref.at[pl.ds(r, 1)], xs_ref.at[pl.ds(dst, 1)], sem).start()
        return carry

    lax.fori_loop(0, td, issue, 0)
    for _ in range(TOP_K):
        pltpu.make_async_copy(h_ref, xs_ref.at[pl.ds(0, td)], sem).wait()


def _dispatch(hp, dest1, dest2, n_slots):
    T, W = hp.shape
    td = _tile(T, 256)
    xs0 = jnp.zeros((n_slots, W), hp.dtype)
    est = 2 * td * W * 4
    return pl.pallas_call(
        _dispatch_kernel,
        out_shape=jax.ShapeDtypeStruct((n_slots, W), hp.dtype),
        grid_spec=pltpu.PrefetchScalarGridSpec(
            num_scalar_prefetch=2,
            grid=(T // td,),
            in_specs=[
                pl.BlockSpec((td, W), lambda i, d1, d2: (i, 0)),
                pl.BlockSpec(memory_space=pl.ANY),
            ],
            out_specs=pl.BlockSpec(memory_space=pl.ANY),
            scratch_shapes=[pltpu.SemaphoreType.DMA],
        ),
        input_output_aliases={3: 0},
        compiler_params=_params(est, 1),
        name="dispatch",
    )(dest1, dest2, hp, xs0)


def _expert_kernel(te_ref, nu_ref, xs_ref, wg_ref, wu_ref, wd_ref, y_ref, xb_scr):
    i, f = pl.program_id(0), pl.program_id(1)
    used = i < nu_ref[0]

    @pl.when(used & (f == 0))
    def _():
        lo, hi = _unpack_bf16_pairs(xs_ref[...])
        half = lo.shape[1]
        xb_scr[:, :half] = lo
        xb_scr[:, half:] = hi

    @pl.when(used)
    def _():
        xb = xb_scr[...]
        a = jnp.dot(xb, wg_ref[...], preferred_element_type=F32)
        b = jnp.dot(xb, wu_ref[...], preferred_element_type=F32)
        y = jnp.dot((_silu(a) * b).astype(BF16), wd_ref[...], preferred_element_type=F32)

        @pl.when(f == 0)
        def _():
            y_ref[...] = y

        @pl.when(f > 0)
        def _():
            y_ref[...] += y

    @pl.when(jnp.logical_not(used) & (f == 0))
    def _():
        y_ref[...] = jnp.zeros_like(y_ref)


def _expert_ffn(xs, tile_expert, n_used, wg, wu, wd, tm):
    L, W = xs.shape
    E, D, F = wg.shape
    tf = _tile(F, 512)
    nf = F // tf
    n_tiles = L // tm

    def w_col(i, f, te, nu):
        return (te[i], 0, jnp.where(i < nu[0], f, nf - 1))

    def w_row(i, f, te, nu):
        return (te[i], jnp.where(i < nu[0], f, nf - 1), 0)

    est = 2 * tm * W * 4 + tm * D * 2 + 2 * 3 * D * tf * 2 + 2 * tm * D * 4 + 3 * tm * tf * 4
    return pl.pallas_call(
        _expert_kernel,
        out_shape=jax.ShapeDtypeStruct((L, D), F32),
        grid_spec=pltpu.PrefetchScalarGridSpec(
            num_scalar_prefetch=2,
            grid=(n_tiles, nf),
            in_specs=[
                pl.BlockSpec((tm, W), lambda i, f, te, nu: (i, 0)),
                pl.BlockSpec((None, D, tf), w_col),
                pl.BlockSpec((None, D, tf), w_col),
                pl.BlockSpec((None, tf, D), w_row),
            ],
            out_specs=pl.BlockSpec((tm, D), lambda i, f, te, nu: (i, 0)),
            scratch_shapes=[pltpu.VMEM((tm, D), BF16)],
        ),
        compiler_params=_params(est, 2),
        name="expert_ffn",
    )(tile_expert, n_used, xs, wg, wu, wd)


def _combine_kernel(d1_ref, d2_ref, x_ref, mf_ref, fg_ref, y_ref, o_ref, y1_scr, y2_scr, sem, *, final_norm):
    tc = x_ref.shape[0]
    base = pl.program_id(0) * tc

    def issue(r, carry):
        for d_ref, dst in ((d1_ref, y1_scr), (d2_ref, y2_scr)):
            src = d_ref[base + r]
            pltpu.make_async_copy(y_ref.at[pl.ds(src, 1)], dst.at[pl.ds(r, 1)], sem).start()
        return carry

    lax.fori_loop(0, tc, issue, 0)
    for dst in (y1_scr, y2_scr):
        pltpu.make_async_copy(y_ref.at[pl.ds(0, tc)], dst, sem).wait()
    w = mf_ref[...]
    o_ref[...] = x_ref[...] + w[:, 0:1] * y1_scr[...] + w[:, 1:2] * y2_scr[...]
    if final_norm:
        _rmsnorm_rows(o_ref, fg_ref, o_ref)


def _combine(x, mf, y, dest1, dest2, fg, final_norm):
    T, D = x.shape
    tc = _tile(T, 256)
    est = 4 * tc * D * 4 + 2 * tc * D * 4 + 2 * tc * LANES * 4
    return pl.pallas_call(
        functools.partial(_combine_kernel, final_norm=final_norm),
        out_shape=jax.ShapeDtypeStruct((T, D), F32),
        grid_spec=pltpu.PrefetchScalarGridSpec(
            num_scalar_prefetch=2,
            grid=(T // tc,),
            in_specs=[
                pl.BlockSpec((tc, D), lambda i, d1, d2: (i, 0)),
                pl.BlockSpec((tc, LANES), lambda i, d1, d2: (i, 0)),
                pl.BlockSpec((1, D), lambda i, d1, d2: (0, 0)),
                pl.BlockSpec(memory_space=pl.ANY),
            ],
            out_specs=pl.BlockSpec((tc, D), lambda i, d1, d2: (i, 0)),
            scratch_shapes=[pltpu.VMEM((tc, D), F32), pltpu.VMEM((tc, D), F32), pltpu.SemaphoreType.DMA],
        ),
        compiler_params=_params(est, 1),
        name="combine",
    )(dest1, dest2, x, mf, fg, y)


def _moe_ffn(x, g, w_router, wg, wu, wd, fg, final_norm):
    T, D = x.shape
    E = w_router.shape[1]
    tm = _tile(T, 1024)
    wr = jnp.pad(w_router, ((0, 0), (0, LANES - E))).astype(BF16)
    hp, mi, mf, cnt = _router(x, g, wr, E)
    counts = cnt[0, :E].astype(jnp.int32)
    padded = ((counts + tm - 1) // tm) * tm
    cum_padded = jnp.cumsum(padded)
    pad_start = cum_padded - padded
    dest1 = pad_start[mi[:, LANE_E1]] + mi[:, LANE_RANK1]
    dest2 = pad_start[mi[:, LANE_E2]] + mi[:, LANE_RANK2]
    n_tiles = (T * TOP_K) // tm + E
    tile_expert = jnp.minimum(
        jnp.searchsorted(cum_padded, jnp.arange(n_tiles, dtype=jnp.int32) * tm, side="right"), E - 1
    ).astype(jnp.int32)
    n_used = (cum_padded[-1:] // tm).astype(jnp.int32)
    xs = _dispatch(hp, dest1, dest2, n_tiles * tm)
    y = _expert_ffn(xs, tile_expert, n_used, wg, wu, wd, tm)
    return _combine(x, mf, y, dest1, dest2, fg, final_norm)


def kernel(x, mix_norm, w_in, w_pool, pool_scale, w_gate_up, b_gate, gla_norm, w_out, ffn_norm,
           dense_w_gate, dense_w_up, dense_w_down, w_router, exp_w_gate, exp_w_up, exp_w_down, final_norm):
    B, S, D = x.shape
    depth = w_in.shape[0]
    G, C = w_pool.shape[1], w_pool.shape[2]
    pool_w = G * C
    rank, key = w_gate_up.shape[1], w_gate_up.shape[2]
    width = gla_norm.shape[1]
    gate_off = pool_w + 2 * key + width
    assert B == 1 and rank <= LANES and w_in.shape[2] == gate_off + rank + width
    xt = x.reshape(S, D)
    fg = final_norm.reshape(1, D)
    for l in range(depth):
        w = w_in[l]
        w_main = jnp.concatenate([w[:, :gate_off], w[:, gate_off + rank:]], axis=1).astype(BF16)
        w_gl = jnp.pad(w[:, gate_off:gate_off + rank], ((0, 0), (0, LANES - rank))).astype(BF16)
        z, zg = _inproj(xt, mix_norm[l].reshape(1, D), w_main, w_gl)
        pool_out = _pool(z, w_pool[l].astype(BF16), pool_scale[l].reshape(1, pool_w))
        wgu = jnp.pad(w_gate_up[l], ((0, LANES - rank), (0, 0)))
        gla_out = _gla(z, zg, wgu, b_gate[l].reshape(1, key), gla_norm[l].reshape(1, width),
                       q_off=pool_w, k_off=pool_w + key, v_off=pool_w + 2 * key, r_off=gate_off,
                       key=key, width=width)
        xt = _outproj(xt, pool_out, gla_out, w_out[l].astype(BF16))
        last = l == depth - 1
        i = l // 2
        if l % 2 == 0:
            xt = _dense_ffn(xt, ffn_norm[l].reshape(1, D), dense_w_gate[i].astype(BF16),
                            dense_w_up[i].astype(BF16), dense_w_down[i].astype(BF16), fg, last)
        else:
            xt = _moe_ffn(xt, ffn_norm[l].reshape(1, D), w_router[i], exp_w_gate[i].astype(BF16),
                          exp_w_up[i].astype(BF16), exp_w_down[i].astype(BF16), fg, last)
    return xt.reshape(B, S, D)
```

```python
import functools

import jax
import jax.numpy as jnp
from jax import lax
from jax.experimental import pallas as pl
from jax.experimental.pallas import tpu as pltpu

EPS = 1e-6
POOL_WINDOWS = (2, 4, 8, 16)
GLA_HEADS = 4
GATE_TAU = 16.0
CHUNK = 64
TOP_K = 2

LANES = 128
V7X_VMEM_BYTES = 64 * 1024 * 1024
VMEM_CAP_BYTES = V7X_VMEM_BYTES - 8 * 1024 * 1024

F32 = jnp.float32
BF16 = jnp.bfloat16
HIGHEST = lax.Precision.HIGHEST


def _tile(n, pref):
    t = min(n, pref)
    while n % t:
        t -= 1
    return t


def _params(vmem_estimate_bytes, n_axes):
    limit = min(VMEM_CAP_BYTES, max(32 * 1024 * 1024, int(vmem_estimate_bytes * 1.25)))
    return pltpu.CompilerParams(
        dimension_semantics=("arbitrary",) * n_axes, vmem_limit_bytes=limit)


def _rmsnorm_rows(x_ref, g_ref, dst_ref):
    rows = x_ref.shape[0]
    chunk = _tile(rows, 128)

    def body(c, carry):
        r0 = pl.multiple_of(c * chunk, chunk)
        x = x_ref[pl.ds(r0, chunk), :]
        ms = jnp.mean(x * x, axis=-1, keepdims=True)
        dst_ref[pl.ds(r0, chunk), :] = (x * lax.rsqrt(ms + EPS) * g_ref[...]).astype(dst_ref.dtype)
        return carry

    lax.fori_loop(0, rows // chunk, body, 0)


def _silu(a):
    return a * (1.0 / (1.0 + jnp.exp(-a)))


def _inproj_kernel(x_ref, g_ref, w_ref, wgl_ref, z_ref, zg_ref, h_scr):
    @pl.when(pl.program_id(1) == 0)
    def _():
        _rmsnorm_rows(x_ref, g_ref, h_scr)
        zg_ref[...] = jnp.dot(h_scr[...], wgl_ref[...], preferred_element_type=F32)

    z_ref[...] = jnp.dot(h_scr[...], w_ref[...], preferred_element_type=F32).astype(z_ref.dtype)


def _inproj(x, g, w_main, w_gl):
    T, D = x.shape
    N = w_main.shape[1]
    tm, tn = _tile(T, 1024), _tile(N, 1024)
    est = 2 * tm * D * 4 + tm * D * 2 + 2 * D * tn * 2 + 2 * tm * tn * 2 + 2 * D * LANES * 2 + 2 * tm * LANES * 4
    return pl.pallas_call(
        _inproj_kernel,
        out_shape=(jax.ShapeDtypeStruct((T, N), BF16), jax.ShapeDtypeStruct((T, LANES), F32)),
        grid=(T // tm, N // tn),
        in_specs=[
            pl.BlockSpec((tm, D), lambda i, j: (i, 0)),
            pl.BlockSpec((1, D), lambda i, j: (0, 0)),
            pl.BlockSpec((D, tn), lambda i, j: (0, j)),
            pl.BlockSpec((D, LANES), lambda i, j: (0, 0)),
        ],
        out_specs=(
            pl.BlockSpec((tm, tn), lambda i, j: (i, j)),
            pl.BlockSpec((tm, LANES), lambda i, j: (i, 0)),
        ),
        scratch_shapes=[pltpu.VMEM((tm, D), BF16)],
        compiler_params=_params(est, 2),
        name="inproj",
    )(x, g, w_main, w_gl)


POOL_HALO = 128


def _pool_kernel(u_ref, halo_ref, wp_ref, ps_ref, o_ref):
    i = pl.program_id(0)
    tp = u_ref.shape[0]
    C = wp_ref.shape[1]
    r = lax.broadcasted_iota(jnp.int32, (tp, tp), 0)
    c = lax.broadcasted_iota(jnp.int32, (tp, tp), 1)
    rh = lax.broadcasted_iota(jnp.int32, (tp, POOL_HALO), 0)
    ch = lax.broadcasted_iota(jnp.int32, (tp, POOL_HALO), 1)
    t1 = i * tp + lax.broadcasted_iota(jnp.int32, (tp, 1), 0) + 1
    for gi, w in enumerate(POOL_WINDOWS):
        cols = slice(gi * C, (gi + 1) * C)
        u = u_ref[:, cols]
        halo = halo_ref[:, cols]
        halo = jnp.where(i > 0, halo, jnp.zeros_like(halo))
        band = jnp.where((c <= r) & (c > r - w), 1.0, 0.0).astype(BF16)
        band_h = jnp.where(ch >= rh + (POOL_HALO + 1 - w), 1.0, 0.0).astype(BF16)
        win_sum = (jnp.dot(band, u, preferred_element_type=F32)
                   + jnp.dot(band_h, halo, preferred_element_type=F32))
        count = jnp.minimum(t1, w).astype(F32)
        d = win_sum / count - u.astype(F32)
        y = jnp.dot(d.astype(BF16), wp_ref[gi], preferred_element_type=F32) * ps_ref[:, cols]
        o_ref[:, cols] = y.astype(o_ref.dtype)


def _pool(z, w_pool, pool_scale):
    T = z.shape[0]
    G, C, _ = w_pool.shape
    W = G * C
    tp = _tile(T, 256)
    assert tp % POOL_HALO == 0 and POOL_HALO >= max(POOL_WINDOWS)
    hb = tp // POOL_HALO
    est = 2 * (tp + POOL_HALO) * W * 2 + 2 * G * C * C * 2 + 2 * tp * W * 2 + 4 * tp * tp * 4
    return pl.pallas_call(
        _pool_kernel,
        out_shape=jax.ShapeDtypeStruct((T, W), BF16),
        grid=(T // tp,),
        in_specs=[
            pl.BlockSpec((tp, W), lambda i: (i, 0)),
            pl.BlockSpec((POOL_HALO, W), lambda i: (jnp.maximum(i * hb - 1, 0), 0)),
            pl.BlockSpec((G, C, C), lambda i: (0, 0, 0)),
            pl.BlockSpec((1, W), lambda i: (0, 0)),
        ],
        out_specs=pl.BlockSpec((tp, W), lambda i: (i, 0)),
        compiler_params=_params(est, 1),
        name="pool",
    )(z, z, w_pool, pool_scale)


def _log_sigmoid(x):
    return jnp.minimum(x, 0.0) - jnp.log1p(jnp.exp(-jnp.abs(x)))


def _gla_kernel(q_ref, k_ref, v_ref, r_ref, zg_ref, wgu_ref, bg_ref, gn_ref, o_ref, st_ref, *, dk, dv):
    @pl.when(pl.program_id(0) == 0)
    def _():
        st_ref[...] = jnp.zeros_like(st_ref)

    tg = q_ref.shape[0]
    logit = jnp.dot(zg_ref[...], wgu_ref[...], preferred_element_type=F32, precision=HIGHEST) + bg_ref[...]
    g = _log_sigmoid(logit) * (1.0 / GATE_TAU)
    rr = lax.broadcasted_iota(jnp.int32, (tg, tg), 0)
    cc = lax.broadcasted_iota(jnp.int32, (tg, tg), 1)
    tri = jnp.where((cc <= rr) & (cc // CHUNK == rr // CHUNK), 1.0, 0.0).astype(F32)
    bc = jnp.dot(tri, g, preferred_element_type=F32, precision=HIGHEST)
    ri = lax.broadcasted_iota(jnp.int32, (CHUNK, CHUNK), 0)
    ci = lax.broadcasted_iota(jnp.int32, (CHUNK, CHUNK), 1)
    causal = ci <= ri
    scale = dk ** -0.5
    nt = (((1,), (1,)), ((), ()))
    tn = (((0,), (0,)), ((), ()))
    for c in range(tg // CHUNK):
        rows = slice(c * CHUNK, (c + 1) * CHUNK)
        bcc = bc[rows, :]
        b_last = bcc[CHUNK - 1:CHUNK, :]
        kf = k_ref[rows, :].astype(F32)
        q_dec = (q_ref[rows, :].astype(F32) * scale * jnp.exp(bcc)).astype(BF16)
        k_dec = (kf * jnp.exp(-bcc)).astype(BF16)
        k_end = (kf * jnp.exp(b_last - bcc)).astype(BF16)
        decay = jnp.exp(b_last)
        for h in range(GLA_HEADS):
            ks = slice(h * dk, (h + 1) * dk)
            vs = slice(h * dv, (h + 1) * dv)
            v = v_ref[rows, vs]
            s_t = st_ref[h]
            o_inter = lax.dot_general(q_dec[:, ks], s_t.astype(BF16), nt, preferred_element_type=F32)
            att = lax.dot_general(q_dec[:, ks], k_dec[:, ks], nt, preferred_element_type=F32)
            att = jnp.where(causal, att, 0.0)
            o_intra = jnp.dot(att.astype(BF16), v, preferred_element_type=F32)
            st_ref[h] = s_t * decay[:, ks] + lax.dot_general(v, k_end[:, ks], tn, preferred_element_type=F32)
            o = o_inter + o_intra
            o = o * lax.rsqrt(jnp.mean(o * o, axis=-1, keepdims=True) + EPS) * gn_ref[:, vs]
            o_ref[rows, vs] = (o * _silu(r_ref[rows, vs].astype(F32))).astype(o_ref.dtype)


def _gla(z, zg, wgu, bg, gn, *, q_off, k_off, v_off, r_off, key, width):
    T = z.shape[0]
    dk, dv = key // GLA_HEADS, width // GLA_HEADS
    tg = _tile(T, 256)
    assert tg % CHUNK == 0 and T % CHUNK == 0
    assert q_off % key == 0 and k_off % key == 0 and v_off % width == 0 and r_off % width == 0
    est = 2 * tg * (2 * key + 3 * width) * 2 + 2 * tg * LANES * 4 + GLA_HEADS * dv * dk * 4 + 6 * tg * key * 4
    return pl.pallas_call(
        functools.partial(_gla_kernel, dk=dk, dv=dv),
        out_shape=jax.ShapeDtypeStruct((T, width), BF16),
        grid=(T // tg,),
        in_specs=[
            pl.BlockSpec((tg, key), lambda i: (i, q_off // key)),
            pl.BlockSpec((tg, key), lambda i: (i, k_off // key)),
            pl.BlockSpec((tg, width), lambda i: (i, v_off // width)),
            pl.BlockSpec((tg, width), lambda i: (i, r_off // width)),
            pl.BlockSpec((tg, LANES), lambda i: (i, 0)),
            pl.BlockSpec((LANES, key), lambda i: (0, 0)),
            pl.BlockSpec((1, key), lambda i: (0, 0)),
            pl.BlockSpec((1, width), lambda i: (0, 0)),
        ],
        out_specs=pl.BlockSpec((tg, width), lambda i: (i, 0)),
        scratch_shapes=[pltpu.VMEM((GLA_HEADS, dv, dk), F32)],
        compiler_params=_params(est, 1),
        name="gla",
    )(z, z, z, z, zg, wgu, bg, gn)


def _outproj_kernel(x_ref, p_ref, a_ref, wp_ref, wa_ref, o_ref):
    o_ref[...] = (x_ref[...]
                  + jnp.dot(p_ref[...], wp_ref[...], preferred_element_type=F32)
                  + jnp.dot(a_ref[...], wa_ref[...], preferred_element_type=F32))


def _outproj(x, pool_out, gla_out, w_out):
    T, D = x.shape
    wp_rows, wa_rows = pool_out.shape[1], gla_out.shape[1]
    tm, tn = _tile(T, 1024), _tile(D, 1024)
    assert wp_rows % tn == 0 or wp_rows == w_out.shape[0]
    est = 4 * tm * tn * 4 + 2 * tm * (wp_rows + wa_rows) * 2 + 2 * (wp_rows + wa_rows) * tn * 2
    return pl.pallas_call(
        _outproj_kernel,
        out_shape=jax.ShapeDtypeStruct((T, D), F32),
        grid=(T // tm, D // tn),
        in_specs=[
            pl.BlockSpec((tm, tn), lambda i, j: (i, j)),
            pl.BlockSpec((tm, wp_rows), lambda i, j: (i, 0)),
            pl.BlockSpec((tm, wa_rows), lambda i, j: (i, 0)),
            pl.BlockSpec((wp_rows, tn), lambda i, j: (0, j)),
            pl.BlockSpec((wa_rows, tn), lambda i, j: (wp_rows // wa_rows, j)),
        ],
        out_specs=pl.BlockSpec((tm, tn), lambda i, j: (i, j)),
        compiler_params=_params(est, 2),
        name="outproj",
    )(x, pool_out, gla_out, w_out, w_out)


def _swiglu_step(f, nf, h_scr, t_scr, wg_ref, wu_ref, wd_ref, acc_ref, side_work=None):
    def up():
        h = h_scr[...]
        a = jnp.dot(h, wg_ref[...].astype(BF16), preferred_element_type=F32)
        b = jnp.dot(h, wu_ref[...].astype(BF16), preferred_element_type=F32)
        return (_silu(a) * b).astype(BF16)

    def down():
        acc_ref[...] += jnp.dot(t_scr[...], wd_ref[...].astype(BF16), preferred_element_type=F32)

    def steady(extra):
        def body():
            t_new = up()
            if extra is not None:
                extra()
            down()
            t_scr[...] = t_new
        return body

    @pl.when(f == 0)
    def _():
        t_scr[...] = up()

    mid = (f > 0) & (f < nf)
    if side_work is None:
        pl.when(mid)(steady(None))
    else:
        pred, fn = side_work
        pl.when(mid & pred)(steady(fn))
        pl.when(mid & jnp.logical_not(pred))(steady(None))

    @pl.when(f == nf)
    def _():
        down()


def _ffn_kernel(x_ref, g_ref, wg_ref, wu_ref, wd_ref, fg_ref, o_ref, h_scr, t_scr, *, final_norm):
    f = pl.program_id(1)
    nf = pl.num_programs(1) - 1

    @pl.when(f == 0)
    def _():
        _rmsnorm_rows(x_ref, g_ref, h_scr)
        o_ref[...] = x_ref[...]

    _swiglu_step(f, nf, h_scr, t_scr, wg_ref, wu_ref, wd_ref, o_ref)

    if final_norm:
        @pl.when(f == nf)
        def _():
            _rmsnorm_rows(o_ref, fg_ref, o_ref)


def _dense_ffn(x, g, wg, wu, wd, fg, final_norm):
    T, D = x.shape
    F = wg.shape[1]
    tm, tf = _tile(T, 1024), _tile(F, 512)
    nf = F // tf
    wbytes = wg.dtype.itemsize
    est = 3 * tm * D * 4 + tm * D * 2 + tm * tf * 2 + 2 * 3 * D * tf * wbytes + 3 * tm * tf * 4
    return pl.pallas_call(
        functools.partial(_ffn_kernel, final_norm=final_norm),
        out_shape=jax.ShapeDtypeStruct((T, D), F32),
        grid=(T // tm, nf + 1),
        in_specs=[
            pl.BlockSpec((tm, D), lambda i, f: (i, 0), pipeline_mode=pl.Buffered(1)),
            pl.BlockSpec((1, D), lambda i, f: (0, 0)),
            pl.BlockSpec((D, tf), lambda i, f: (0, jnp.minimum(f, nf - 1))),
            pl.BlockSpec((D, tf), lambda i, f: (0, jnp.minimum(f, nf - 1))),
            pl.BlockSpec((tf, D), lambda i, f: (jnp.maximum(f - 1, 0), 0)),
            pl.BlockSpec((1, D), lambda i, f: (0, 0)),
        ],
        out_specs=pl.BlockSpec((tm, D), lambda i, f: (i, 0)),
        scratch_shapes=[pltpu.VMEM((tm, D), BF16), pltpu.VMEM((tm, tf), BF16)],
        compiler_params=_params(est, 2),
        name="dense_ffn",
    )(x, g, wg, wu, wd, fg)


LANE_E1, LANE_E2, LANE_RANK1, LANE_RANK2 = 0, 1, 2, 3


def _router_kernel(x_ref, g_ref, wr_ref, mi_ref, mf_ref, cnt_ref, h_scr, run_ref, *, n_experts):
    @pl.when(pl.program_id(0) == 0)
    def _():
        run_ref[...] = jnp.zeros_like(run_ref)

    tm = x_ref.shape[0]
    _rmsnorm_rows(x_ref, g_ref, h_scr)
    logits = jnp.dot(h_scr[...], wr_ref[...], preferred_element_type=F32)
    lane = lax.broadcasted_iota(jnp.int32, (tm, LANES), 1)
    neg = jnp.float32(-jnp.inf)
    l1 = jnp.where(lane < n_experts, logits, neg)
    m1 = jnp.max(l1, axis=-1, keepdims=True)
    e1 = jnp.min(jnp.where(l1 == m1, lane, LANES), axis=-1, keepdims=True)
    l2 = jnp.where(lane == e1, neg, l1)
    m2 = jnp.max(l2, axis=-1, keepdims=True)
    e2 = jnp.min(jnp.where(l2 == m2, lane, LANES), axis=-1, keepdims=True)
    ex = jnp.exp(m2 - m1)
    w1 = 1.0 / (1.0 + ex)
    w2 = ex / (1.0 + ex)
    onehot = jnp.where((lane == e1) | (lane == e2), 1.0, 0.0)
    rr = lax.broadcasted_iota(jnp.int32, (tm, tm), 0)
    cc = lax.broadcasted_iota(jnp.int32, (tm, tm), 1)
    strict = jnp.where(cc < rr, 1.0, 0.0).astype(BF16)
    before = jnp.dot(strict, onehot.astype(BF16), preferred_element_type=F32) + run_ref[...]
    rank1 = jnp.sum(jnp.where(lane == e1, before, 0.0), axis=-1, keepdims=True).astype(jnp.int32)
    rank2 = jnp.sum(jnp.where(lane == e2, before, 0.0), axis=-1, keepdims=True).astype(jnp.int32)
    run_ref[...] += jnp.sum(onehot, axis=0, keepdims=True)
    mi_ref[...] = jnp.where(lane == LANE_E1, e1, jnp.where(lane == LANE_E2, e2, jnp.where(
        lane == LANE_RANK1, rank1, jnp.where(lane == LANE_RANK2, rank2, 0))))
    mf_ref[...] = jnp.where(lane == 0, w1, jnp.where(lane == 1, w2, 0.0))
    cnt_ref[...] = jnp.broadcast_to(run_ref[...], cnt_ref.shape)


def _router(x, g, wr, n_experts):
    T, D = x.shape
    tm = _tile(T, 512)
    est = 2 * tm * D * 4 + tm * D * 2 + 4 * tm * tm * 4 + 2 * D * LANES * 2
    return pl.pallas_call(
        functools.partial(_router_kernel, n_experts=n_experts),
        out_shape=(
            jax.ShapeDtypeStruct((T, LANES), jnp.int32),
            jax.ShapeDtypeStruct((T, LANES), F32),
            jax.ShapeDtypeStruct((8, LANES), F32),
        ),
        grid=(T // tm,),
        in_specs=[
            pl.BlockSpec((tm, D), lambda i: (i, 0)),
            pl.BlockSpec((1, D), lambda i: (0, 0)),
            pl.BlockSpec((D, LANES), lambda i: (0, 0)),
        ],
        out_specs=(
            pl.BlockSpec((tm, LANES), lambda i: (i, 0)),
            pl.BlockSpec((tm, LANES), lambda i: (i, 0)),
            pl.BlockSpec((8, LANES), lambda i: (0, 0)),
        ),
        scratch_shapes=[pltpu.VMEM((tm, D), BF16), pltpu.VMEM((1, LANES), F32)],
        compiler_params=_params(est, 1),
        name="router",
    )(x, g, wr)


def _expert_kernel(te_ref, nu_ref, tok_ref, tok_next_ref, x_hbm, g_ref, wg_ref, wu_ref, wd_ref,
                   y_ref, xs_scr, xb_scr, t_scr, sem, *, rows_per_step, issue_steps):
    i, f = pl.program_id(0), pl.program_id(1)
    nf = pl.num_programs(1) - 1
    n_tiles = pl.num_programs(0)
    tm = xs_scr.shape[0]
    n_used = nu_ref[0]
    used = i < n_used

    def row_copy(idx_ref, r):
        return pltpu.make_async_copy(x_hbm.at[pl.ds(idx_ref[0, r], 1)], xs_scr.at[pl.ds(r, 1)], sem)

    @pl.when((i == 0) & (f == 0))
    def _():
        def issue(r, carry):
            row_copy(tok_ref, r).start()
            return carry
        lax.fori_loop(0, tm, issue, 0)

    @pl.when((f == 0) & ((i == 0) | (i - 1 < n_used)))
    def _():
        pltpu.make_async_copy(x_hbm.at[pl.ds(0, tm)], xs_scr, sem).wait()

    @pl.when(f == 0)
    def _():
        y_ref[...] = jnp.zeros_like(y_ref)

    @pl.when(used & (f == 0))
    def _():
        _rmsnorm_rows(xs_scr, g_ref, xb_scr)

    def issue_next_rows():
        base = (f - 1) * rows_per_step
        for r in range(rows_per_step):
            row_copy(tok_next_ref, base + r).start()

    @pl.when(used)
    def _():
        prefetch = (f <= issue_steps) & (i + 1 < n_tiles)
        _swiglu_step(f, nf, xb_scr, t_scr, wg_ref, wu_ref, wd_ref, y_ref,
                     side_work=(prefetch, issue_next_rows))


def _expert_ffn(x, g, slot_tok, tile_expert, n_used, wg, wu, wd, tm):
    T, D = x.shape
    E, _, F = wg.shape
    n_tiles = slot_tok.shape[0]
    tf = _tile(F, 256)
    nf = F // tf
    issue_steps = _tile(tm, nf - 1)
    rows_per_step = tm // issue_steps

    def w_col(i, f, te, nu):
        return (te[i], 0, jnp.where(i < nu[0], jnp.minimum(f, nf - 1), nf - 1))

    def w_row(i, f, te, nu):
        return (te[i], jnp.where(i < nu[0], jnp.maximum(f - 1, 0), nf - 1), 0)

    wbytes = wg.dtype.itemsize
    est = (tm * D * 4 + tm * D * 2 + tm * tf * 2 + 2 * 3 * D * tf * wbytes + 2 * tm * D * 4
           + 3 * tm * tf * 4 + 3 * D * tf * 2)
    smem = pl.BlockSpec((None, 1, tm), lambda i, f, te, nu: (i, 0, 0), memory_space=pltpu.SMEM)
    smem_next = pl.BlockSpec((None, 1, tm), lambda i, f, te, nu: (jnp.minimum(i + 1, n_tiles - 1), 0, 0),
                             memory_space=pltpu.SMEM)
    return pl.pallas_call(
        functools.partial(_expert_kernel, rows_per_step=rows_per_step, issue_steps=issue_steps),
        out_shape=jax.ShapeDtypeStruct((n_tiles * tm, D), F32),
        grid_spec=pltpu.PrefetchScalarGridSpec(
            num_scalar_prefetch=2,
            grid=(n_tiles, nf + 1),
            in_specs=[
                smem,
                smem_next,
                pl.BlockSpec(memory_space=pl.ANY),
                pl.BlockSpec((1, D), lambda i, f, te, nu: (0, 0)),
                pl.BlockSpec((None, D, tf), w_col),
                pl.BlockSpec((None, D, tf), w_col),
                pl.BlockSpec((None, tf, D), w_row),
            ],
            out_specs=pl.BlockSpec((tm, D), lambda i, f, te, nu: (i, 0)),
            scratch_shapes=[pltpu.VMEM((tm, D), F32), pltpu.VMEM((tm, D), BF16), pltpu.VMEM((tm, tf), BF16),
                            pltpu.SemaphoreType.DMA],
        ),
        compiler_params=_params(est, 2),
        name="expert_ffn",
    )(tile_expert, n_used, slot_tok, slot_tok, x, g, wg, wu, wd)


def _combine_kernel(d1_ref, d2_ref, x_ref, mf_ref, fg_ref, y_ref, o_ref, y1_scr, y2_scr, sem, *, final_norm):
    tc = x_ref.shape[0]
    base = pl.program_id(0) * tc

    def issue(r, carry):
        for d_ref, dst in ((d1_ref, y1_scr), (d2_ref, y2_scr)):
            src = d_ref[base + r]
            pltpu.make_async_copy(y_ref.at[pl.ds(src, 1)], dst.at[pl.ds(r, 1)], sem).start()
        return carry

    lax.fori_loop(0, tc, issue, 0)
    for dst in (y1_scr, y2_scr):
        pltpu.make_async_copy(y_ref.at[pl.ds(0, tc)], dst, sem).wait()
    w = mf_ref[...]
    o_ref[...] = x_ref[...] + w[:, 0:1] * y1_scr[...] + w[:, 1:2] * y2_scr[...]
    if final_norm:
        _rmsnorm_rows(o_ref, fg_ref, o_ref)


def _combine(x, mf, y, dest1, dest2, fg, final_norm):
    T, D = x.shape
    tc = _tile(T, 256)
    est = 4 * tc * D * 4 + 2 * tc * D * 4 + 2 * tc * LANES * 4
    return pl.pallas_call(
        functools.partial(_combine_kernel, final_norm=final_norm),
        out_shape=jax.ShapeDtypeStruct((T, D), F32),
        grid_spec=pltpu.PrefetchScalarGridSpec(
            num_scalar_prefetch=2,
            grid=(T // tc,),
            in_specs=[
                pl.BlockSpec((tc, D), lambda i, d1, d2: (i, 0)),
                pl.BlockSpec((tc, LANES), lambda i, d1, d2: (i, 0)),
                pl.BlockSpec((1, D), lambda i, d1, d2: (0, 0)),
                pl.BlockSpec(memory_space=pl.ANY),
            ],
            out_specs=pl.BlockSpec((tc, D), lambda i, d1, d2: (i, 0)),
            scratch_shapes=[pltpu.VMEM((tc, D), F32), pltpu.VMEM((tc, D), F32), pltpu.SemaphoreType.DMA],
        ),
        compiler_params=_params(est, 1),
        name="combine",
    )(dest1, dest2, x, mf, fg, y)


def _moe_ffn(x, g, w_router, wg, wu, wd, fg, final_norm):
    T, D = x.shape
    E = w_router.shape[1]
    tm = _tile(T, 1024)
    wr = jnp.pad(w_router, ((0, 0), (0, LANES - E))).astype(BF16)
    mi, mf, cnt = _router(x, g, wr, E)
    counts = cnt[0, :E].astype(jnp.int32)
    padded = ((counts + tm - 1) // tm) * tm
    cum_padded = jnp.cumsum(padded)
    pad_start = cum_padded - padded
    dest1 = pad_start[mi[:, LANE_E1]] + mi[:, LANE_RANK1]
    dest2 = pad_start[mi[:, LANE_E2]] + mi[:, LANE_RANK2]
    n_tiles = (T * TOP_K) // tm + E
    tile_start = jnp.arange(n_tiles, dtype=jnp.int32) * tm
    tile_expert = jnp.minimum(
        jnp.sum((tile_start[:, None] >= cum_padded[None, :]).astype(jnp.int32), axis=1), E - 1)
    n_used = (cum_padded[-1:] // tm).astype(jnp.int32)
    tok = jnp.arange(T, dtype=jnp.int32)
    slot_tok = jnp.zeros((n_tiles * tm,), jnp.int32).at[jnp.concatenate([dest1, dest2])].set(
        jnp.concatenate([tok, tok]))
    y = _expert_ffn(x, g, slot_tok.reshape(n_tiles, 1, tm), tile_expert, n_used, wg, wu, wd, tm)
    return _combine(x, mf, y, dest1, dest2, fg, final_norm)


def kernel(x, mix_norm, w_in, w_pool, pool_scale, w_gate_up, b_gate, gla_norm, w_out, ffn_norm,
           dense_w_gate, dense_w_up, dense_w_down, w_router, exp_w_gate, exp_w_up, exp_w_down, final_norm):
    B, S, D = x.shape
    depth = w_in.shape[0]
    G, C = w_pool.shape[1], w_pool.shape[2]
    pool_w = G * C
    rank, key = w_gate_up.shape[1], w_gate_up.shape[2]
    width = gla_norm.shape[1]
    gate_off = pool_w + 2 * key + width
    assert B == 1 and rank <= LANES and w_in.shape[2] == gate_off + rank + width
    xt = x.reshape(S, D)
    fg = final_norm.reshape(1, D)
    for l in range(depth):
        w = w_in[l]
        w_main = jnp.concatenate([w[:, :gate_off], w[:, gate_off + rank:]], axis=1).astype(BF16)
        w_gl = jnp.pad(w[:, gate_off:gate_off + rank], ((0, 0), (0, LANES - rank))).astype(BF16)
        z, zg = _inproj(xt, mix_norm[l].reshape(1, D), w_main, w_gl)
        pool_out = _pool(z, w_pool[l].astype(BF16), pool_scale[l].reshape(1, pool_w))
        wgu = jnp.pad(w_gate_up[l], ((0, LANES - rank), (0, 0)))
        gla_out = _gla(z, zg, wgu, b_gate[l].reshape(1, key), gla_norm[l].reshape(1, width),
                       q_off=pool_w, k_off=pool_w + key, v_off=pool_w + 2 * key, r_off=gate_off,
                       key=key, width=width)
        xt = _outproj(xt, pool_out, gla_out, w_out[l].astype(BF16))
        last = l == depth - 1
        i = l // 2
        if l % 2 == 0:
            xt = _dense_ffn(xt, ffn_norm[l].reshape(1, D), dense_w_gate[i].astype(BF16),
                            dense_w_up[i].astype(BF16), dense_w_down[i].astype(BF16), fg, last)
        else:
            xt = _moe_ffn(xt, ffn_norm[l].reshape(1, D), w_router[i], exp_w_gate[i], exp_w_up[i],
                          exp_w_down[i], fg, last)
    return xt.reshape(B, S, D)
```

```python
import functools

import jax
import jax.numpy as jnp
from jax import lax
from jax.experimental import pallas as pl
from jax.experimental.pallas import tpu as pltpu

EPS = 1e-6
POOL_WINDOWS = (2, 4, 8, 16)
GLA_HEADS = 4
GATE_TAU = 16.0
CHUNK = 64
TOP_K = 2

LANES = 128
V7X_VMEM_BYTES = 64 * 1024 * 1024
VMEM_CAP_BYTES = V7X_VMEM_BYTES - 8 * 1024 * 1024

F32 = jnp.float32
BF16 = jnp.bfloat16
HIGHEST = lax.Precision.HIGHEST


def _tile(n, pref):
    t = min(n, pref)
    while n % t:
        t -= 1
    return t


def _params(vmem_estimate_bytes, n_axes):
    limit = min(VMEM_CAP_BYTES, max(32 * 1024 * 1024, int(vmem_estimate_bytes * 1.25)))
    return pltpu.CompilerParams(
        dimension_semantics=("arbitrary",) * n_axes, vmem_limit_bytes=limit)


def _rmsnorm_rows(x_ref, g_ref, dst_ref):
    rows = x_ref.shape[0]
    chunk = _tile(rows, 128)

    def body(c, carry):
        r0 = pl.multiple_of(c * chunk, chunk)
        x = x_ref[pl.ds(r0, chunk), :]
        ms = jnp.mean(x * x, axis=-1, keepdims=True)
        dst_ref[pl.ds(r0, chunk), :] = (x * lax.rsqrt(ms + EPS) * g_ref[...]).astype(dst_ref.dtype)
        return carry

    lax.fori_loop(0, rows // chunk, body, 0)


def _silu(a):
    return a * (1.0 / (1.0 + jnp.exp(-a)))


def _inproj_kernel(x_ref, g_ref, w_ref, wgl_ref, z_ref, zg_ref, h_scr):
    @pl.when(pl.program_id(1) == 0)
    def _():
        _rmsnorm_rows(x_ref, g_ref, h_scr)
        zg_ref[...] = jnp.dot(h_scr[...], wgl_ref[...], preferred_element_type=F32)

    z_ref[...] = jnp.dot(h_scr[...], w_ref[...], preferred_element_type=F32).astype(z_ref.dtype)


def _inproj(x, g, w_main, w_gl):
    T, D = x.shape
    N = w_main.shape[1]
    tm, tn = _tile(T, 1024), _tile(N, 1024)
    est = 2 * tm * D * 4 + tm * D * 2 + 2 * D * tn * 2 + 2 * tm * tn * 2 + 2 * D * LANES * 2 + 2 * tm * LANES * 4
    return pl.pallas_call(
        _inproj_kernel,
        out_shape=(jax.ShapeDtypeStruct((T, N), BF16), jax.ShapeDtypeStruct((T, LANES), F32)),
        grid=(T // tm, N // tn),
        in_specs=[
            pl.BlockSpec((tm, D), lambda i, j: (i, 0)),
            pl.BlockSpec((1, D), lambda i, j: (0, 0)),
            pl.BlockSpec((D, tn), lambda i, j: (0, j)),
            pl.BlockSpec((D, LANES), lambda i, j: (0, 0)),
        ],
        out_specs=(
            pl.BlockSpec((tm, tn), lambda i, j: (i, j)),
            pl.BlockSpec((tm, LANES), lambda i, j: (i, 0)),
        ),
        scratch_shapes=[pltpu.VMEM((tm, D), BF16)],
        compiler_params=_params(est, 2),
        name="inproj",
    )(x, g, w_main, w_gl)


POOL_HALO = 128


def _pool_kernel(u_ref, halo_ref, wp_ref, ps_ref, o_ref):
    i = pl.program_id(0)
    tp = u_ref.shape[0]
    C = wp_ref.shape[1]
    r = lax.broadcasted_iota(jnp.int32, (tp, tp), 0)
    c = lax.broadcasted_iota(jnp.int32, (tp, tp), 1)
    rh = lax.broadcasted_iota(jnp.int32, (tp, POOL_HALO), 0)
    ch = lax.broadcasted_iota(jnp.int32, (tp, POOL_HALO), 1)
    t1 = i * tp + lax.broadcasted_iota(jnp.int32, (tp, 1), 0) + 1
    for gi, w in enumerate(POOL_WINDOWS):
        cols = slice(gi * C, (gi + 1) * C)
        u = u_ref[:, cols]
        halo = halo_ref[:, cols]
        halo = jnp.where(i > 0, halo, jnp.zeros_like(halo))
        band = jnp.where((c <= r) & (c > r - w), 1.0, 0.0).astype(BF16)
        band_h = jnp.where(ch >= rh + (POOL_HALO + 1 - w), 1.0, 0.0).astype(BF16)
        win_sum = (jnp.dot(band, u, preferred_element_type=F32)
                   + jnp.dot(band_h, halo, preferred_element_type=F32))
        count = jnp.minimum(t1, w).astype(F32)
        d = win_sum / count - u.astype(F32)
        y = jnp.dot(d.astype(BF16), wp_ref[gi], preferred_element_type=F32) * ps_ref[:, cols]
        o_ref[:, cols] = y.astype(o_ref.dtype)


def _pool(z, w_pool, pool_scale):
    T = z.shape[0]
    G, C, _ = w_pool.shape
    W = G * C
    tp = _tile(T, 256)
    assert tp % POOL_HALO == 0 and POOL_HALO >= max(POOL_WINDOWS)
    hb = tp // POOL_HALO
    est = 2 * (tp + POOL_HALO) * W * 2 + 2 * G * C * C * 2 + 2 * tp * W * 2 + 4 * tp * tp * 4
    return pl.pallas_call(
        _pool_kernel,
        out_shape=jax.ShapeDtypeStruct((T, W), BF16),
        grid=(T // tp,),
        in_specs=[
            pl.BlockSpec((tp, W), lambda i: (i, 0)),
            pl.BlockSpec((POOL_HALO, W), lambda i: (jnp.maximum(i * hb - 1, 0), 0)),
            pl.BlockSpec((G, C, C), lambda i: (0, 0, 0)),
            pl.BlockSpec((1, W), lambda i: (0, 0)),
        ],
        out_specs=pl.BlockSpec((tp, W), lambda i: (i, 0)),
        compiler_params=_params(est, 1),
        name="pool",
    )(z, z, w_pool, pool_scale)


def _log_sigmoid(x):
    return jnp.minimum(x, 0.0) - jnp.log1p(jnp.exp(-jnp.abs(x)))


def _gla_kernel(q_ref, k_ref, v_ref, r_ref, zg_ref, wgu_ref, bg_ref, gn_ref, o_ref, st_ref, *, dk, dv):
    @pl.when(pl.program_id(0) == 0)
    def _():
        st_ref[...] = jnp.zeros_like(st_ref)

    tg = q_ref.shape[0]
    logit = jnp.dot(zg_ref[...], wgu_ref[...], preferred_element_type=F32, precision=HIGHEST) + bg_ref[...]
    g = _log_sigmoid(logit) * (1.0 / GATE_TAU)
    rr = lax.broadcasted_iota(jnp.int32, (tg, tg), 0)
    cc = lax.broadcasted_iota(jnp.int32, (tg, tg), 1)
    tri = jnp.where((cc <= rr) & (cc // CHUNK == rr // CHUNK), 1.0, 0.0).astype(F32)
    bc = jnp.dot(tri, g, preferred_element_type=F32, precision=HIGHEST)
    ri = lax.broadcasted_iota(jnp.int32, (CHUNK, CHUNK), 0)
    ci = lax.broadcasted_iota(jnp.int32, (CHUNK, CHUNK), 1)
    causal = ci <= ri
    scale = dk ** -0.5
    nt = (((1,), (1,)), ((), ()))
    tn = (((0,), (0,)), ((), ()))
    for c in range(tg // CHUNK):
        rows = slice(c * CHUNK, (c + 1) * CHUNK)
        bcc = bc[rows, :]
        b_last = bcc[CHUNK - 1:CHUNK, :]
        kf = k_ref[rows, :].astype(F32)
        q_dec = (q_ref[rows, :].astype(F32) * scale * jnp.exp(bcc)).astype(BF16)
        k_dec = (kf * jnp.exp(-bcc)).astype(BF16)
        k_end = (kf * jnp.exp(b_last - bcc)).astype(BF16)
        decay = jnp.exp(b_last)
        for h in range(GLA_HEADS):
            ks = slice(h * dk, (h + 1) * dk)
            vs = slice(h * dv, (h + 1) * dv)
            v = v_ref[rows, vs]
            s_t = st_ref[h]
            o_inter = lax.dot_general(q_dec[:, ks], s_t.astype(BF16), nt, preferred_element_type=F32)
            att = lax.dot_general(q_dec[:, ks], k_dec[:, ks], nt, preferred_element_type=F32)
            att = jnp.where(causal, att, 0.0)
            o_intra = jnp.dot(att.astype(BF16), v, preferred_element_type=F32)
            st_ref[h] = s_t * decay[:, ks] + lax.dot_general(v, k_end[:, ks], tn, preferred_element_type=F32)
            o = o_inter + o_intra
            o = o * lax.rsqrt(jnp.mean(o * o, axis=-1, keepdims=True) + EPS) * gn_ref[:, vs]
            o_ref[rows, vs] = (o * _silu(r_ref[rows, vs].astype(F32))).astype(o_ref.dtype)


def _gla(z, zg, wgu, bg, gn, *, q_off, k_off, v_off, r_off, key, width):
    T = z.shape[0]
    dk, dv = key // GLA_HEADS, width // GLA_HEADS
    tg = _tile(T, 256)
    assert tg % CHUNK == 0 and T % CHUNK == 0
    assert q_off % key == 0 and k_off % key == 0 and v_off % width == 0 and r_off % width == 0
    est = 2 * tg * (2 * key + 3 * width) * 2 + 2 * tg * LANES * 4 + GLA_HEADS * dv * dk * 4 + 6 * tg * key * 4
    return pl.pallas_call(
        functools.partial(_gla_kernel, dk=dk, dv=dv),
        out_shape=jax.ShapeDtypeStruct((T, width), BF16),
        grid=(T // tg,),
        in_specs=[
            pl.BlockSpec((tg, key), lambda i: (i, q_off // key)),
            pl.BlockSpec((tg, key), lambda i: (i, k_off // key)),
            pl.BlockSpec((tg, width), lambda i: (i, v_off // width)),
            pl.BlockSpec((tg, width), lambda i: (i, r_off // width)),
            pl.BlockSpec((tg, LANES), lambda i: (i, 0)),
            pl.BlockSpec((LANES, key), lambda i: (0, 0)),
            pl.BlockSpec((1, key), lambda i: (0, 0)),
            pl.BlockSpec((1, width), lambda i: (0, 0)),
        ],
        out_specs=pl.BlockSpec((tg, width), lambda i: (i, 0)),
        scratch_shapes=[pltpu.VMEM((GLA_HEADS, dv, dk), F32)],
        compiler_params=_params(est, 1),
        name="gla",
    )(z, z, z, z, zg, wgu, bg, gn)


def _outproj_kernel(x_ref, p_ref, a_ref, wp_ref, wa_ref, o_ref):
    o_ref[...] = (x_ref[...]
                  + jnp.dot(p_ref[...], wp_ref[...], preferred_element_type=F32)
                  + jnp.dot(a_ref[...], wa_ref[...], preferred_element_type=F32))


def _outproj(x, pool_out, gla_out, w_out):
    T, D = x.shape
    wp_rows, wa_rows = pool_out.shape[1], gla_out.shape[1]
    tm, tn = _tile(T, 1024), _tile(D, 1024)
    assert wp_rows % tn == 0 or wp_rows == w_out.shape[0]
    est = 4 * tm * tn * 4 + 2 * tm * (wp_rows + wa_rows) * 2 + 2 * (wp_rows + wa_rows) * tn * 2
    return pl.pallas_call(
        _outproj_kernel,
        out_shape=jax.ShapeDtypeStruct((T, D), F32),
        grid=(T // tm, D // tn),
        in_specs=[
            pl.BlockSpec((tm, tn), lambda i, j: (i, j)),
            pl.BlockSpec((tm, wp_rows), lambda i, j: (i, 0)),
            pl.BlockSpec((tm, wa_rows), lambda i, j: (i, 0)),
            pl.BlockSpec((wp_rows, tn), lambda i, j: (0, j)),
            pl.BlockSpec((wa_rows, tn), lambda i, j: (wp_rows // wa_rows, j)),
        ],
        out_specs=pl.BlockSpec((tm, tn), lambda i, j: (i, j)),
        compiler_params=_params(est, 2),
        name="outproj",
    )(x, pool_out, gla_out, w_out, w_out)


def _swiglu_up(h_ref, wg_ref, wu_ref):
    h = h_ref[...]
    a = jnp.dot(h, wg_ref[...].astype(BF16), preferred_element_type=F32)
    b = jnp.dot(h, wu_ref[...].astype(BF16), preferred_element_type=F32)
    return (_silu(a) * b).astype(BF16)


def _swiglu_down(t_ref, wd_ref, acc_ref, out_ref=None):
    out_ref = acc_ref if out_ref is None else out_ref
    out_ref[...] = acc_ref[...] + jnp.dot(t_ref[...], wd_ref[...].astype(BF16), preferred_element_type=F32)


def _swiglu_phase(phase, h_ref, t_ref, wg_ref, wu_ref, wd_ref, acc_ref, side_work=None, out_ref=None):
    if phase == "first":
        t_ref[...] = _swiglu_up(h_ref, wg_ref, wu_ref)
    elif phase == "last":
        _swiglu_down(t_ref, wd_ref, acc_ref, out_ref)
    else:
        t_new = _swiglu_up(h_ref, wg_ref, wu_ref)
        if side_work is not None:
            side_work()
        _swiglu_down(t_ref, wd_ref, acc_ref)
        t_ref[...] = t_new


def _ffn_kernel(x_ref, g_ref, wg_ref, wu_ref, wd_ref, fg_ref, o_ref, h_scr, t_scr, *, final_norm):
    f = pl.program_id(1)
    nf = pl.num_programs(1) - 1

    @pl.when(f == 0)
    def _():
        _rmsnorm_rows(x_ref, g_ref, h_scr)
        o_ref[...] = x_ref[...]

    args = (h_scr, t_scr, wg_ref, wu_ref, wd_ref, o_ref)
    pl.when(f == 0)(functools.partial(_swiglu_phase, "first", *args))
    pl.when((f > 0) & (f < nf))(functools.partial(_swiglu_phase, "steady", *args))
    pl.when(f == nf)(functools.partial(_swiglu_phase, "last", *args))

    if final_norm:
        @pl.when(f == nf)
        def _():
            _rmsnorm_rows(o_ref, fg_ref, o_ref)


def _dense_ffn(x, g, wg, wu, wd, fg, final_norm):
    T, D = x.shape
    F = wg.shape[1]
    tm, tf = _tile(T, 1024), _tile(F, 512)
    nf = F // tf
    wbytes = wg.dtype.itemsize
    est = 3 * tm * D * 4 + tm * D * 2 + tm * tf * 2 + 2 * 3 * D * tf * wbytes + 3 * tm * tf * 4
    return pl.pallas_call(
        functools.partial(_ffn_kernel, final_norm=final_norm),
        out_shape=jax.ShapeDtypeStruct((T, D), F32),
        grid=(T // tm, nf + 1),
        in_specs=[
            pl.BlockSpec((tm, D), lambda i, f: (i, 0), pipeline_mode=pl.Buffered(1)),
            pl.BlockSpec((1, D), lambda i, f: (0, 0)),
            pl.BlockSpec((D, tf), lambda i, f: (0, jnp.minimum(f, nf - 1))),
            pl.BlockSpec((D, tf), lambda i, f: (0, jnp.minimum(f, nf - 1))),
            pl.BlockSpec((tf, D), lambda i, f: (jnp.maximum(f - 1, 0), 0)),
            pl.BlockSpec((1, D), lambda i, f: (0, 0)),
        ],
        out_specs=pl.BlockSpec((tm, D), lambda i, f: (i, 0)),
        scratch_shapes=[pltpu.VMEM((tm, D), BF16), pltpu.VMEM((tm, tf), BF16)],
        compiler_params=_params(est, 2),
        name="dense_ffn",
    )(x, g, wg, wu, wd, fg)


LANE_E1, LANE_E2, LANE_RANK1, LANE_RANK2 = 0, 1, 2, 3


def _router_kernel(x_ref, g_ref, wr_ref, mi_ref, mf_ref, cnt_ref, h_scr, run_ref, *, n_experts):
    @pl.when(pl.program_id(0) == 0)
    def _():
        run_ref[...] = jnp.zeros_like(run_ref)

    tm = x_ref.shape[0]
    _rmsnorm_rows(x_ref, g_ref, h_scr)
    logits = jnp.dot(h_scr[...], wr_ref[...], preferred_element_type=F32)
    lane = lax.broadcasted_iota(jnp.int32, (tm, LANES), 1)
    neg = jnp.float32(-jnp.inf)
    l1 = jnp.where(lane < n_experts, logits, neg)
    m1 = jnp.max(l1, axis=-1, keepdims=True)
    e1 = jnp.min(jnp.where(l1 == m1, lane, LANES), axis=-1, keepdims=True)
    l2 = jnp.where(lane == e1, neg, l1)
    m2 = jnp.max(l2, axis=-1, keepdims=True)
    e2 = jnp.min(jnp.where(l2 == m2, lane, LANES), axis=-1, keepdims=True)
    ex = jnp.exp(m2 - m1)
    w1 = 1.0 / (1.0 + ex)
    w2 = ex / (1.0 + ex)
    onehot = jnp.where((lane == e1) | (lane == e2), 1.0, 0.0)
    rr = lax.broadcasted_iota(jnp.int32, (tm, tm), 0)
    cc = lax.broadcasted_iota(jnp.int32, (tm, tm), 1)
    strict = jnp.where(cc < rr, 1.0, 0.0).astype(BF16)
    before = jnp.dot(strict, onehot.astype(BF16), preferred_element_type=F32) + run_ref[...]
    rank1 = jnp.sum(jnp.where(lane == e1, before, 0.0), axis=-1, keepdims=True).astype(jnp.int32)
    rank2 = jnp.sum(jnp.where(lane == e2, before, 0.0), axis=-1, keepdims=True).astype(jnp.int32)
    run_ref[...] += jnp.sum(onehot, axis=0, keepdims=True)
    mi_ref[...] = jnp.where(lane == LANE_E1, e1, jnp.where(lane == LANE_E2, e2, jnp.where(
        lane == LANE_RANK1, rank1, jnp.where(lane == LANE_RANK2, rank2, 0))))
    mf_ref[...] = jnp.where(lane == 0, w1, jnp.where(lane == 1, w2, 0.0))
    cnt_ref[...] = jnp.broadcast_to(run_ref[...], cnt_ref.shape)


def _router(x, g, wr, n_experts):
    T, D = x.shape
    tm = _tile(T, 512)
    est = 2 * tm * D * 4 + tm * D * 2 + 4 * tm * tm * 4 + 2 * D * LANES * 2
    return pl.pallas_call(
        functools.partial(_router_kernel, n_experts=n_experts),
        out_shape=(
            jax.ShapeDtypeStruct((T, LANES), jnp.int32),
            jax.ShapeDtypeStruct((T, LANES), F32),
            jax.ShapeDtypeStruct((8, LANES), F32),
        ),
        grid=(T // tm,),
        in_specs=[
            pl.BlockSpec((tm, D), lambda i: (i, 0)),
            pl.BlockSpec((1, D), lambda i: (0, 0)),
            pl.BlockSpec((D, LANES), lambda i: (0, 0)),
        ],
        out_specs=(
            pl.BlockSpec((tm, LANES), lambda i: (i, 0)),
            pl.BlockSpec((tm, LANES), lambda i: (i, 0)),
            pl.BlockSpec((8, LANES), lambda i: (0, 0)),
        ),
        scratch_shapes=[pltpu.VMEM((tm, D), BF16), pltpu.VMEM((1, LANES), F32)],
        compiler_params=_params(est, 1),
        name="router",
    )(x, g, wr)


def _expert_kernel(ie_ref, ist_ref, inr_ref, nu_ref, cur_tok, nxt_tok, prv_out,
                   x_hbm, g_ref, wg_ref, wu_ref, wd_ref, y_hbm,
                   xs_scr, xb_scr, t_scr, acc_scr, out_scr, gsem, ssem, *, n_tok, rows_per_step, issue_steps):
    del ie_ref, ist_ref
    j, f = pl.program_id(0), pl.program_id(1)
    nf = pl.num_programs(1) - 1
    tm = xs_scr.shape[0]
    half = tm // 2
    n_used = nu_ref[0]
    used = j < n_used

    def gather_copy(tok_ref, r):
        return pltpu.make_async_copy(x_hbm.at[pl.ds(tok_ref[0, r], 1)], xs_scr.at[pl.ds(r, 1)], gsem)

    def scatter_copy(rows_valid, r):
        dst = jnp.where(r < rows_valid, prv_out[0, r], 2 * n_tok + r)
        return pltpu.make_async_copy(out_scr.at[pl.ds(r, 1)], y_hbm.at[pl.ds(dst, 1)], ssem)

    def wait_gather():
        pltpu.make_async_copy(x_hbm.at[pl.ds(0, tm)], xs_scr, gsem).wait()

    def wait_scatter():
        pltpu.make_async_copy(out_scr, y_hbm.at[pl.ds(0, tm)], ssem).wait()

    @pl.when((j == 0) & (f == 0))
    def _():
        def issue(r, carry):
            gather_copy(cur_tok, r).start()
            return carry
        lax.fori_loop(0, tm, issue, 0)
        wait_gather()
        _rmsnorm_rows(xs_scr, g_ref, xb_scr)
        out_scr[...] = jnp.zeros_like(out_scr)

    @pl.when(used & (f == 0))
    def _():
        acc_scr[...] = jnp.zeros_like(acc_scr)

    prev_rows = jnp.where(j > 0, inr_ref[jnp.maximum(j - 1, 0)], 0)

    def side_work():
        base = (f - 1) * rows_per_step
        for rr in range(rows_per_step):
            gather_copy(nxt_tok, base + rr).start()
            scatter_copy(prev_rows, base + rr).start()

    pl.when(used & (f == nf))(wait_scatter)

    def phases(rows):
        args = (xb_scr.at[pl.ds(0, rows)], t_scr.at[pl.ds(0, rows)], wg_ref, wu_ref, wd_ref,
                acc_scr.at[pl.ds(0, rows)])
        steady = (f > 0) & (f < nf)
        pl.when(f == 0)(functools.partial(_swiglu_phase, "first", *args))
        pl.when(steady & (f <= issue_steps))(functools.partial(_swiglu_phase, "steady", *args, side_work=side_work))
        pl.when(steady & (f > issue_steps))(functools.partial(_swiglu_phase, "steady", *args))
        pl.when(f == nf)(functools.partial(_swiglu_phase, "last", *args, out_ref=out_scr.at[pl.ds(0, rows)]))

    pl.when(used & (inr_ref[j] > half))(functools.partial(phases, tm))
    pl.when(used & (inr_ref[j] <= half))(functools.partial(phases, half))

    @pl.when(used & (f == nf))
    def _():
        wait_gather()
        _rmsnorm_rows(xs_scr, g_ref, xb_scr)

    @pl.when((j == n_used) & (f == 0))
    def _():
        def issue(r, carry):
            scatter_copy(prev_rows, r).start()
            return carry
        lax.fori_loop(0, tm, issue, 0)
        wait_scatter()


def _expert_ffn(x, g, tok_win, out_win, item_expert, item_start, item_rows, n_used, wg, wu, wd):
    T, D = x.shape
    E, _, F = wg.shape
    n_items = item_expert.shape[0]
    n_half, _, tm = tok_win.shape
    tf = _tile(F, 256)
    nf = F // tf
    assert nf >= 2
    issue_steps = _tile(tm, nf - 1)
    rows_per_step = tm // issue_steps

    def w_col(j, f, ie, ist, inr, nu):
        return (ie[j], 0, jnp.where(j < nu[0], jnp.minimum(f, nf - 1), nf - 1))

    def w_row(j, f, ie, ist, inr, nu):
        return (ie[j], jnp.where(j < nu[0], jnp.maximum(f - 1, 0), nf - 1), 0)

    def slots_of_item(shift):
        def index_map(j, f, ie, ist, inr, nu):
            return (ist[jnp.clip(j + shift, 0, n_items - 1)], 0, 0)
        return pl.BlockSpec((None, 1, tm), index_map, memory_space=pltpu.SMEM)

    wbytes = wg.dtype.itemsize
    est = (tm * D * 4 + tm * D * 2 + tm * tf * 2 + 2 * 3 * D * tf * wbytes + 2 * tm * D * 4
           + 3 * tm * tf * 4 + 3 * D * tf * 2)
    return pl.pallas_call(
        functools.partial(_expert_kernel, n_tok=T, rows_per_step=rows_per_step, issue_steps=issue_steps),
        out_shape=jax.ShapeDtypeStruct((2 * T + tm, D), F32),
        grid_spec=pltpu.PrefetchScalarGridSpec(
            num_scalar_prefetch=4,
            grid=(n_items, nf + 1),
            in_specs=[
                slots_of_item(0), slots_of_item(1), slots_of_item(-1),
                pl.BlockSpec(memory_space=pl.ANY),
                pl.BlockSpec((1, D), lambda j, f, ie, ist, inr, nu: (0, 0)),
                pl.BlockSpec((None, D, tf), w_col),
                pl.BlockSpec((None, D, tf), w_col),
                pl.BlockSpec((None, tf, D), w_row),
            ],
            out_specs=pl.BlockSpec(memory_space=pl.ANY),
            scratch_shapes=[pltpu.VMEM((tm, D), F32), pltpu.VMEM((tm, D), BF16), pltpu.VMEM((tm, tf), BF16),
                            pltpu.VMEM((tm, D), F32), pltpu.VMEM((tm, D), F32),
                            pltpu.SemaphoreType.DMA, pltpu.SemaphoreType.DMA],
        ),
        compiler_params=_params(est, 2),
        name="expert_ffn",
    )(item_expert, item_start, item_rows, n_used, tok_win, tok_win, out_win, x, g, wg, wu, wd)


def _combine_kernel(x_ref, mf_ref, fg_ref, y1_ref, y2_ref, o_ref, *, final_norm):
    w = mf_ref[...]
    o_ref[...] = x_ref[...] + w[:, 0:1] * y1_ref[...] + w[:, 1:2] * y2_ref[...]
    if final_norm:
        _rmsnorm_rows(o_ref, fg_ref, o_ref)


def _combine(x, mf, y, fg, final_norm):
    T, D = x.shape
    tc = _tile(T, 512)
    est = 8 * tc * D * 4 + 2 * tc * LANES * 4
    return pl.pallas_call(
        functools.partial(_combine_kernel, final_norm=final_norm),
        out_shape=jax.ShapeDtypeStruct((T, D), F32),
        grid=(T // tc,),
        in_specs=[
            pl.BlockSpec((tc, D), lambda i: (i, 0)),
            pl.BlockSpec((tc, LANES), lambda i: (i, 0)),
            pl.BlockSpec((1, D), lambda i: (0, 0)),
            pl.BlockSpec((tc, D), lambda i: (i, 0)),
            pl.BlockSpec((tc, D), lambda i: (i + T // tc, 0)),
        ],
        out_specs=pl.BlockSpec((tc, D), lambda i: (i, 0)),
        compiler_params=_params(est, 1),
        name="combine",
    )(x, mf, fg, y, y)


def _moe_ffn(x, g, w_router, wg, wu, wd, fg, final_norm):
    T, D = x.shape
    E = w_router.shape[1]
    tm = _tile(T, 1024)
    half = tm // 2
    assert (T * TOP_K) % tm == 0 and E <= half
    wr = jnp.pad(w_router, ((0, 0), (0, LANES - E))).astype(BF16)
    mi, mf, cnt = _router(x, g, wr, E)
    counts = cnt[0, :E].astype(jnp.int32)
    padded = ((counts + half - 1) // half) * half
    cum_padded = jnp.cumsum(padded)
    pad_start = cum_padded - padded
    dest1 = pad_start[mi[:, LANE_E1]] + mi[:, LANE_RANK1]
    dest2 = pad_start[mi[:, LANE_E2]] + mi[:, LANE_RANK2]
    n_half = (T * TOP_K) // half + E
    tok = jnp.arange(T, dtype=jnp.int32)
    slot_dst = jnp.full((n_half * half,), -1, jnp.int32).at[jnp.concatenate([dest1, dest2])].set(
        jnp.concatenate([tok, tok + T])).reshape(n_half, half)

    def windows(a):
        return jnp.concatenate([a, jnp.roll(a, -1, axis=0)], axis=1).reshape(n_half, 1, tm)

    tok_win = windows(jnp.where(slot_dst < 0, 0, jnp.where(slot_dst >= T, slot_dst - T, slot_dst)))
    out_win = windows(slot_dst)
    out_win = jnp.where(out_win < 0, 2 * T + jnp.arange(tm, dtype=jnp.int32), out_win)
    n_items = (T * TOP_K) // tm + E
    items_per_expert = (padded + tm - 1) // tm
    cum_items = jnp.cumsum(items_per_expert)
    item = jnp.arange(n_items, dtype=jnp.int32)
    item_expert = jnp.minimum(jnp.sum((item[:, None] >= cum_items[None, :]).astype(jnp.int32), axis=1), E - 1)
    k = item - (cum_items - items_per_expert)[item_expert]
    n_used = cum_items[-1:]
    used = item < n_used[0]
    item_rows = jnp.where(used, jnp.clip(padded[item_expert] - k * tm, 0, tm), 0)
    item_start = jnp.where(used, (pad_start[item_expert] + k * tm) // half, 0)
    y = _expert_ffn(x, g, tok_win, out_win, item_expert, item_start, item_rows, n_used, wg, wu, wd)
    return _combine(x, mf, y, fg, final_norm)


def kernel(x, mix_norm, w_in, w_pool, pool_scale, w_gate_up, b_gate, gla_norm, w_out, ffn_norm,
           dense_w_gate, dense_w_up, dense_w_down, w_router, exp_w_gate, exp_w_up, exp_w_down, final_norm):
    B, S, D = x.shape
    depth = w_in.shape[0]
    G, C = w_pool.shape[1], w_pool.shape[2]
    pool_w = G * C
    rank, key = w_gate_up.shape[1], w_gate_up.shape[2]
    width = gla_norm.shape[1]
    gate_off = pool_w + 2 * key + width
    assert B == 1 and rank <= LANES and w_in.shape[2] == gate_off + rank + width
    xt = x.reshape(S, D)
    fg = final_norm.reshape(1, D)
    for l in range(depth):
        w = w_in[l]
        w_main = jnp.concatenate([w[:, :gate_off], w[:, gate_off + rank:]], axis=1).astype(BF16)
        w_gl = jnp.pad(w[:, gate_off:gate_off + rank], ((0, 0), (0, LANES - rank))).astype(BF16)
        z, zg = _inproj(xt, mix_norm[l].reshape(1, D), w_main, w_gl)
        pool_out = _pool(z, w_pool[l].astype(BF16), pool_scale[l].reshape(1, pool_w))
        wgu = jnp.pad(w_gate_up[l], ((0, LANES - rank), (0, 0)))
        gla_out = _gla(z, zg, wgu, b_gate[l].reshape(1, key), gla_norm[l].reshape(1, width),
                       q_off=pool_w, k_off=pool_w + key, v_off=pool_w + 2 * key, r_off=gate_off,
                       key=key, width=width)
        xt = _outproj(xt, pool_out, gla_out, w_out[l].astype(BF16))
        last = l == depth - 1
        i = l // 2
        if l % 2 == 0:
            xt = _dense_ffn(xt, ffn_norm[l].reshape(1, D), dense_w_gate[i].astype(BF16),
                            dense_w_up[i].astype(BF16), dense_w_down[i].astype(BF16), fg, last)
        else:
            xt = _moe_ffn(xt, ffn_norm[l].reshape(1, D), w_router[i], exp_w_gate[i], exp_w_up[i],
                          exp_w_down[i], fg, last)
    return xt.reshape(B, S, D)
```

```python
import functools

import jax
import jax.numpy as jnp
from jax import lax
from jax.experimental import pallas as pl
from jax.experimental.pallas import tpu as pltpu

EPS = 1e-6
POOL_WINDOWS = (2, 4, 8, 16)
GLA_HEADS = 4
GATE_TAU = 16.0
CHUNK = 64
TOP_K = 2

LANES = 128
V7X_VMEM_BYTES = 64 * 1024 * 1024
VMEM_CAP_BYTES = V7X_VMEM_BYTES - 8 * 1024 * 1024

F32 = jnp.float32
BF16 = jnp.bfloat16
HIGHEST = lax.Precision.HIGHEST


def _tile(n, pref):
    t = min(n, pref)
    while n % t:
        t -= 1
    return t


def _params(vmem_estimate_bytes, n_axes):
    limit = min(VMEM_CAP_BYTES, max(32 * 1024 * 1024, int(vmem_estimate_bytes * 1.25)))
    return pltpu.CompilerParams(
        dimension_semantics=("arbitrary",) * n_axes, vmem_limit_bytes=limit)


def _rmsnorm_rows(x_ref, g_ref, dst_ref):
    rows = x_ref.shape[0]
    chunk = _tile(rows, 128)

    def body(c, carry):
        r0 = pl.multiple_of(c * chunk, chunk)
        x = x_ref[pl.ds(r0, chunk), :]
        ms = jnp.mean(x * x, axis=-1, keepdims=True)
        dst_ref[pl.ds(r0, chunk), :] = (x * lax.rsqrt(ms + EPS) * g_ref[...]).astype(dst_ref.dtype)
        return carry

    lax.fori_loop(0, rows // chunk, body, 0)


def _silu(a):
    return a * (1.0 / (1.0 + jnp.exp(-a)))


def _inproj_kernel(x_ref, g_ref, w_ref, wgl_ref, z_ref, zg_ref, h_scr):
    @pl.when(pl.program_id(1) == 0)
    def _():
        _rmsnorm_rows(x_ref, g_ref, h_scr)
        zg_ref[...] = jnp.dot(h_scr[...], wgl_ref[...], preferred_element_type=F32)

    z_ref[...] = jnp.dot(h_scr[...], w_ref[...], preferred_element_type=F32).astype(z_ref.dtype)


def _inproj(x, g, w_main, w_gl):
    T, D = x.shape
    N = w_main.shape[1]
    tm, tn = _tile(T, 1024), _tile(N, 1024)
    est = 2 * tm * D * 4 + tm * D * 2 + 2 * D * tn * 2 + 2 * tm * tn * 2 + 2 * D * LANES * 2 + 2 * tm * LANES * 4
    return pl.pallas_call(
        _inproj_kernel,
        out_shape=(jax.ShapeDtypeStruct((T, N), BF16), jax.ShapeDtypeStruct((T, LANES), F32)),
        grid=(T // tm, N // tn),
        in_specs=[
            pl.BlockSpec((tm, D), lambda i, j: (i, 0)),
            pl.BlockSpec((1, D), lambda i, j: (0, 0)),
            pl.BlockSpec((D, tn), lambda i, j: (0, j)),
            pl.BlockSpec((D, LANES), lambda i, j: (0, 0)),
        ],
        out_specs=(
            pl.BlockSpec((tm, tn), lambda i, j: (i, j)),
            pl.BlockSpec((tm, LANES), lambda i, j: (i, 0)),
        ),
        scratch_shapes=[pltpu.VMEM((tm, D), BF16)],
        compiler_params=_params(est, 2),
        name="inproj",
    )(x, g, w_main, w_gl)


POOL_HALO = 128


def _pool_kernel(u_ref, halo_ref, wp_ref, ps_ref, o_ref):
    i = pl.program_id(0)
    tp = u_ref.shape[0]
    C = wp_ref.shape[1]
    r = lax.broadcasted_iota(jnp.int32, (tp, tp), 0)
    c = lax.broadcasted_iota(jnp.int32, (tp, tp), 1)
    rh = lax.broadcasted_iota(jnp.int32, (tp, POOL_HALO), 0)
    ch = lax.broadcasted_iota(jnp.int32, (tp, POOL_HALO), 1)
    t1 = i * tp + lax.broadcasted_iota(jnp.int32, (tp, 1), 0) + 1
    for gi, w in enumerate(POOL_WINDOWS):
        cols = slice(gi * C, (gi + 1) * C)
        u = u_ref[:, cols]
        halo = halo_ref[:, cols]
        halo = jnp.where(i > 0, halo, jnp.zeros_like(halo))
        band = jnp.where((c <= r) & (c > r - w), 1.0, 0.0).astype(BF16)
        band_h = jnp.where(ch >= rh + (POOL_HALO + 1 - w), 1.0, 0.0).astype(BF16)
        win_sum = (jnp.dot(band, u, preferred_element_type=F32)
                   + jnp.dot(band_h, halo, preferred_element_type=F32))
        count = jnp.minimum(t1, w).astype(F32)
        d = win_sum / count - u.astype(F32)
        y = jnp.dot(d.astype(BF16), wp_ref[gi], preferred_element_type=F32) * ps_ref[:, cols]
        o_ref[:, cols] = y.astype(o_ref.dtype)


def _pool(z, w_pool, pool_scale):
    T = z.shape[0]
    G, C, _ = w_pool.shape
    W = G * C
    tp = _tile(T, 256)
    assert tp % POOL_HALO == 0 and POOL_HALO >= max(POOL_WINDOWS)
    hb = tp // POOL_HALO
    est = 2 * (tp + POOL_HALO) * W * 2 + 2 * G * C * C * 2 + 2 * tp * W * 2 + 4 * tp * tp * 4
    return pl.pallas_call(
        _pool_kernel,
        out_shape=jax.ShapeDtypeStruct((T, W), BF16),
        grid=(T // tp,),
        in_specs=[
            pl.BlockSpec((tp, W), lambda i: (i, 0)),
            pl.BlockSpec((POOL_HALO, W), lambda i: (jnp.maximum(i * hb - 1, 0), 0)),
            pl.BlockSpec((G, C, C), lambda i: (0, 0, 0)),
            pl.BlockSpec((1, W), lambda i: (0, 0)),
        ],
        out_specs=pl.BlockSpec((tp, W), lambda i: (i, 0)),
        compiler_params=_params(est, 1),
        name="pool",
    )(z, z, w_pool, pool_scale)


def _log_sigmoid(x):
    return jnp.minimum(x, 0.0) - jnp.log(1.0 + jnp.exp(-jnp.abs(x)))


def _gla_gate_logits(zg_ref, wgu_ref, bg_ref):
    return jnp.dot(zg_ref[...].astype(BF16), wgu_ref[...], preferred_element_type=F32) + bg_ref[...]


def _gla_cum_log_decay(logit, tri):
    g = _log_sigmoid(logit) * (1.0 / GATE_TAU)
    g_head = g.astype(BF16)
    g_rest = (g - g_head.astype(F32)).astype(BF16)
    return (jnp.dot(tri, g_head, preferred_element_type=F32)
            + jnp.dot(tri, g_rest, preferred_element_type=F32))


def _gla_kernel(q_ref, k_ref, v_ref, r_ref, zg_ref, zg_next_ref, wgu_ref, bg_ref, gn_ref, tri_ref, o_ref,
                st_ref, bc_ref, bc_next_ref, *, dk, dv):
    @pl.when(pl.program_id(0) == 0)
    def _():
        st_ref[...] = jnp.zeros_like(st_ref)
        bc_next_ref[...] = _gla_cum_log_decay(_gla_gate_logits(zg_ref, wgu_ref, bg_ref), tri_ref[...])

    tg = q_ref.shape[0]
    n_chunks = tg // CHUNK
    heads = range(GLA_HEADS)
    ks = [slice(h * dk, (h + 1) * dk) for h in heads]
    vs = [slice(h * dv, (h + 1) * dv) for h in heads]
    ri = lax.broadcasted_iota(jnp.int32, (CHUNK, CHUNK), 0)
    ci = lax.broadcasted_iota(jnp.int32, (CHUNK, CHUNK), 1)
    causal = ci <= ri
    scale = dk ** -0.5
    nt = (((1,), (1,)), ((), ()))
    tn = (((0,), (0,)), ((), ()))

    bc_ref[...] = bc_next_ref[...]
    logit_next = _gla_gate_logits(zg_next_ref, wgu_ref, bg_ref)

    def stage_a(c):
        rows = slice(c * CHUNK, (c + 1) * CHUNK)
        bcc = bc_ref[rows, :]
        b_last = bcc[CHUNK - 1:CHUNK, :]
        kf = k_ref[rows, :].astype(F32)
        q_dec = (q_ref[rows, :].astype(F32) * scale * jnp.exp(bcc)).astype(BF16)
        k_dec = (kf * jnp.exp(-bcc)).astype(BF16)
        k_end = (kf * jnp.exp(b_last - bcc)).astype(BF16)
        v = [v_ref[rows, vs[h]] for h in heads]
        att = [lax.dot_general(q_dec[:, ks[h]], k_dec[:, ks[h]], nt, preferred_element_type=F32) for h in heads]
        kv = [lax.dot_general(v[h], k_end[:, ks[h]], tn, preferred_element_type=F32) for h in heads]
        return rows, q_dec, jnp.exp(b_last), v, att, kv

    def stage_b(rows, q_dec, decay, v, att, kv):
        s_t = [st_ref[h] for h in heads]
        o_inter = [lax.dot_general(q_dec[:, ks[h]], s_t[h].astype(BF16), nt, preferred_element_type=F32)
                   for h in heads]
        for h in heads:
            st_ref[h] = s_t[h] * decay[:, ks[h]] + kv[h]
        o_intra = [jnp.dot(jnp.where(causal, att[h], 0.0).astype(BF16), v[h], preferred_element_type=F32)
                   for h in heads]
        for h in heads:
            o = o_inter[h] + o_intra[h]
            o = o * lax.rsqrt(jnp.mean(o * o, axis=-1, keepdims=True) + EPS) * gn_ref[:, vs[h]]
            o_ref[rows, vs[h]] = (o * _silu(r_ref[rows, vs[h]].astype(F32))).astype(o_ref.dtype)

    pending = stage_a(0)
    for c in range(1, n_chunks):
        upcoming = stage_a(c)
        if c == 1:
            bc_next_ref[...] = _gla_cum_log_decay(logit_next, tri_ref[...])
        stage_b(*pending)
        pending = upcoming
    if n_chunks == 1:
        bc_next_ref[...] = _gla_cum_log_decay(logit_next, tri_ref[...])
    stage_b(*pending)


def _gla(z, zg, wgu, bg, gn, *, q_off, k_off, v_off, r_off, key, width):
    T = z.shape[0]
    dk, dv = key // GLA_HEADS, width // GLA_HEADS
    tg = _tile(T, 256)
    assert tg % CHUNK == 0 and T % CHUNK == 0
    assert q_off % key == 0 and k_off % key == 0 and v_off % width == 0 and r_off % width == 0
    est = 2 * tg * (2 * key + 3 * width) * 2 + 2 * tg * LANES * 4 + GLA_HEADS * dv * dk * 4 + 6 * tg * key * 4
    row = jnp.arange(tg, dtype=jnp.int32)
    tri = ((row[None, :] <= row[:, None]) & (row[None, :] // CHUNK == row[:, None] // CHUNK)).astype(BF16)
    return pl.pallas_call(
        functools.partial(_gla_kernel, dk=dk, dv=dv),
        out_shape=jax.ShapeDtypeStruct((T, width), BF16),
        grid=(T // tg,),
        in_specs=[
            pl.BlockSpec((tg, key), lambda i: (i, q_off // key)),
            pl.BlockSpec((tg, key), lambda i: (i, k_off // key)),
            pl.BlockSpec((tg, width), lambda i: (i, v_off // width)),
            pl.BlockSpec((tg, width), lambda i: (i, r_off // width)),
            pl.BlockSpec((tg, LANES), lambda i: (i, 0)),
            pl.BlockSpec((tg, LANES), lambda i: (jnp.minimum(i + 1, T // tg - 1), 0)),
            pl.BlockSpec((LANES, key), lambda i: (0, 0)),
            pl.BlockSpec((1, key), lambda i: (0, 0)),
            pl.BlockSpec((1, width), lambda i: (0, 0)),
            pl.BlockSpec((tg, tg), lambda i: (0, 0)),
        ],
        out_specs=pl.BlockSpec((tg, width), lambda i: (i, 0)),
        scratch_shapes=[pltpu.VMEM((GLA_HEADS, dv, dk), F32), pltpu.VMEM((tg, key), F32),
                        pltpu.VMEM((tg, key), F32)],
        compiler_params=_params(est, 1),
        name="gla",
    )(z, z, z, z, zg, zg, wgu, bg, gn, tri)


def _outproj_kernel(x_ref, p_ref, a_ref, wp_ref, wa_ref, o_ref):
    o_ref[...] = (x_ref[...]
                  + jnp.dot(p_ref[...], wp_ref[...], preferred_element_type=F32)
                  + jnp.dot(a_ref[...], wa_ref[...], preferred_element_type=F32))


def _outproj(x, pool_out, gla_out, w_out):
    T, D = x.shape
    wp_rows, wa_rows = pool_out.shape[1], gla_out.shape[1]
    tm, tn = _tile(T, 1024), _tile(D, 1024)
    assert wp_rows % tn == 0 or wp_rows == w_out.shape[0]
    est = 4 * tm * tn * 4 + 2 * tm * (wp_rows + wa_rows) * 2 + 2 * (wp_rows + wa_rows) * tn * 2
    return pl.pallas_call(
        _outproj_kernel,
        out_shape=jax.ShapeDtypeStruct((T, D), F32),
        grid=(T // tm, D // tn),
        in_specs=[
            pl.BlockSpec((tm, tn), lambda i, j: (i, j)),
            pl.BlockSpec((tm, wp_rows), lambda i, j: (i, 0)),
            pl.BlockSpec((tm, wa_rows), lambda i, j: (i, 0)),
            pl.BlockSpec((wp_rows, tn), lambda i, j: (0, j)),
            pl.BlockSpec((wa_rows, tn), lambda i, j: (wp_rows // wa_rows, j)),
        ],
        out_specs=pl.BlockSpec((tm, tn), lambda i, j: (i, j)),
        compiler_params=_params(est, 2),
        name="outproj",
    )(x, pool_out, gla_out, w_out, w_out)


def _swiglu_up(h_ref, wg_ref, wu_ref):
    h = h_ref[...]
    a = jnp.dot(h, wg_ref[...].astype(BF16), preferred_element_type=F32)
    b = jnp.dot(h, wu_ref[...].astype(BF16), preferred_element_type=F32)
    return (_silu(a) * b).astype(BF16)


def _swiglu_down(t_ref, wd_ref, acc_ref, out_ref=None):
    out_ref = acc_ref if out_ref is None else out_ref
    out_ref[...] = acc_ref[...] + jnp.dot(t_ref[...], wd_ref[...].astype(BF16), preferred_element_type=F32)


def _swiglu_phase(phase, h_ref, t_ref, wg_ref, wu_ref, wd_ref, acc_ref, side_work=None, out_ref=None):
    if phase == "first":
        t_ref[...] = _swiglu_up(h_ref, wg_ref, wu_ref)
    elif phase == "last":
        _swiglu_down(t_ref, wd_ref, acc_ref, out_ref)
    else:
        t_new = _swiglu_up(h_ref, wg_ref, wu_ref)
        if side_work is not None:
            side_work()
        _swiglu_down(t_ref, wd_ref, acc_ref)
        t_ref[...] = t_new


def _ffn_kernel(x_ref, g_ref, wg_ref, wu_ref, wd_ref, fg_ref, o_ref, h_scr, t_scr, *, final_norm):
    f = pl.program_id(1)
    nf = pl.num_programs(1) - 1

    @pl.when(f == 0)
    def _():
        _rmsnorm_rows(x_ref, g_ref, h_scr)
        o_ref[...] = x_ref[...]

    args = (h_scr, t_scr, wg_ref, wu_ref, wd_ref, o_ref)
    pl.when(f == 0)(functools.partial(_swiglu_phase, "first", *args))
    pl.when((f > 0) & (f < nf))(functools.partial(_swiglu_phase, "steady", *args))
    pl.when(f == nf)(functools.partial(_swiglu_phase, "last", *args))

    if final_norm:
        @pl.when(f == nf)
        def _():
            _rmsnorm_rows(o_ref, fg_ref, o_ref)


def _dense_ffn(x, g, wg, wu, wd, fg, final_norm):
    T, D = x.shape
    F = wg.shape[1]
    tm, tf = _tile(T, 1024), _tile(F, 512)
    nf = F // tf
    wbytes = wg.dtype.itemsize
    est = 3 * tm * D * 4 + tm * D * 2 + tm * tf * 2 + 2 * 3 * D * tf * wbytes + 3 * tm * tf * 4
    return pl.pallas_call(
        functools.partial(_ffn_kernel, final_norm=final_norm),
        out_shape=jax.ShapeDtypeStruct((T, D), F32),
        grid=(T // tm, nf + 1),
        in_specs=[
            pl.BlockSpec((tm, D), lambda i, f: (i, 0), pipeline_mode=pl.Buffered(1)),
            pl.BlockSpec((1, D), lambda i, f: (0, 0)),
            pl.BlockSpec((D, tf), lambda i, f: (0, jnp.minimum(f, nf - 1))),
            pl.BlockSpec((D, tf), lambda i, f: (0, jnp.minimum(f, nf - 1))),
            pl.BlockSpec((tf, D), lambda i, f: (jnp.maximum(f - 1, 0), 0)),
            pl.BlockSpec((1, D), lambda i, f: (0, 0)),
        ],
        out_specs=pl.BlockSpec((tm, D), lambda i, f: (i, 0)),
        scratch_shapes=[pltpu.VMEM((tm, D), BF16), pltpu.VMEM((tm, tf), BF16)],
        compiler_params=_params(est, 2),
        name="dense_ffn",
    )(x, g, wg, wu, wd, fg)


LANE_E1, LANE_E2, LANE_RANK1, LANE_RANK2 = 0, 1, 2, 3


def _router_kernel(x_ref, g_ref, wr_ref, mi_ref, mf_ref, cnt_ref, h_scr, run_ref, *, n_experts):
    @pl.when(pl.program_id(0) == 0)
    def _():
        run_ref[...] = jnp.zeros_like(run_ref)

    tm = x_ref.shape[0]
    _rmsnorm_rows(x_ref, g_ref, h_scr)
    logits = jnp.dot(h_scr[...], wr_ref[...], preferred_element_type=F32)
    lane = lax.broadcasted_iota(jnp.int32, (tm, LANES), 1)
    neg = jnp.float32(-jnp.inf)
    l1 = jnp.where(lane < n_experts, logits, neg)
    m1 = jnp.max(l1, axis=-1, keepdims=True)
    e1 = jnp.min(jnp.where(l1 == m1, lane, LANES), axis=-1, keepdims=True)
    l2 = jnp.where(lane == e1, neg, l1)
    m2 = jnp.max(l2, axis=-1, keepdims=True)
    e2 = jnp.min(jnp.where(l2 == m2, lane, LANES), axis=-1, keepdims=True)
    ex = jnp.exp(m2 - m1)
    w1 = 1.0 / (1.0 + ex)
    w2 = ex / (1.0 + ex)
    onehot = jnp.where((lane == e1) | (lane == e2), 1.0, 0.0)
    rr = lax.broadcasted_iota(jnp.int32, (tm, tm), 0)
    cc = lax.broadcasted_iota(jnp.int32, (tm, tm), 1)
    strict = jnp.where(cc < rr, 1.0, 0.0).astype(BF16)
    before = jnp.dot(strict, onehot.astype(BF16), preferred_element_type=F32) + run_ref[...]
    rank1 = jnp.sum(jnp.where(lane == e1, before, 0.0), axis=-1, keepdims=True).astype(jnp.int32)
    rank2 = jnp.sum(jnp.where(lane == e2, before, 0.0), axis=-1, keepdims=True).astype(jnp.int32)
    run_ref[...] += jnp.sum(onehot, axis=0, keepdims=True)
    mi_ref[...] = jnp.where(lane == LANE_E1, e1, jnp.where(lane == LANE_E2, e2, jnp.where(
        lane == LANE_RANK1, rank1, jnp.where(lane == LANE_RANK2, rank2, 0))))
    mf_ref[...] = jnp.where(lane == 0, w1, jnp.where(lane == 1, w2, 0.0))
    cnt_ref[...] = jnp.broadcast_to(run_ref[...], cnt_ref.shape)


def _router(x, g, wr, n_experts):
    T, D = x.shape
    tm = _tile(T, 512)
    est = 2 * tm * D * 4 + tm * D * 2 + 4 * tm * tm * 4 + 2 * D * LANES * 2
    return pl.pallas_call(
        functools.partial(_router_kernel, n_experts=n_experts),
        out_shape=(
            jax.ShapeDtypeStruct((T, LANES), jnp.int32),
            jax.ShapeDtypeStruct((T, LANES), F32),
            jax.ShapeDtypeStruct((8, LANES), F32),
        ),
        grid=(T // tm,),
        in_specs=[
            pl.BlockSpec((tm, D), lambda i: (i, 0)),
            pl.BlockSpec((1, D), lambda i: (0, 0)),
            pl.BlockSpec((D, LANES), lambda i: (0, 0)),
        ],
        out_specs=(
            pl.BlockSpec((tm, LANES), lambda i: (i, 0)),
            pl.BlockSpec((tm, LANES), lambda i: (i, 0)),
            pl.BlockSpec((8, LANES), lambda i: (0, 0)),
        ),
        scratch_shapes=[pltpu.VMEM((tm, D), BF16), pltpu.VMEM((1, LANES), F32)],
        compiler_params=_params(est, 1),
        name="router",
    )(x, g, wr)


def _expert_kernel(ie_ref, ist_ref, inr_ref, nu_ref, cur_tok, nxt_tok, prv_out,
                   x_hbm, g_ref, wg_ref, wu_ref, wd_ref, y_hbm,
                   xs_scr, xb_scr, t_scr, acc_scr, out_scr, gsem, ssem, *, n_tok, rows_per_step, issue_steps):
    del ie_ref, ist_ref
    j, f = pl.program_id(0), pl.program_id(1)
    nf = pl.num_programs(1) - 1
    tm = xs_scr.shape[0]
    half = tm // 2
    n_used = nu_ref[0]
    used = j < n_used

    def gather_copy(tok_ref, r):
        return pltpu.make_async_copy(x_hbm.at[pl.ds(tok_ref[0, r], 1)], xs_scr.at[pl.ds(r, 1)], gsem)

    def scatter_copy(rows_valid, r):
        dst = jnp.where(r < rows_valid, prv_out[0, r], 2 * n_tok + r)
        return pltpu.make_async_copy(out_scr.at[pl.ds(r, 1)], y_hbm.at[pl.ds(dst, 1)], ssem)

    def wait_gather():
        pltpu.make_async_copy(x_hbm.at[pl.ds(0, tm)], xs_scr, gsem).wait()

    def wait_scatter():
        pltpu.make_async_copy(out_scr, y_hbm.at[pl.ds(0, tm)], ssem).wait()

    @pl.when((j == 0) & (f == 0))
    def _():
        def issue(r, carry):
            gather_copy(cur_tok, r).start()
            return carry
        lax.fori_loop(0, tm, issue, 0)
        wait_gather()
        _rmsnorm_rows(xs_scr, g_ref, xb_scr)
        out_scr[...] = jnp.zeros_like(out_scr)

    @pl.when(used & (f == 0))
    def _():
        acc_scr[...] = jnp.zeros_like(acc_scr)

    prev_rows = jnp.where(j > 0, inr_ref[jnp.maximum(j - 1, 0)], 0)

    def side_work():
        base = (f - 1) * rows_per_step
        for rr in range(rows_per_step):
            gather_copy(nxt_tok, base + rr).start()
            scatter_copy(prev_rows, base + rr).start()

    pl.when(used & (f == nf))(wait_scatter)

    def phases(rows):
        args = (xb_scr.at[pl.ds(0, rows)], t_scr.at[pl.ds(0, rows)], wg_ref, wu_ref, wd_ref,
                acc_scr.at[pl.ds(0, rows)])
        steady = (f > 0) & (f < nf)
        pl.when(f == 0)(functools.partial(_swiglu_phase, "first", *args))
        pl.when(steady & (f <= issue_steps))(functools.partial(_swiglu_phase, "steady", *args, side_work=side_work))
        pl.when(steady & (f > issue_steps))(functools.partial(_swiglu_phase, "steady", *args))
        pl.when(f == nf)(functools.partial(_swiglu_phase, "last", *args, out_ref=out_scr.at[pl.ds(0, rows)]))

    pl.when(used & (inr_ref[j] > half))(functools.partial(phases, tm))
    pl.when(used & (inr_ref[j] <= half))(functools.partial(phases, half))

    @pl.when(used & (f == nf))
    def _():
        wait_gather()
        _rmsnorm_rows(xs_scr, g_ref, xb_scr)

    @pl.when((j == n_used) & (f == 0))
    def _():
        def issue(r, carry):
            scatter_copy(prev_rows, r).start()
            return carry
        lax.fori_loop(0, tm, issue, 0)
        wait_scatter()


def _expert_ffn(x, g, tok_win, out_win, item_expert, item_start, item_rows, n_used, wg, wu, wd):
    T, D = x.shape
    E, _, F = wg.shape
    n_items = item_expert.shape[0]
    n_half, _, tm = tok_win.shape
    tf = _tile(F, 256)
    nf = F // tf
    assert nf >= 2
    issue_steps = _tile(tm, nf - 1)
    rows_per_step = tm // issue_steps

    def w_col(j, f, ie, ist, inr, nu):
        return (ie[j], 0, jnp.where(j < nu[0], jnp.minimum(f, nf - 1), nf - 1))

    def w_row(j, f, ie, ist, inr, nu):
        return (ie[j], jnp.where(j < nu[0], jnp.maximum(f - 1, 0), nf - 1), 0)

    def slots_of_item(shift):
        def index_map(j, f, ie, ist, inr, nu):
            return (ist[jnp.clip(j + shift, 0, n_items - 1)], 0, 0)
        return pl.BlockSpec((None, 1, tm), index_map, memory_space=pltpu.SMEM)

    wbytes = wg.dtype.itemsize
    est = (tm * D * 4 + tm * D * 2 + tm * tf * 2 + 2 * 3 * D * tf * wbytes + 2 * tm * D * 4
           + 3 * tm * tf * 4 + 3 * D * tf * 2)
    return pl.pallas_call(
        functools.partial(_expert_kernel, n_tok=T, rows_per_step=rows_per_step, issue_steps=issue_steps),
        out_shape=jax.ShapeDtypeStruct((2 * T + tm, D), F32),
        grid_spec=pltpu.PrefetchScalarGridSpec(
            num_scalar_prefetch=4,
            grid=(n_items, nf + 1),
            in_specs=[
                slots_of_item(0), slots_of_item(1), slots_of_item(-1),
                pl.BlockSpec(memory_space=pl.ANY),
                pl.BlockSpec((1, D), lambda j, f, ie, ist, inr, nu: (0, 0)),
                pl.BlockSpec((None, D, tf), w_col),
                pl.BlockSpec((None, D, tf), w_col),
                pl.BlockSpec((None, tf, D), w_row),
            ],
            out_specs=pl.BlockSpec(memory_space=pl.ANY),
            scratch_shapes=[pltpu.VMEM((tm, D), F32), pltpu.VMEM((tm, D), BF16), pltpu.VMEM((tm, tf), BF16),
                            pltpu.VMEM((tm, D), F32), pltpu.VMEM((tm, D), F32),
                            pltpu.SemaphoreType.DMA, pltpu.SemaphoreType.DMA],
        ),
        compiler_params=_params(est, 2),
        name="expert_ffn",
    )(item_expert, item_start, item_rows, n_used, tok_win, tok_win, out_win, x, g, wg, wu, wd)


def _combine_kernel(x_ref, mf_ref, fg_ref, y1_ref, y2_ref, o_ref, *, final_norm):
    w = mf_ref[...]
    o_ref[...] = x_ref[...] + w[:, 0:1] * y1_ref[...] + w[:, 1:2] * y2_ref[...]
    if final_norm:
        _rmsnorm_rows(o_ref, fg_ref, o_ref)


def _combine(x, mf, y, fg, final_norm):
    T, D = x.shape
    tc = _tile(T, 512)
    est = 8 * tc * D * 4 + 2 * tc * LANES * 4
    return pl.pallas_call(
        functools.partial(_combine_kernel, final_norm=final_norm),
        out_shape=jax.ShapeDtypeStruct((T, D), F32),
        grid=(T // tc,),
        in_specs=[
            pl.BlockSpec((tc, D), lambda i: (i, 0)),
            pl.BlockSpec((tc, LANES), lambda i: (i, 0)),
            pl.BlockSpec((1, D), lambda i: (0, 0)),
            pl.BlockSpec((tc, D), lambda i: (i, 0)),
            pl.BlockSpec((tc, D), lambda i: (i + T // tc, 0)),
        ],
        out_specs=pl.BlockSpec((tc, D), lambda i: (i, 0)),
        compiler_params=_params(est, 1),
        name="combine",
    )(x, mf, fg, y, y)


def _moe_ffn(x, g, w_router, wg, wu, wd, fg, final_norm):
    T, D = x.shape
    E = w_router.shape[1]
    tm = _tile(T, 1024)
    half = tm // 2
    assert (T * TOP_K) % tm == 0 and E <= half
    wr = jnp.pad(w_router, ((0, 0), (0, LANES - E))).astype(BF16)
    mi, mf, cnt = _router(x, g, wr, E)
    counts = cnt[0, :E].astype(jnp.int32)
    padded = ((counts + half - 1) // half) * half
    cum_padded = jnp.cumsum(padded)
    pad_start = cum_padded - padded
    dest1 = pad_start[mi[:, LANE_E1]] + mi[:, LANE_RANK1]
    dest2 = pad_start[mi[:, LANE_E2]] + mi[:, LANE_RANK2]
    n_half = (T * TOP_K) // half + E
    tok = jnp.arange(T, dtype=jnp.int32)
    slot_dst = jnp.full((n_half * half,), -1, jnp.int32).at[jnp.concatenate([dest1, dest2])].set(
        jnp.concatenate([tok, tok + T])).reshape(n_half, half)

    def windows(a):
        return jnp.concatenate([a, jnp.roll(a, -1, axis=0)], axis=1).reshape(n_half, 1, tm)

    tok_win = windows(jnp.where(slot_dst < 0, 0, jnp.where(slot_dst >= T, slot_dst - T, slot_dst)))
    out_win = windows(slot_dst)
    out_win = jnp.where(out_win < 0, 2 * T + jnp.arange(tm, dtype=jnp.int32), out_win)
    n_items = (T * TOP_K) // tm + E
    items_per_expert = (padded + tm - 1) // tm
    cum_items = jnp.cumsum(items_per_expert)
    item = jnp.arange(n_items, dtype=jnp.int32)
    item_expert = jnp.minimum(jnp.sum((item[:, None] >= cum_items[None, :]).astype(jnp.int32), axis=1), E - 1)
    k = item - (cum_items - items_per_expert)[item_expert]
    n_used = cum_items[-1:]
    used = item < n_used[0]
    item_rows = jnp.where(used, jnp.clip(padded[item_expert] - k * tm, 0, tm), 0)
    item_start = jnp.where(used, (pad_start[item_expert] + k * tm) // half, 0)
    y = _expert_ffn(x, g, tok_win, out_win, item_expert, item_start, item_rows, n_used, wg, wu, wd)
    return _combine(x, mf, y, fg, final_norm)


def kernel(x, mix_norm, w_in, w_pool, pool_scale, w_gate_up, b_gate, gla_norm, w_out, ffn_norm,
           dense_w_gate, dense_w_up, dense_w_down, w_router, exp_w_gate, exp_w_up, exp_w_down, final_norm):
    B, S, D = x.shape
    depth = w_in.shape[0]
    G, C = w_pool.shape[1], w_pool.shape[2]
    pool_w = G * C
    rank, key = w_gate_up.shape[1], w_gate_up.shape[2]
    width = gla_norm.shape[1]
    gate_off = pool_w + 2 * key + width
    assert B == 1 and rank <= LANES and w_in.shape[2] == gate_off + rank + width
    xt = x.reshape(S, D)
    fg = final_norm.reshape(1, D)
    for l in range(depth):
        w = w_in[l]
        w_main = jnp.concatenate([w[:, :gate_off], w[:, gate_off + rank:]], axis=1).astype(BF16)
        w_gl = jnp.pad(w[:, gate_off:gate_off + rank], ((0, 0), (0, LANES - rank))).astype(BF16)
        z, zg = _inproj(xt, mix_norm[l].reshape(1, D), w_main, w_gl)
        pool_out = _pool(z, w_pool[l].astype(BF16), pool_scale[l].reshape(1, pool_w))
        wgu = jnp.pad(w_gate_up[l], ((0, LANES - rank), (0, 0))).astype(BF16)
        gla_out = _gla(z, zg, wgu, b_gate[l].reshape(1, key), gla_norm[l].reshape(1, width),
                       q_off=pool_w, k_off=pool_w + key, v_off=pool_w + 2 * key, r_off=gate_off,
                       key=key, width=width)
        xt = _outproj(xt, pool_out, gla_out, w_out[l].astype(BF16))
        last = l == depth - 1
        i = l // 2
        if l % 2 == 0:
            xt = _dense_ffn(xt, ffn_norm[l].reshape(1, D), dense_w_gate[i].astype(BF16),
                            dense_w_up[i].astype(BF16), dense_w_down[i].astype(BF16), fg, last)
        else:
            xt = _moe_ffn(xt, ffn_norm[l].reshape(1, D), w_router[i], exp_w_gate[i], exp_w_up[i],
                          exp_w_down[i], fg, last)
    return xt.reshape(B, S, D)
```

```python
import functools

import jax
import jax.numpy as jnp
from jax import lax
from jax.experimental import pallas as pl
from jax.experimental.pallas import tpu as pltpu

EPS = 1e-6
POOL_WINDOWS = (2, 4, 8, 16)
GLA_HEADS = 4
GATE_TAU = 16.0
CHUNK = 64
TOP_K = 2

LANES = 128
V7X_VMEM_BYTES = 64 * 1024 * 1024
VMEM_CAP_BYTES = V7X_VMEM_BYTES - 2 * 1024 * 1024

F32 = jnp.float32
BF16 = jnp.bfloat16
HIGHEST = lax.Precision.HIGHEST


def _tile(n, pref):
    t = min(n, pref)
    while n % t:
        t -= 1
    return t


def _params(vmem_estimate_bytes, n_axes):
    limit = min(VMEM_CAP_BYTES, max(32 * 1024 * 1024, int(vmem_estimate_bytes * 1.25)))
    return pltpu.CompilerParams(
        dimension_semantics=("arbitrary",) * n_axes, vmem_limit_bytes=limit)


def _rmsnorm_rows(x_ref, g_ref, dst_ref):
    rows = x_ref.shape[0]
    chunk = _tile(rows, 128)

    def body(c, carry):
        r0 = pl.multiple_of(c * chunk, chunk)
        x = x_ref[pl.ds(r0, chunk), :]
        ms = jnp.mean(x * x, axis=-1, keepdims=True)
        dst_ref[pl.ds(r0, chunk), :] = (x * lax.rsqrt(ms + EPS) * g_ref[...]).astype(dst_ref.dtype)
        return carry

    lax.fori_loop(0, rows // chunk, body, 0)


def _silu(a):
    return a * (1.0 / (1.0 + jnp.exp(-a)))


def _inproj_kernel(x_ref, g_ref, w_ref, wgl_ref, z_ref, zg_ref, h_scr):
    @pl.when(pl.program_id(1) == 0)
    def _():
        _rmsnorm_rows(x_ref, g_ref, h_scr)
        zg_ref[...] = jnp.dot(h_scr[...], wgl_ref[...], preferred_element_type=F32)

    z_ref[...] = jnp.dot(h_scr[...], w_ref[...], preferred_element_type=F32).astype(z_ref.dtype)


def _inproj(x, g, w_main, w_gl):
    T, D = x.shape
    N = w_main.shape[1]
    tm, tn = _tile(T, 1024), _tile(N, 1024)
    est = 2 * tm * D * 4 + tm * D * 2 + 2 * D * tn * 2 + 2 * tm * tn * 2 + 2 * D * LANES * 2 + 2 * tm * LANES * 4
    return pl.pallas_call(
        _inproj_kernel,
        out_shape=(jax.ShapeDtypeStruct((T, N), BF16), jax.ShapeDtypeStruct((T, LANES), F32)),
        grid=(T // tm, N // tn),
        in_specs=[
            pl.BlockSpec((tm, D), lambda i, j: (i, 0)),
            pl.BlockSpec((1, D), lambda i, j: (0, 0)),
            pl.BlockSpec((D, tn), lambda i, j: (0, j)),
            pl.BlockSpec((D, LANES), lambda i, j: (0, 0)),
        ],
        out_specs=(
            pl.BlockSpec((tm, tn), lambda i, j: (i, j)),
            pl.BlockSpec((tm, LANES), lambda i, j: (i, 0)),
        ),
        scratch_shapes=[pltpu.VMEM((tm, D), BF16)],
        compiler_params=_params(est, 2),
        name="inproj",
    )(x, g, w_main, w_gl)


POOL_HALO = 128


def _pool_kernel(u_ref, halo_ref, wp_ref, ps_ref, o_ref):
    i = pl.program_id(0)
    tp = u_ref.shape[0]
    C = wp_ref.shape[1]
    r = lax.broadcasted_iota(jnp.int32, (tp, tp), 0)
    c = lax.broadcasted_iota(jnp.int32, (tp, tp), 1)
    rh = lax.broadcasted_iota(jnp.int32, (tp, POOL_HALO), 0)
    ch = lax.broadcasted_iota(jnp.int32, (tp, POOL_HALO), 1)
    t1 = i * tp + lax.broadcasted_iota(jnp.int32, (tp, 1), 0) + 1
    for gi, w in enumerate(POOL_WINDOWS):
        cols = slice(gi * C, (gi + 1) * C)
        u = u_ref[:, cols]
        halo = halo_ref[:, cols]
        halo = jnp.where(i > 0, halo, jnp.zeros_like(halo))
        band = jnp.where((c <= r) & (c > r - w), 1.0, 0.0).astype(BF16)
        band_h = jnp.where(ch >= rh + (POOL_HALO + 1 - w), 1.0, 0.0).astype(BF16)
        win_sum = (jnp.dot(band, u, preferred_element_type=F32)
                   + jnp.dot(band_h, halo, preferred_element_type=F32))
        count = jnp.minimum(t1, w).astype(F32)
        d = win_sum / count - u.astype(F32)
        y = jnp.dot(d.astype(BF16), wp_ref[gi], preferred_element_type=F32) * ps_ref[:, cols]
        o_ref[:, cols] = y.astype(o_ref.dtype)


def _pool(z, w_pool, pool_scale):
    T = z.shape[0]
    G, C, _ = w_pool.shape
    W = G * C
    tp = _tile(T, 256)
    assert tp % POOL_HALO == 0 and POOL_HALO >= max(POOL_WINDOWS)
    hb = tp // POOL_HALO
    est = 2 * (tp + POOL_HALO) * W * 2 + 2 * G * C * C * 2 + 2 * tp * W * 2 + 4 * tp * tp * 4
    return pl.pallas_call(
        _pool_kernel,
        out_shape=jax.ShapeDtypeStruct((T, W), BF16),
        grid=(T // tp,),
        in_specs=[
            pl.BlockSpec((tp, W), lambda i: (i, 0)),
            pl.BlockSpec((POOL_HALO, W), lambda i: (jnp.maximum(i * hb - 1, 0), 0)),
            pl.BlockSpec((G, C, C), lambda i: (0, 0, 0)),
            pl.BlockSpec((1, W), lambda i: (0, 0)),
        ],
        out_specs=pl.BlockSpec((tp, W), lambda i: (i, 0)),
        compiler_params=_params(est, 1),
        name="pool",
    )(z, z, w_pool, pool_scale)


def _log_sigmoid(x):
    return jnp.minimum(x, 0.0) - jnp.log(1.0 + jnp.exp(-jnp.abs(x)))


def _gla_gate_logits(zg_ref, wgu_ref, bg_ref):
    return jnp.dot(zg_ref[...].astype(BF16), wgu_ref[...], preferred_element_type=F32) + bg_ref[...]


def _gla_cum_log_decay(logit, tri):
    g = _log_sigmoid(logit) * (1.0 / GATE_TAU)
    g_head = g.astype(BF16)
    g_rest = (g - g_head.astype(F32)).astype(BF16)
    return (jnp.dot(tri, g_head, preferred_element_type=F32)
            + jnp.dot(tri, g_rest, preferred_element_type=F32))


def _gla_kernel(q_ref, k_ref, v_ref, r_ref, zg_ref, zg_next_ref, wgu_ref, bg_ref, gn_ref, tri_ref, o_ref,
                st_ref, bc_ref, bc_next_ref, *, dk, dv):
    @pl.when(pl.program_id(0) == 0)
    def _():
        st_ref[...] = jnp.zeros_like(st_ref)
        bc_next_ref[...] = _gla_cum_log_decay(_gla_gate_logits(zg_ref, wgu_ref, bg_ref), tri_ref[...])

    tg = q_ref.shape[0]
    n_chunks = tg // CHUNK
    heads = range(GLA_HEADS)
    ks = [slice(h * dk, (h + 1) * dk) for h in heads]
    vs = [slice(h * dv, (h + 1) * dv) for h in heads]
    ri = lax.broadcasted_iota(jnp.int32, (CHUNK, CHUNK), 0)
    ci = lax.broadcasted_iota(jnp.int32, (CHUNK, CHUNK), 1)
    causal = ci <= ri
    scale = dk ** -0.5
    nt = (((1,), (1,)), ((), ()))
    tn = (((0,), (0,)), ((), ()))

    bc_ref[...] = bc_next_ref[...]
    logit_next = _gla_gate_logits(zg_next_ref, wgu_ref, bg_ref)

    def stage_a(c):
        rows = slice(c * CHUNK, (c + 1) * CHUNK)
        bcc = bc_ref[rows, :]
        b_last = bcc[CHUNK - 1:CHUNK, :]
        kf = k_ref[rows, :].astype(F32)
        q_dec = (q_ref[rows, :].astype(F32) * scale * jnp.exp(bcc)).astype(BF16)
        k_dec = (kf * jnp.exp(-bcc)).astype(BF16)
        k_end = (kf * jnp.exp(b_last - bcc)).astype(BF16)
        v = [v_ref[rows, vs[h]] for h in heads]
        att = [lax.dot_general(q_dec[:, ks[h]], k_dec[:, ks[h]], nt, preferred_element_type=F32) for h in heads]
        kv = [lax.dot_general(v[h], k_end[:, ks[h]], tn, preferred_element_type=F32) for h in heads]
        return rows, q_dec, jnp.exp(b_last), v, att, kv

    def stage_b(rows, q_dec, decay, v, att, kv):
        s_t = [st_ref[h] for h in heads]
        o_inter = [lax.dot_general(q_dec[:, ks[h]], s_t[h].astype(BF16), nt, preferred_element_type=F32)
                   for h in heads]
        for h in heads:
            st_ref[h] = s_t[h] * decay[:, ks[h]] + kv[h]
        o_intra = [jnp.dot(jnp.where(causal, att[h], 0.0).astype(BF16), v[h], preferred_element_type=F32)
                   for h in heads]
        for h in heads:
            o = o_inter[h] + o_intra[h]
            o = o * lax.rsqrt(jnp.mean(o * o, axis=-1, keepdims=True) + EPS) * gn_ref[:, vs[h]]
            o_ref[rows, vs[h]] = (o * _silu(r_ref[rows, vs[h]].astype(F32))).astype(o_ref.dtype)

    pending = stage_a(0)
    for c in range(1, n_chunks):
        upcoming = stage_a(c)
        if c == 1:
            bc_next_ref[...] = _gla_cum_log_decay(logit_next, tri_ref[...])
        stage_b(*pending)
        pending = upcoming
    if n_chunks == 1:
        bc_next_ref[...] = _gla_cum_log_decay(logit_next, tri_ref[...])
    stage_b(*pending)


def _gla(z, zg, wgu, bg, gn, *, q_off, k_off, v_off, r_off, key, width):
    T = z.shape[0]
    dk, dv = key // GLA_HEADS, width // GLA_HEADS
    tg = _tile(T, 256)
    assert tg % CHUNK == 0 and T % CHUNK == 0
    assert q_off % key == 0 and k_off % key == 0 and v_off % width == 0 and r_off % width == 0
    est = 2 * tg * (2 * key + 3 * width) * 2 + 2 * tg * LANES * 4 + GLA_HEADS * dv * dk * 4 + 6 * tg * key * 4
    row = jnp.arange(tg, dtype=jnp.int32)
    tri = ((row[None, :] <= row[:, None]) & (row[None, :] // CHUNK == row[:, None] // CHUNK)).astype(BF16)
    return pl.pallas_call(
        functools.partial(_gla_kernel, dk=dk, dv=dv),
        out_shape=jax.ShapeDtypeStruct((T, width), BF16),
        grid=(T // tg,),
        in_specs=[
            pl.BlockSpec((tg, key), lambda i: (i, q_off // key)),
            pl.BlockSpec((tg, key), lambda i: (i, k_off // key)),
            pl.BlockSpec((tg, width), lambda i: (i, v_off // width)),
            pl.BlockSpec((tg, width), lambda i: (i, r_off // width)),
            pl.BlockSpec((tg, LANES), lambda i: (i, 0)),
            pl.BlockSpec((tg, LANES), lambda i: (jnp.minimum(i + 1, T // tg - 1), 0)),
            pl.BlockSpec((LANES, key), lambda i: (0, 0)),
            pl.BlockSpec((1, key), lambda i: (0, 0)),
            pl.BlockSpec((1, width), lambda i: (0, 0)),
            pl.BlockSpec((tg, tg), lambda i: (0, 0)),
        ],
        out_specs=pl.BlockSpec((tg, width), lambda i: (i, 0)),
        scratch_shapes=[pltpu.VMEM((GLA_HEADS, dv, dk), F32), pltpu.VMEM((tg, key), F32),
                        pltpu.VMEM((tg, key), F32)],
        compiler_params=_params(est, 1),
        name="gla",
    )(z, z, z, z, zg, zg, wgu, bg, gn, tri)


def _outproj_kernel(x_ref, p_ref, a_ref, wp_ref, wa_ref, o_ref):
    o_ref[...] = (x_ref[...]
                  + jnp.dot(p_ref[...], wp_ref[...], preferred_element_type=F32)
                  + jnp.dot(a_ref[...], wa_ref[...], preferred_element_type=F32))


def _outproj(x, pool_out, gla_out, w_out):
    T, D = x.shape
    wp_rows, wa_rows = pool_out.shape[1], gla_out.shape[1]
    tm, tn = _tile(T, 1024), _tile(D, 1024)
    assert wp_rows % tn == 0 or wp_rows == w_out.shape[0]
    est = 4 * tm * tn * 4 + 2 * tm * (wp_rows + wa_rows) * 2 + 2 * (wp_rows + wa_rows) * tn * 2
    return pl.pallas_call(
        _outproj_kernel,
        out_shape=jax.ShapeDtypeStruct((T, D), F32),
        grid=(T // tm, D // tn),
        in_specs=[
            pl.BlockSpec((tm, tn), lambda i, j: (i, j)),
            pl.BlockSpec((tm, wp_rows), lambda i, j: (i, 0)),
            pl.BlockSpec((tm, wa_rows), lambda i, j: (i, 0)),
            pl.BlockSpec((wp_rows, tn), lambda i, j: (0, j)),
            pl.BlockSpec((wa_rows, tn), lambda i, j: (wp_rows // wa_rows, j)),
        ],
        out_specs=pl.BlockSpec((tm, tn), lambda i, j: (i, j)),
        compiler_params=_params(est, 2),
        name="outproj",
    )(x, pool_out, gla_out, w_out, w_out)


def _swiglu_up(h_ref, wg_ref, wu_ref):
    h = h_ref[...]
    a = jnp.dot(h, wg_ref[...].astype(BF16), preferred_element_type=F32)
    b = jnp.dot(h, wu_ref[...].astype(BF16), preferred_element_type=F32)
    return (_silu(a) * b).astype(BF16)


def _swiglu_down(t_ref, wd_ref, acc_ref, out_ref=None):
    out_ref = acc_ref if out_ref is None else out_ref
    out_ref[...] = acc_ref[...] + jnp.dot(t_ref[...], wd_ref[...].astype(BF16), preferred_element_type=F32)


def _swiglu_phase(phase, h_ref, t_ref, wg_ref, wu_ref, wd_ref, acc_ref, side_work=None, out_ref=None):
    if phase == "first":
        t_ref[...] = _swiglu_up(h_ref, wg_ref, wu_ref)
    elif phase == "last":
        _swiglu_down(t_ref, wd_ref, acc_ref, out_ref)
    else:
        t_new = _swiglu_up(h_ref, wg_ref, wu_ref)
        if side_work is not None:
            side_work()
        _swiglu_down(t_ref, wd_ref, acc_ref)
        t_ref[...] = t_new


def _ffn_kernel(x_ref, g_ref, wg_ref, wu_ref, wd_ref, fg_ref, o_ref, h_scr, t_scr, *, final_norm):
    f = pl.program_id(1)
    nf = pl.num_programs(1) - 1

    @pl.when(f == 0)
    def _():
        _rmsnorm_rows(x_ref, g_ref, h_scr)
        o_ref[...] = x_ref[...]

    args = (h_scr, t_scr, wg_ref, wu_ref, wd_ref, o_ref)
    pl.when(f == 0)(functools.partial(_swiglu_phase, "first", *args))
    pl.when((f > 0) & (f < nf))(functools.partial(_swiglu_phase, "steady", *args))
    pl.when(f == nf)(functools.partial(_swiglu_phase, "last", *args))

    if final_norm:
        @pl.when(f == nf)
        def _():
            _rmsnorm_rows(o_ref, fg_ref, o_ref)


def _dense_ffn(x, g, wg, wu, wd, fg, final_norm):
    T, D = x.shape
    F = wg.shape[1]
    tm, tf = _tile(T, 1024), _tile(F, 512)
    nf = F // tf
    wbytes = wg.dtype.itemsize
    est = 3 * tm * D * 4 + tm * D * 2 + tm * tf * 2 + 2 * 3 * D * tf * wbytes + 3 * tm * tf * 4
    return pl.pallas_call(
        functools.partial(_ffn_kernel, final_norm=final_norm),
        out_shape=jax.ShapeDtypeStruct((T, D), F32),
        grid=(T // tm, nf + 1),
        in_specs=[
            pl.BlockSpec((tm, D), lambda i, f: (i, 0), pipeline_mode=pl.Buffered(1)),
            pl.BlockSpec((1, D), lambda i, f: (0, 0)),
            pl.BlockSpec((D, tf), lambda i, f: (0, jnp.minimum(f, nf - 1))),
            pl.BlockSpec((D, tf), lambda i, f: (0, jnp.minimum(f, nf - 1))),
            pl.BlockSpec((tf, D), lambda i, f: (jnp.maximum(f - 1, 0), 0)),
            pl.BlockSpec((1, D), lambda i, f: (0, 0)),
        ],
        out_specs=pl.BlockSpec((tm, D), lambda i, f: (i, 0)),
        scratch_shapes=[pltpu.VMEM((tm, D), BF16), pltpu.VMEM((tm, tf), BF16)],
        compiler_params=_params(est, 2),
        name="dense_ffn",
    )(x, g, wg, wu, wd, fg)


LANE_E1, LANE_E2, LANE_RANK1, LANE_RANK2 = 0, 1, 2, 3


def _router_kernel(x_ref, g_ref, wr_ref, mi_ref, mf_ref, cnt_ref, h_scr, run_ref, *, n_experts):
    @pl.when(pl.program_id(0) == 0)
    def _():
        run_ref[...] = jnp.zeros_like(run_ref)

    tm = x_ref.shape[0]
    _rmsnorm_rows(x_ref, g_ref, h_scr)
    logits = jnp.dot(h_scr[...], wr_ref[...], preferred_element_type=F32)
    lane = lax.broadcasted_iota(jnp.int32, (tm, LANES), 1)
    neg = jnp.float32(-jnp.inf)
    l1 = jnp.where(lane < n_experts, logits, neg)
    m1 = jnp.max(l1, axis=-1, keepdims=True)
    e1 = jnp.min(jnp.where(l1 == m1, lane, LANES), axis=-1, keepdims=True)
    l2 = jnp.where(lane == e1, neg, l1)
    m2 = jnp.max(l2, axis=-1, keepdims=True)
    e2 = jnp.min(jnp.where(l2 == m2, lane, LANES), axis=-1, keepdims=True)
    ex = jnp.exp(m2 - m1)
    w1 = 1.0 / (1.0 + ex)
    w2 = ex / (1.0 + ex)
    onehot = jnp.where((lane == e1) | (lane == e2), 1.0, 0.0)
    rr = lax.broadcasted_iota(jnp.int32, (tm, tm), 0)
    cc = lax.broadcasted_iota(jnp.int32, (tm, tm), 1)
    strict = jnp.where(cc < rr, 1.0, 0.0).astype(BF16)
    before = jnp.dot(strict, onehot.astype(BF16), preferred_element_type=F32) + run_ref[...]
    rank1 = jnp.sum(jnp.where(lane == e1, before, 0.0), axis=-1, keepdims=True).astype(jnp.int32)
    rank2 = jnp.sum(jnp.where(lane == e2, before, 0.0), axis=-1, keepdims=True).astype(jnp.int32)
    run_ref[...] += jnp.sum(onehot, axis=0, keepdims=True)
    mi_ref[...] = jnp.where(lane == LANE_E1, e1, jnp.where(lane == LANE_E2, e2, jnp.where(
        lane == LANE_RANK1, rank1, jnp.where(lane == LANE_RANK2, rank2, 0))))
    mf_ref[...] = jnp.where(lane == 0, w1, jnp.where(lane == 1, w2, 0.0))
    cnt_ref[...] = jnp.broadcast_to(run_ref[...], cnt_ref.shape)


def _router(x, g, wr, n_experts):
    T, D = x.shape
    tm = _tile(T, 512)
    est = 2 * tm * D * 4 + tm * D * 2 + 4 * tm * tm * 4 + 2 * D * LANES * 2
    return pl.pallas_call(
        functools.partial(_router_kernel, n_experts=n_experts),
        out_shape=(
            jax.ShapeDtypeStruct((T, LANES), jnp.int32),
            jax.ShapeDtypeStruct((T, LANES), F32),
            jax.ShapeDtypeStruct((8, LANES), F32),
        ),
        grid=(T // tm,),
        in_specs=[
            pl.BlockSpec((tm, D), lambda i: (i, 0)),
            pl.BlockSpec((1, D), lambda i: (0, 0)),
            pl.BlockSpec((D, LANES), lambda i: (0, 0)),
        ],
        out_specs=(
            pl.BlockSpec((tm, LANES), lambda i: (i, 0)),
            pl.BlockSpec((tm, LANES), lambda i: (i, 0)),
            pl.BlockSpec((8, LANES), lambda i: (0, 0)),
        ),
        scratch_shapes=[pltpu.VMEM((tm, D), BF16), pltpu.VMEM((1, LANES), F32)],
        compiler_params=_params(est, 1),
        name="router",
    )(x, g, wr)


def _expert_kernel(ie_ref, ist_ref, inr_ref, nu_ref, cur_tok, nxt_tok, prv_out,
                   x_hbm, g_ref, wg_ref, wu_ref, wd_ref, y_hbm,
                   xs_scr, xb_scr, t_scr, acc_scr, out_scr, gsem, ssem, *, n_tok, rows_per_step, issue_steps):
    del ie_ref, ist_ref
    j, f = pl.program_id(0), pl.program_id(1)
    nf = pl.num_programs(1) - 1
    tm = xs_scr.shape[0]
    half = tm // 2
    n_used = nu_ref[0]
    used = j < n_used

    def gather_copy(tok_ref, r):
        return pltpu.make_async_copy(x_hbm.at[pl.ds(tok_ref[0, r], 1)], xs_scr.at[pl.ds(r, 1)], gsem)

    def scatter_copy(rows_valid, r):
        dst = jnp.where(r < rows_valid, prv_out[0, r], 2 * n_tok + r)
        return pltpu.make_async_copy(out_scr.at[pl.ds(r, 1)], y_hbm.at[pl.ds(dst, 1)], ssem)

    def wait_gather():
        pltpu.make_async_copy(x_hbm.at[pl.ds(0, tm)], xs_scr, gsem).wait()

    def wait_scatter():
        pltpu.make_async_copy(out_scr, y_hbm.at[pl.ds(0, tm)], ssem).wait()

    @pl.when((j == 0) & (f == 0))
    def _():
        def issue(r, carry):
            gather_copy(cur_tok, r).start()
            return carry
        lax.fori_loop(0, tm, issue, 0)
        wait_gather()
        _rmsnorm_rows(xs_scr, g_ref, xb_scr)
        out_scr[...] = jnp.zeros_like(out_scr)

    @pl.when(used & (f == 0))
    def _():
        acc_scr[...] = jnp.zeros_like(acc_scr)

    prev_rows = jnp.where(j > 0, inr_ref[jnp.maximum(j - 1, 0)], 0)

    def side_work():
        base = (f - 1) * rows_per_step
        for rr in range(rows_per_step):
            gather_copy(nxt_tok, base + rr).start()
            scatter_copy(prev_rows, base + rr).start()

    pl.when(used & (f == nf))(wait_scatter)

    def phases(rows):
        args = (xb_scr.at[pl.ds(0, rows)], t_scr.at[pl.ds(0, rows)], wg_ref, wu_ref, wd_ref,
                acc_scr.at[pl.ds(0, rows)])
        steady = (f > 0) & (f < nf)
        pl.when(f == 0)(functools.partial(_swiglu_phase, "first", *args))
        pl.when(steady & (f <= issue_steps))(functools.partial(_swiglu_phase, "steady", *args, side_work=side_work))
        pl.when(steady & (f > issue_steps))(functools.partial(_swiglu_phase, "steady", *args))
        pl.when(f == nf)(functools.partial(_swiglu_phase, "last", *args, out_ref=out_scr.at[pl.ds(0, rows)]))

    pl.when(used & (inr_ref[j] > half))(functools.partial(phases, tm))
    pl.when(used & (inr_ref[j] <= half))(functools.partial(phases, half))

    @pl.when(used & (f == nf))
    def _():
        wait_gather()
        _rmsnorm_rows(xs_scr, g_ref, xb_scr)

    @pl.when((j == n_used) & (f == 0))
    def _():
        def issue(r, carry):
            scatter_copy(prev_rows, r).start()
            return carry
        lax.fori_loop(0, tm, issue, 0)
        wait_scatter()


def _expert_ffn(x, g, tok_win, out_win, item_expert, item_start, item_rows, n_used, wg, wu, wd):
    T, D = x.shape
    E, _, F = wg.shape
    n_items = item_expert.shape[0]
    n_half, _, tm = tok_win.shape
    tf = _tile(F, 512)
    nf = F // tf
    assert nf >= 2
    issue_steps = _tile(tm, nf - 1)
    rows_per_step = tm // issue_steps

    def w_col(j, f, ie, ist, inr, nu):
        return (ie[j], 0, jnp.where(j < nu[0], jnp.minimum(f, nf - 1), nf - 1))

    def w_row(j, f, ie, ist, inr, nu):
        return (ie[j], jnp.where(j < nu[0], jnp.maximum(f - 1, 0), nf - 1), 0)

    def slots_of_item(shift):
        def index_map(j, f, ie, ist, inr, nu):
            return (ist[jnp.clip(j + shift, 0, n_items - 1)], 0, 0)
        return pl.BlockSpec((None, 1, tm), index_map, memory_space=pltpu.SMEM)

    wbytes = wg.dtype.itemsize
    est = (tm * D * 4 + tm * D * 2 + tm * tf * 2 + 2 * 3 * D * tf * wbytes + 2 * tm * D * 4
           + 3 * tm * tf * 4 + 3 * D * tf * 2)
    return pl.pallas_call(
        functools.partial(_expert_kernel, n_tok=T, rows_per_step=rows_per_step, issue_steps=issue_steps),
        out_shape=jax.ShapeDtypeStruct((2 * T + tm, D), F32),
        grid_spec=pltpu.PrefetchScalarGridSpec(
            num_scalar_prefetch=4,
            grid=(n_items, nf + 1),
            in_specs=[
                slots_of_item(0), slots_of_item(1), slots_of_item(-1),
                pl.BlockSpec(memory_space=pl.ANY),
                pl.BlockSpec((1, D), lambda j, f, ie, ist, inr, nu: (0, 0)),
                pl.BlockSpec((None, D, tf), w_col),
                pl.BlockSpec((None, D, tf), w_col),
                pl.BlockSpec((None, tf, D), w_row),
            ],
            out_specs=pl.BlockSpec(memory_space=pl.ANY),
            scratch_shapes=[pltpu.VMEM((tm, D), F32), pltpu.VMEM((tm, D), BF16), pltpu.VMEM((tm, tf), BF16),
                            pltpu.VMEM((tm, D), F32), pltpu.VMEM((tm, D), F32),
                            pltpu.SemaphoreType.DMA, pltpu.SemaphoreType.DMA],
        ),
        compiler_params=_params(est, 2),
        name="expert_ffn",
    )(item_expert, item_start, item_rows, n_used, tok_win, tok_win, out_win, x, g, wg, wu, wd)


def _combine_kernel(x_ref, mf_ref, fg_ref, y1_ref, y2_ref, o_ref, *, final_norm):
    w = mf_ref[...]
    o_ref[...] = x_ref[...] + w[:, 0:1] * y1_ref[...] + w[:, 1:2] * y2_ref[...]
    if final_norm:
        _rmsnorm_rows(o_ref, fg_ref, o_ref)


def _combine(x, mf, y, fg, final_norm):
    T, D = x.shape
    tc = _tile(T, 512)
    est = 8 * tc * D * 4 + 2 * tc * LANES * 4
    return pl.pallas_call(
        functools.partial(_combine_kernel, final_norm=final_norm),
        out_shape=jax.ShapeDtypeStruct((T, D), F32),
        grid=(T // tc,),
        in_specs=[
            pl.BlockSpec((tc, D), lambda i: (i, 0)),
            pl.BlockSpec((tc, LANES), lambda i: (i, 0)),
            pl.BlockSpec((1, D), lambda i: (0, 0)),
            pl.BlockSpec((tc, D), lambda i: (i, 0)),
            pl.BlockSpec((tc, D), lambda i: (i + T // tc, 0)),
        ],
        out_specs=pl.BlockSpec((tc, D), lambda i: (i, 0)),
        compiler_params=_params(est, 1),
        name="combine",
    )(x, mf, fg, y, y)


def _moe_ffn(x, g, w_router, wg, wu, wd, fg, final_norm):
    T, D = x.shape
    E = w_router.shape[1]
    tm = _tile(T, 1024)
    half = tm // 2
    assert (T * TOP_K) % tm == 0 and E <= half
    wr = jnp.pad(w_router, ((0, 0), (0, LANES - E))).astype(BF16)
    mi, mf, cnt = _router(x, g, wr, E)
    counts = cnt[0, :E].astype(jnp.int32)
    padded = ((counts + half - 1) // half) * half
    cum_padded = jnp.cumsum(padded)
    pad_start = cum_padded - padded
    dest1 = pad_start[mi[:, LANE_E1]] + mi[:, LANE_RANK1]
    dest2 = pad_start[mi[:, LANE_E2]] + mi[:, LANE_RANK2]
    n_half = (T * TOP_K) // half + E
    tok = jnp.arange(T, dtype=jnp.int32)
    slot_dst = jnp.full((n_half * half,), -1, jnp.int32).at[jnp.concatenate([dest1, dest2])].set(
        jnp.concatenate([tok, tok + T])).reshape(n_half, half)

    def windows(a):
        return jnp.concatenate([a, jnp.roll(a, -1, axis=0)], axis=1).reshape(n_half, 1, tm)

    tok_win = windows(jnp.where(slot_dst < 0, 0, jnp.where(slot_dst >= T, slot_dst - T, slot_dst)))
    out_win = windows(slot_dst)
    out_win = jnp.where(out_win < 0, 2 * T + jnp.arange(tm, dtype=jnp.int32), out_win)
    n_items = (T * TOP_K) // tm + E
    items_per_expert = (padded + tm - 1) // tm
    cum_items = jnp.cumsum(items_per_expert)
    item = jnp.arange(n_items, dtype=jnp.int32)
    item_expert = jnp.minimum(jnp.sum((item[:, None] >= cum_items[None, :]).astype(jnp.int32), axis=1), E - 1)
    k = item - (cum_items - items_per_expert)[item_expert]
    n_used = cum_items[-1:]
    used = item < n_used[0]
    item_rows = jnp.where(used, jnp.clip(padded[item_expert] - k * tm, 0, tm), 0)
    item_start = jnp.where(used, (pad_start[item_expert] + k * tm) // half, 0)
    y = _expert_ffn(x, g, tok_win, out_win, item_expert, item_start, item_rows, n_used, wg, wu, wd)
    return _combine(x, mf, y, fg, final_norm)


def kernel(x, mix_norm, w_in, w_pool, pool_scale, w_gate_up, b_gate, gla_norm, w_out, ffn_norm,
           dense_w_gate, dense_w_up, dense_w_down, w_router, exp_w_gate, exp_w_up, exp_w_down, final_norm):
    B, S, D = x.shape
    depth = w_in.shape[0]
    G, C = w_pool.shape[1], w_pool.shape[2]
    pool_w = G * C
    rank, key = w_gate_up.shape[1], w_gate_up.shape[2]
    width = gla_norm.shape[1]
    gate_off = pool_w + 2 * key + width
    assert B == 1 and rank <= LANES and w_in.shape[2] == gate_off + rank + width
    xt = x.reshape(S, D)
    fg = final_norm.reshape(1, D)
    for l in range(depth):
        w = w_in[l]
        w_main = jnp.concatenate([w[:, :gate_off], w[:, gate_off + rank:]], axis=1).astype(BF16)
        w_gl = jnp.pad(w[:, gate_off:gate_off + rank], ((0, 0), (0, LANES - rank))).astype(BF16)
        z, zg = _inproj(xt, mix_norm[l].reshape(1, D), w_main, w_gl)
        pool_out = _pool(z, w_pool[l].astype(BF16), pool_scale[l].reshape(1, pool_w))
        wgu = jnp.pad(w_gate_up[l], ((0, LANES - rank), (0, 0))).astype(BF16)
        gla_out = _gla(z, zg, wgu, b_gate[l].reshape(1, key), gla_norm[l].reshape(1, width),
                       q_off=pool_w, k_off=pool_w + key, v_off=pool_w + 2 * key, r_off=gate_off,
                       key=key, width=width)
        xt = _outproj(xt, pool_out, gla_out, w_out[l].astype(BF16))
        last = l == depth - 1
        i = l // 2
        if l % 2 == 0:
            xt = _dense_ffn(xt, ffn_norm[l].reshape(1, D), dense_w_gate[i].astype(BF16),
                            dense_w_up[i].astype(BF16), dense_w_down[i].astype(BF16), fg, last)
        else:
            xt = _moe_ffn(xt, ffn_norm[l].reshape(1, D), w_router[i], exp_w_gate[i], exp_w_up[i],
                          exp_w_down[i], fg, last)
    return xt.reshape(B, S, D)
```

```python
import functools

import jax
import jax.numpy as jnp
from jax import lax
from jax.experimental import pallas as pl
from jax.experimental.pallas import tpu as pltpu

EPS = 1e-6
POOL_WINDOWS = (2, 4, 8, 16)
GLA_HEADS = 4
GATE_TAU = 16.0
CHUNK = 64
TOP_K = 2

LANES = 128
V7X_VMEM_BYTES = 64 * 1024 * 1024
VMEM_CAP_BYTES = V7X_VMEM_BYTES - 2 * 1024 * 1024

F32 = jnp.float32
BF16 = jnp.bfloat16
HIGHEST = lax.Precision.HIGHEST


def _tile(n, pref):
    t = min(n, pref)
    while n % t:
        t -= 1
    return t


def _params(vmem_estimate_bytes, n_axes):
    limit = min(VMEM_CAP_BYTES, max(32 * 1024 * 1024, int(vmem_estimate_bytes * 1.25)))
    return pltpu.CompilerParams(
        dimension_semantics=("arbitrary",) * n_axes, vmem_limit_bytes=limit)


def _rmsnorm_rows(x_ref, g_ref, dst_ref, straight_line=False):
    rows = x_ref.shape[0]
    chunk = _tile(rows, 128)

    def body(c, carry):
        r0 = c * chunk if straight_line else pl.multiple_of(c * chunk, chunk)
        x = x_ref[pl.ds(r0, chunk), :]
        ms = jnp.mean(x * x, axis=-1, keepdims=True)
        dst_ref[pl.ds(r0, chunk), :] = (x * lax.rsqrt(ms + EPS) * g_ref[...]).astype(dst_ref.dtype)
        return carry

    if straight_line:
        for c in range(rows // chunk):
            body(c, 0)
    else:
        lax.fori_loop(0, rows // chunk, body, 0)


def _silu(a):
    return a * (1.0 / (1.0 + jnp.exp(-a)))


def _inproj_kernel(x_ref, g_ref, w_ref, wgl_ref, z_ref, zg_ref, h_scr):
    @pl.when(pl.program_id(1) == 0)
    def _():
        _rmsnorm_rows(x_ref, g_ref, h_scr)
        zg_ref[...] = jnp.dot(h_scr[...], wgl_ref[...], preferred_element_type=F32)

    z_ref[...] = jnp.dot(h_scr[...], w_ref[...], preferred_element_type=F32).astype(z_ref.dtype)


def _inproj(x, g, w_main, w_gl):
    T, D = x.shape
    N = w_main.shape[1]
    tm, tn = _tile(T, 1024), _tile(N, 1024)
    est = 2 * tm * D * 4 + tm * D * 2 + 2 * D * tn * 2 + 2 * tm * tn * 2 + 2 * D * LANES * 2 + 2 * tm * LANES * 4
    return pl.pallas_call(
        _inproj_kernel,
        out_shape=(jax.ShapeDtypeStruct((T, N), BF16), jax.ShapeDtypeStruct((T, LANES), F32)),
        grid=(T // tm, N // tn),
        in_specs=[
            pl.BlockSpec((tm, D), lambda i, j: (i, 0)),
            pl.BlockSpec((1, D), lambda i, j: (0, 0)),
            pl.BlockSpec((D, tn), lambda i, j: (0, j)),
            pl.BlockSpec((D, LANES), lambda i, j: (0, 0)),
        ],
        out_specs=(
            pl.BlockSpec((tm, tn), lambda i, j: (i, j)),
            pl.BlockSpec((tm, LANES), lambda i, j: (i, 0)),
        ),
        scratch_shapes=[pltpu.VMEM((tm, D), BF16)],
        compiler_params=_params(est, 2),
        name="inproj",
    )(x, g, w_main, w_gl)


POOL_HALO = 128


POOL_HEAD = 16


def _pool_kernel(u_ref, halo_ref, band_ref, band_h_ref, wp_ref, ps_ref, o_ref):
    i = pl.program_id(0)
    tp = u_ref.shape[0]
    C = wp_ref.shape[1]
    t1 = i * tp + lax.broadcasted_iota(jnp.int32, (tp, 1), 0) + 1
    for gi, w in enumerate(POOL_WINDOWS):
        cols = slice(gi * C, (gi + 1) * C)
        u = u_ref[:, cols]
        halo = halo_ref[:, cols]
        halo = jnp.where(i > 0, halo, jnp.zeros_like(halo))
        head = jnp.dot(band_h_ref[gi], halo, preferred_element_type=F32)
        win_sum = jnp.dot(band_ref[gi], u, preferred_element_type=F32) + jnp.concatenate(
            [head, jnp.zeros((tp - POOL_HEAD, C), F32)], axis=0)
        count = jnp.minimum(t1, w).astype(F32)
        d = win_sum / count - u.astype(F32)
        y = jnp.dot(d.astype(BF16), wp_ref[gi], preferred_element_type=F32) * ps_ref[:, cols]
        o_ref[:, cols] = y.astype(o_ref.dtype)


def _pool(z, w_pool, pool_scale):
    T = z.shape[0]
    G, C, _ = w_pool.shape
    W = G * C
    tp = _tile(T, 256)
    assert tp % POOL_HALO == 0 and POOL_HALO >= POOL_HEAD >= max(POOL_WINDOWS) - 1 and G == len(POOL_WINDOWS)
    hb = tp // POOL_HALO
    win = jnp.asarray(POOL_WINDOWS, jnp.int32)[:, None, None]
    row = jnp.arange(tp, dtype=jnp.int32)[None, :, None]
    band = ((jnp.arange(tp)[None, None, :] <= row) & (jnp.arange(tp)[None, None, :] > row - win)).astype(BF16)
    band_h = (jnp.arange(POOL_HALO)[None, None, :] >= row[:, :POOL_HEAD] + (POOL_HALO + 1) - win).astype(BF16)
    est = 2 * (tp + POOL_HALO) * W * 2 + 2 * G * C * C * 2 + 2 * tp * W * 2 + 2 * G * tp * (tp + POOL_HALO) * 2
    return pl.pallas_call(
        _pool_kernel,
        out_shape=jax.ShapeDtypeStruct((T, W), BF16),
        grid=(T // tp,),
        in_specs=[
            pl.BlockSpec((tp, W), lambda i: (i, 0)),
            pl.BlockSpec((POOL_HALO, W), lambda i: (jnp.maximum(i * hb - 1, 0), 0)),
            pl.BlockSpec((G, tp, tp), lambda i: (0, 0, 0)),
            pl.BlockSpec((G, POOL_HEAD, POOL_HALO), lambda i: (0, 0, 0)),
            pl.BlockSpec((G, C, C), lambda i: (0, 0, 0)),
            pl.BlockSpec((1, W), lambda i: (0, 0)),
        ],
        out_specs=pl.BlockSpec((tp, W), lambda i: (i, 0)),
        compiler_params=_params(est, 1),
        name="pool",
    )(z, z, band, band_h, w_pool, pool_scale)


def _log_sigmoid(x):
    return jnp.minimum(x, 0.0) - jnp.log(1.0 + jnp.exp(-jnp.abs(x)))


def _gla_gate_logits(zg_ref, wgu_ref, bg_ref):
    return jnp.dot(zg_ref[...].astype(BF16), wgu_ref[...], preferred_element_type=F32) + bg_ref[...]


def _gla_cum_log_decay(logit, tri):
    g = _log_sigmoid(logit) * (1.0 / GATE_TAU)
    g_head = g.astype(BF16)
    g_rest = (g - g_head.astype(F32)).astype(BF16)
    return (jnp.dot(tri, g_head, preferred_element_type=F32)
            + jnp.dot(tri, g_rest, preferred_element_type=F32))


def _gla_kernel(q_ref, k_ref, v_ref, r_ref, zg_ref, zg_next_ref, wgu_ref, bg_ref, gn_ref, tri_ref, o_ref,
                st_ref, bc_ref, bc_next_ref, *, dk, dv):
    @pl.when(pl.program_id(0) == 0)
    def _():
        st_ref[...] = jnp.zeros_like(st_ref)
        bc_next_ref[...] = _gla_cum_log_decay(_gla_gate_logits(zg_ref, wgu_ref, bg_ref), tri_ref[...])

    tg = q_ref.shape[0]
    n_chunks = tg // CHUNK
    heads = range(GLA_HEADS)
    ks = [slice(h * dk, (h + 1) * dk) for h in heads]
    vs = [slice(h * dv, (h + 1) * dv) for h in heads]
    ri = lax.broadcasted_iota(jnp.int32, (CHUNK, CHUNK), 0)
    ci = lax.broadcasted_iota(jnp.int32, (CHUNK, CHUNK), 1)
    causal = ci <= ri
    scale = dk ** -0.5
    nt = (((1,), (1,)), ((), ()))
    tn = (((0,), (0,)), ((), ()))

    bc_ref[...] = bc_next_ref[...]
    logit_next = _gla_gate_logits(zg_next_ref, wgu_ref, bg_ref)

    def stage_a(c):
        rows = slice(c * CHUNK, (c + 1) * CHUNK)
        bcc = bc_ref[rows, :]
        b_last = bcc[CHUNK - 1:CHUNK, :]
        kf = k_ref[rows, :].astype(F32)
        q_dec = (q_ref[rows, :].astype(F32) * scale * jnp.exp(bcc)).astype(BF16)
        k_dec = (kf * jnp.exp(-bcc)).astype(BF16)
        k_end = (kf * jnp.exp(b_last - bcc)).astype(BF16)
        v = [v_ref[rows, vs[h]] for h in heads]
        att = [lax.dot_general(q_dec[:, ks[h]], k_dec[:, ks[h]], nt, preferred_element_type=F32) for h in heads]
        kv = [lax.dot_general(v[h], k_end[:, ks[h]], tn, preferred_element_type=F32) for h in heads]
        return rows, q_dec, jnp.exp(b_last), v, att, kv

    def stage_b(rows, q_dec, decay, v, att, kv):
        s_t = [st_ref[h] for h in heads]
        o_inter = [lax.dot_general(q_dec[:, ks[h]], s_t[h].astype(BF16), nt, preferred_element_type=F32)
                   for h in heads]
        for h in heads:
            st_ref[h] = s_t[h] * decay[:, ks[h]] + kv[h]
        o_intra = [jnp.dot(jnp.where(causal, att[h], 0.0).astype(BF16), v[h], preferred_element_type=F32)
                   for h in heads]
        for h in heads:
            o = o_inter[h] + o_intra[h]
            o = o * lax.rsqrt(jnp.mean(o * o, axis=-1, keepdims=True) + EPS) * gn_ref[:, vs[h]]
            o_ref[rows, vs[h]] = (o * _silu(r_ref[rows, vs[h]].astype(F32))).astype(o_ref.dtype)

    pending = stage_a(0)
    for c in range(1, n_chunks):
        upcoming = stage_a(c)
        if c == 1:
            bc_next_ref[...] = _gla_cum_log_decay(logit_next, tri_ref[...])
        stage_b(*pending)
        pending = upcoming
    if n_chunks == 1:
        bc_next_ref[...] = _gla_cum_log_decay(logit_next, tri_ref[...])
    stage_b(*pending)


def _gla(z, zg, wgu, bg, gn, *, q_off, k_off, v_off, r_off, key, width):
    T = z.shape[0]
    dk, dv = key // GLA_HEADS, width // GLA_HEADS
    tg = _tile(T, 256)
    assert tg % CHUNK == 0 and T % CHUNK == 0
    assert q_off % key == 0 and k_off % key == 0 and v_off % width == 0 and r_off % width == 0
    est = 2 * tg * (2 * key + 3 * width) * 2 + 2 * tg * LANES * 4 + GLA_HEADS * dv * dk * 4 + 6 * tg * key * 4
    row = jnp.arange(tg, dtype=jnp.int32)
    tri = ((row[None, :] <= row[:, None]) & (row[None, :] // CHUNK == row[:, None] // CHUNK)).astype(BF16)
    return pl.pallas_call(
        functools.partial(_gla_kernel, dk=dk, dv=dv),
        out_shape=jax.ShapeDtypeStruct((T, width), BF16),
        grid=(T // tg,),
        in_specs=[
            pl.BlockSpec((tg, key), lambda i: (i, q_off // key)),
            pl.BlockSpec((tg, key), lambda i: (i, k_off // key)),
            pl.BlockSpec((tg, width), lambda i: (i, v_off // width)),
            pl.BlockSpec((tg, width), lambda i: (i, r_off // width)),
            pl.BlockSpec((tg, LANES), lambda i: (i, 0)),
            pl.BlockSpec((tg, LANES), lambda i: (jnp.minimum(i + 1, T // tg - 1), 0)),
            pl.BlockSpec((LANES, key), lambda i: (0, 0)),
            pl.BlockSpec((1, key), lambda i: (0, 0)),
            pl.BlockSpec((1, width), lambda i: (0, 0)),
            pl.BlockSpec((tg, tg), lambda i: (0, 0)),
        ],
        out_specs=pl.BlockSpec((tg, width), lambda i: (i, 0)),
        scratch_shapes=[pltpu.VMEM((GLA_HEADS, dv, dk), F32), pltpu.VMEM((tg, key), F32),
                        pltpu.VMEM((tg, key), F32)],
        compiler_params=_params(est, 1),
        name="gla",
    )(z, z, z, z, zg, zg, wgu, bg, gn, tri)


def _outproj_kernel(x_ref, p_ref, a_ref, wp_ref, wa_ref, o_ref):
    o_ref[...] = (x_ref[...]
                  + jnp.dot(p_ref[...], wp_ref[...], preferred_element_type=F32)
                  + jnp.dot(a_ref[...], wa_ref[...], preferred_element_type=F32))


def _outproj(x, pool_out, gla_out, w_out):
    T, D = x.shape
    wp_rows, wa_rows = pool_out.shape[1], gla_out.shape[1]
    tm, tn = _tile(T, 1024), _tile(D, 1024)
    assert wp_rows % tn == 0 or wp_rows == w_out.shape[0]
    est = 4 * tm * tn * 4 + 2 * tm * (wp_rows + wa_rows) * 2 + 2 * (wp_rows + wa_rows) * tn * 2
    return pl.pallas_call(
        _outproj_kernel,
        out_shape=jax.ShapeDtypeStruct((T, D), F32),
        grid=(T // tm, D // tn),
        in_specs=[
            pl.BlockSpec((tm, tn), lambda i, j: (i, j)),
            pl.BlockSpec((tm, wp_rows), lambda i, j: (i, 0)),
            pl.BlockSpec((tm, wa_rows), lambda i, j: (i, 0)),
            pl.BlockSpec((wp_rows, tn), lambda i, j: (0, j)),
            pl.BlockSpec((wa_rows, tn), lambda i, j: (wp_rows // wa_rows, j)),
        ],
        out_specs=pl.BlockSpec((tm, tn), lambda i, j: (i, j)),
        compiler_params=_params(est, 2),
        name="outproj",
    )(x, pool_out, gla_out, w_out, w_out)


def _swiglu_up(h_ref, wg_ref, wu_ref):
    h = h_ref[...]
    a = jnp.dot(h, wg_ref[...].astype(BF16), preferred_element_type=F32)
    b = jnp.dot(h, wu_ref[...].astype(BF16), preferred_element_type=F32)
    return (_silu(a) * b).astype(BF16)


def _swiglu_down(t_ref, wd_ref, acc_ref, out_ref=None):
    out_ref = acc_ref if out_ref is None else out_ref
    out_ref[...] = acc_ref[...] + jnp.dot(t_ref[...], wd_ref[...].astype(BF16), preferred_element_type=F32)


def _swiglu_phase(phase, h_ref, t_ref, wg_ref, wu_ref, wd_ref, acc_ref, side_work=None, out_ref=None):
    if phase == "first":
        t_ref[...] = _swiglu_up(h_ref, wg_ref, wu_ref)
    elif phase == "last":
        _swiglu_down(t_ref, wd_ref, acc_ref, out_ref)
    else:
        t_new = _swiglu_up(h_ref, wg_ref, wu_ref)
        if side_work is not None:
            side_work()
        _swiglu_down(t_ref, wd_ref, acc_ref)
        t_ref[...] = t_new


def _ffn_kernel(x_ref, g_ref, wg_ref, wu_ref, wd_ref, fg_ref, o_ref, h_scr, t_scr, *, final_norm):
    f = pl.program_id(1)
    nf = pl.num_programs(1) - 1

    @pl.when(f == 0)
    def _():
        _rmsnorm_rows(x_ref, g_ref, h_scr)
        o_ref[...] = x_ref[...]

    args = (h_scr, t_scr, wg_ref, wu_ref, wd_ref, o_ref)
    pl.when(f == 0)(functools.partial(_swiglu_phase, "first", *args))
    pl.when((f > 0) & (f < nf))(functools.partial(_swiglu_phase, "steady", *args))
    pl.when(f == nf)(functools.partial(_swiglu_phase, "last", *args))

    if final_norm:
        @pl.when(f == nf)
        def _():
            _rmsnorm_rows(o_ref, fg_ref, o_ref)


def _dense_ffn(x, g, wg, wu, wd, fg, final_norm):
    T, D = x.shape
    F = wg.shape[1]
    tm, tf = _tile(T, 1024), _tile(F, 512)
    nf = F // tf
    wbytes = wg.dtype.itemsize
    est = 4 * tm * D * 4 + tm * D * 2 + tm * tf * 2 + 2 * 3 * D * tf * wbytes + 3 * tm * tf * 4
    return pl.pallas_call(
        functools.partial(_ffn_kernel, final_norm=final_norm),
        out_shape=jax.ShapeDtypeStruct((T, D), F32),
        grid=(T // tm, nf + 1),
        in_specs=[
            pl.BlockSpec((tm, D), lambda i, f: (i, 0)),
            pl.BlockSpec((1, D), lambda i, f: (0, 0)),
            pl.BlockSpec((D, tf), lambda i, f: (0, jnp.minimum(f, nf - 1))),
            pl.BlockSpec((D, tf), lambda i, f: (0, jnp.minimum(f, nf - 1))),
            pl.BlockSpec((tf, D), lambda i, f: (jnp.maximum(f - 1, 0), 0)),
            pl.BlockSpec((1, D), lambda i, f: (0, 0)),
        ],
        out_specs=pl.BlockSpec((tm, D), lambda i, f: (i, 0)),
        scratch_shapes=[pltpu.VMEM((tm, D), BF16), pltpu.VMEM((tm, tf), BF16)],
        compiler_params=_params(est, 2),
        name="dense_ffn",
    )(x, g, wg, wu, wd, fg)


LANE_E1, LANE_E2, LANE_RANK1, LANE_RANK2 = 0, 1, 2, 3


def _router_kernel(x_ref, g_ref, wr_ref, mi_ref, mf_ref, cnt_ref, h_scr, run_ref, *, n_experts):
    @pl.when(pl.program_id(0) == 0)
    def _():
        run_ref[...] = jnp.zeros_like(run_ref)

    tm = x_ref.shape[0]
    _rmsnorm_rows(x_ref, g_ref, h_scr)
    logits = jnp.dot(h_scr[...], wr_ref[...], preferred_element_type=F32)
    lane = lax.broadcasted_iota(jnp.int32, (tm, LANES), 1)
    neg = jnp.float32(-jnp.inf)
    l1 = jnp.where(lane < n_experts, logits, neg)
    m1 = jnp.max(l1, axis=-1, keepdims=True)
    e1 = jnp.min(jnp.where(l1 == m1, lane, LANES), axis=-1, keepdims=True)
    l2 = jnp.where(lane == e1, neg, l1)
    m2 = jnp.max(l2, axis=-1, keepdims=True)
    e2 = jnp.min(jnp.where(l2 == m2, lane, LANES), axis=-1, keepdims=True)
    ex = jnp.exp(m2 - m1)
    w1 = 1.0 / (1.0 + ex)
    w2 = ex / (1.0 + ex)
    onehot = jnp.where((lane == e1) | (lane == e2), 1.0, 0.0)
    rr = lax.broadcasted_iota(jnp.int32, (tm, tm), 0)
    cc = lax.broadcasted_iota(jnp.int32, (tm, tm), 1)
    strict = jnp.where(cc < rr, 1.0, 0.0).astype(BF16)
    before = jnp.dot(strict, onehot.astype(BF16), preferred_element_type=F32) + run_ref[...]
    rank1 = jnp.sum(jnp.where(lane == e1, before, 0.0), axis=-1, keepdims=True).astype(jnp.int32)
    rank2 = jnp.sum(jnp.where(lane == e2, before, 0.0), axis=-1, keepdims=True).astype(jnp.int32)
    run_ref[...] += jnp.sum(onehot, axis=0, keepdims=True)
    mi_ref[...] = jnp.where(lane == LANE_E1, e1, jnp.where(lane == LANE_E2, e2, jnp.where(
        lane == LANE_RANK1, rank1, jnp.where(lane == LANE_RANK2, rank2, 0))))
    mf_ref[...] = jnp.where(lane == 0, w1, jnp.where(lane == 1, w2, 0.0))
    cnt_ref[...] = jnp.broadcast_to(run_ref[...], cnt_ref.shape)


def _router(x, g, wr, n_experts):
    T, D = x.shape
    tm = _tile(T, 512)
    est = 2 * tm * D * 4 + tm * D * 2 + 4 * tm * tm * 4 + 2 * D * LANES * 2
    return pl.pallas_call(
        functools.partial(_router_kernel, n_experts=n_experts),
        out_shape=(
            jax.ShapeDtypeStruct((T, LANES), jnp.int32),
            jax.ShapeDtypeStruct((T, LANES), F32),
            jax.ShapeDtypeStruct((8, LANES), F32),
        ),
        grid=(T // tm,),
        in_specs=[
            pl.BlockSpec((tm, D), lambda i: (i, 0)),
            pl.BlockSpec((1, D), lambda i: (0, 0)),
            pl.BlockSpec((D, LANES), lambda i: (0, 0)),
        ],
        out_specs=(
            pl.BlockSpec((tm, LANES), lambda i: (i, 0)),
            pl.BlockSpec((tm, LANES), lambda i: (i, 0)),
            pl.BlockSpec((8, LANES), lambda i: (0, 0)),
        ),
        scratch_shapes=[pltpu.VMEM((tm, D), BF16), pltpu.VMEM((1, LANES), F32)],
        compiler_params=_params(est, 1),
        name="router",
    )(x, g, wr)


def _expert_kernel(ie_ref, ist_ref, inr_ref, nu_ref, cur_tok, nxt_tok, prv_out,
                   x_hbm, g_ref, wg_ref, wu_ref, wd_ref, y_hbm,
                   xs_scr, xb_scr, t_scr, acc_scr, out_scr, gsem, ssem, *, n_tok, rows_per_step, issue_steps):
    del ie_ref, ist_ref
    j, f = pl.program_id(0), pl.program_id(1)
    nf = pl.num_programs(1) - 1
    tm = xs_scr.shape[0]
    half = tm // 2
    n_used = nu_ref[0]
    used = j < n_used

    def gather_copy(tok_ref, r):
        return pltpu.make_async_copy(x_hbm.at[pl.ds(tok_ref[0, r], 1)], xs_scr.at[pl.ds(r, 1)], gsem)

    def scatter_copy(rows_valid, r):
        dst = jnp.where(r < rows_valid, prv_out[0, r], 2 * n_tok + r)
        return pltpu.make_async_copy(out_scr.at[pl.ds(r, 1)], y_hbm.at[pl.ds(dst, 1)], ssem)

    def wait_gather():
        pltpu.make_async_copy(x_hbm.at[pl.ds(0, tm)], xs_scr, gsem).wait()

    def wait_scatter():
        pltpu.make_async_copy(out_scr, y_hbm.at[pl.ds(0, tm)], ssem).wait()

    @pl.when((j == 0) & (f == 0))
    def _():
        def issue(r, carry):
            gather_copy(cur_tok, r).start()
            return carry
        lax.fori_loop(0, tm, issue, 0)
        wait_gather()
        _rmsnorm_rows(xs_scr, g_ref, xb_scr)
        out_scr[...] = jnp.zeros_like(out_scr)

    @pl.when(used & (f == 0))
    def _():
        acc_scr[...] = jnp.zeros_like(acc_scr)

    prev_rows = jnp.where(j > 0, inr_ref[jnp.maximum(j - 1, 0)], 0)

    def side_work():
        base = (f - 1) * rows_per_step
        for rr in range(rows_per_step):
            gather_copy(nxt_tok, base + rr).start()
            scatter_copy(prev_rows, base + rr).start()

    pl.when(used & (f == nf))(wait_scatter)

    def phases(rows):
        args = (xb_scr.at[pl.ds(0, rows)], t_scr.at[pl.ds(0, rows)], wg_ref, wu_ref, wd_ref,
                acc_scr.at[pl.ds(0, rows)])
        steady = (f > 0) & (f < nf)
        pl.when(f == 0)(functools.partial(_swiglu_phase, "first", *args))
        pl.when(steady & (f <= issue_steps))(functools.partial(_swiglu_phase, "steady", *args, side_work=side_work))
        pl.when(steady & (f > issue_steps))(functools.partial(_swiglu_phase, "steady", *args))
        @pl.when(f == nf)
        def _():
            wait_gather()
            _swiglu_phase("last", *args, out_ref=out_scr.at[pl.ds(0, rows)])
            _rmsnorm_rows(xs_scr, g_ref, xb_scr, straight_line=True)

    pl.when(used & (inr_ref[j] > half))(functools.partial(phases, tm))
    pl.when(used & (inr_ref[j] <= half))(functools.partial(phases, half))

    @pl.when((j == n_used) & (f == 0))
    def _():
        def issue(r, carry):
            scatter_copy(prev_rows, r).start()
            return carry
        lax.fori_loop(0, tm, issue, 0)
        wait_scatter()


def _expert_ffn(x, g, tok_win, out_win, item_expert, item_start, item_rows, n_used, wg, wu, wd):
    T, D = x.shape
    E, _, F = wg.shape
    n_items = item_expert.shape[0]
    n_half, _, tm = tok_win.shape
    tf = _tile(F, 512)
    nf = F // tf
    assert nf >= 2
    issue_steps = _tile(tm, nf - 1)
    rows_per_step = tm // issue_steps

    def w_col(j, f, ie, ist, inr, nu):
        return (ie[j], 0, jnp.where(j < nu[0], jnp.minimum(f, nf - 1), nf - 1))

    def w_row(j, f, ie, ist, inr, nu):
        return (ie[j], jnp.where(j < nu[0], jnp.maximum(f - 1, 0), nf - 1), 0)

    def slots_of_item(shift):
        def index_map(j, f, ie, ist, inr, nu):
            return (ist[jnp.clip(j + shift, 0, n_items - 1)], 0, 0)
        return pl.BlockSpec((None, 1, tm), index_map, memory_space=pltpu.SMEM)

    wbytes = wg.dtype.itemsize
    est = (tm * D * 4 + tm * D * 2 + tm * tf * 2 + 2 * 3 * D * tf * wbytes + 2 * tm * D * 4
           + 3 * tm * tf * 4 + 3 * D * tf * 2)
    return pl.pallas_call(
        functools.partial(_expert_kernel, n_tok=T, rows_per_step=rows_per_step, issue_steps=issue_steps),
        out_shape=jax.ShapeDtypeStruct((2 * T + tm, D), F32),
        grid_spec=pltpu.PrefetchScalarGridSpec(
            num_scalar_prefetch=4,
            grid=(n_items, nf + 1),
            in_specs=[
                slots_of_item(0), slots_of_item(1), slots_of_item(-1),
                pl.BlockSpec(memory_space=pl.ANY),
                pl.BlockSpec((1, D), lambda j, f, ie, ist, inr, nu: (0, 0)),
                pl.BlockSpec((None, D, tf), w_col),
                pl.BlockSpec((None, D, tf), w_col),
                pl.BlockSpec((None, tf, D), w_row),
            ],
            out_specs=pl.BlockSpec(memory_space=pl.ANY),
            scratch_shapes=[pltpu.VMEM((tm, D), F32), pltpu.VMEM((tm, D), BF16), pltpu.VMEM((tm, tf), BF16),
                            pltpu.VMEM((tm, D), F32), pltpu.VMEM((tm, D), F32),
                            pltpu.SemaphoreType.DMA, pltpu.SemaphoreType.DMA],
        ),
        compiler_params=_params(est, 2),
        name="expert_ffn",
    )(item_expert, item_start, item_rows, n_used, tok_win, tok_win, out_win, x, g, wg, wu, wd)


def _combine_kernel(x_ref, mf_ref, fg_ref, y1_ref, y2_ref, o_ref, *, final_norm):
    w = mf_ref[...]
    o_ref[...] = x_ref[...] + w[:, 0:1] * y1_ref[...] + w[:, 1:2] * y2_ref[...]
    if final_norm:
        _rmsnorm_rows(o_ref, fg_ref, o_ref)


def _combine(x, mf, y, fg, final_norm):
    T, D = x.shape
    tc = _tile(T, 512)
    est = 8 * tc * D * 4 + 2 * tc * LANES * 4
    return pl.pallas_call(
        functools.partial(_combine_kernel, final_norm=final_norm),
        out_shape=jax.ShapeDtypeStruct((T, D), F32),
        grid=(T // tc,),
        in_specs=[
            pl.BlockSpec((tc, D), lambda i: (i, 0)),
            pl.BlockSpec((tc, LANES), lambda i: (i, 0)),
            pl.BlockSpec((1, D), lambda i: (0, 0)),
            pl.BlockSpec((tc, D), lambda i: (i, 0)),
            pl.BlockSpec((tc, D), lambda i: (i + T // tc, 0)),
        ],
        out_specs=pl.BlockSpec((tc, D), lambda i: (i, 0)),
        compiler_params=_params(est, 1),
        name="combine",
    )(x, mf, fg, y, y)


def _moe_ffn(x, g, w_router, wg, wu, wd, fg, final_norm):
    T, D = x.shape
    E = w_router.shape[1]
    tm = _tile(T, 1024)
    half = tm // 2
    assert (T * TOP_K) % tm == 0 and E <= half
    wr = jnp.pad(w_router, ((0, 0), (0, LANES - E))).astype(BF16)
    mi, mf, cnt = _router(x, g, wr, E)
    counts = cnt[0, :E].astype(jnp.int32)
    padded = ((counts + half - 1) // half) * half
    cum_padded = jnp.cumsum(padded)
    pad_start = cum_padded - padded
    dest1 = pad_start[mi[:, LANE_E1]] + mi[:, LANE_RANK1]
    dest2 = pad_start[mi[:, LANE_E2]] + mi[:, LANE_RANK2]
    n_half = (T * TOP_K) // half + E
    tok = jnp.arange(T, dtype=jnp.int32)
    slot_dst = jnp.full((n_half * half,), -1, jnp.int32).at[jnp.concatenate([dest1, dest2])].set(
        jnp.concatenate([tok, tok + T])).reshape(n_half, half)

    def windows(a):
        return jnp.concatenate([a, jnp.roll(a, -1, axis=0)], axis=1).reshape(n_half, 1, tm)

    tok_win = windows(jnp.where(slot_dst < 0, 0, jnp.where(slot_dst >= T, slot_dst - T, slot_dst)))
    out_win = windows(slot_dst)
    out_win = jnp.where(out_win < 0, 2 * T + jnp.arange(tm, dtype=jnp.int32), out_win)
    n_items = (T * TOP_K) // tm + E
    items_per_expert = (padded + tm - 1) // tm
    cum_items = jnp.cumsum(items_per_expert)
    item = jnp.arange(n_items, dtype=jnp.int32)
    item_expert = jnp.minimum(jnp.sum((item[:, None] >= cum_items[None, :]).astype(jnp.int32), axis=1), E - 1)
    k = item - (cum_items - items_per_expert)[item_expert]
    n_used = cum_items[-1:]
    used = item < n_used[0]
    item_rows = jnp.where(used, jnp.clip(padded[item_expert] - k * tm, 0, tm), 0)
    item_start = jnp.where(used, (pad_start[item_expert] + k * tm) // half, 0)
    y = _expert_ffn(x, g, tok_win, out_win, item_expert, item_start, item_rows, n_used, wg, wu, wd)
    return _combine(x, mf, y, fg, final_norm)


def kernel(x, mix_norm, w_in, w_pool, pool_scale, w_gate_up, b_gate, gla_norm, w_out, ffn_norm,
           dense_w_gate, dense_w_up, dense_w_down, w_router, exp_w_gate, exp_w_up, exp_w_down, final_norm):
    B, S, D = x.shape
    depth = w_in.shape[0]
    G, C = w_pool.shape[1], w_pool.shape[2]
    pool_w = G * C
    rank, key = w_gate_up.shape[1], w_gate_up.shape[2]
    width = gla_norm.shape[1]
    gate_off = pool_w + 2 * key + width
    assert B == 1 and rank <= LANES and w_in.shape[2] == gate_off + rank + width
    xt = x.reshape(S, D)
    fg = final_norm.reshape(1, D)
    for l in range(depth):
        w = w_in[l]
        w_main = jnp.concatenate([w[:, :gate_off], w[:, gate_off + rank:]], axis=1).astype(BF16)
        w_gl = jnp.pad(w[:, gate_off:gate_off + rank], ((0, 0), (0, LANES - rank))).astype(BF16)
        z, zg = _inproj(xt, mix_norm[l].reshape(1, D), w_main, w_gl)
        pool_out = _pool(z, w_pool[l].astype(BF16), pool_scale[l].reshape(1, pool_w))
        wgu = jnp.pad(w_gate_up[l], ((0, LANES - rank), (0, 0))).astype(BF16)
        gla_out = _gla(z, zg, wgu, b_gate[l].reshape(1, key), gla_norm[l].reshape(1, width),
                       q_off=pool_w, k_off=pool_w + key, v_off=pool_w + 2 * key, r_off=gate_off,
                       key=key, width=width)
        xt = _outproj(xt, pool_out, gla_out, w_out[l].astype(BF16))
        last = l == depth - 1
        i = l // 2
        if l % 2 == 0:
            xt = _dense_ffn(xt, ffn_norm[l].reshape(1, D), dense_w_gate[i].astype(BF16),
                            dense_w_up[i].astype(BF16), dense_w_down[i].astype(BF16), fg, last)
        else:
            xt = _moe_ffn(xt, ffn_norm[l].reshape(1, D), w_router[i], exp_w_gate[i], exp_w_up[i],
                          exp_w_down[i], fg, last)
    return xt.reshape(B, S, D)
```

```python
import functools

import jax
import jax.numpy as jnp
from jax import lax
from jax.experimental import pallas as pl
from jax.experimental.pallas import tpu as pltpu

EPS = 1e-6
POOL_WINDOWS = (2, 4, 8, 16)
GLA_HEADS = 4
GATE_TAU = 16.0
CHUNK = 64
TOP_K = 2

LANES = 128
V7X_VMEM_BYTES = 64 * 1024 * 1024
VMEM_CAP_BYTES = V7X_VMEM_BYTES - 2 * 1024 * 1024

F32 = jnp.float32
BF16 = jnp.bfloat16
HIGHEST = lax.Precision.HIGHEST


def _tile(n, pref):
    t = min(n, pref)
    while n % t:
        t -= 1
    return t


def _params(vmem_estimate_bytes, n_axes):
    limit = min(VMEM_CAP_BYTES, max(32 * 1024 * 1024, int(vmem_estimate_bytes * 1.25)))
    return pltpu.CompilerParams(
        dimension_semantics=("arbitrary",) * n_axes, vmem_limit_bytes=limit)


def _rmsnorm_rows(x_ref, g_ref, dst_ref, straight_line=False):
    rows = x_ref.shape[0]
    chunk = _tile(rows, 128)

    def body(c, carry):
        r0 = c * chunk if straight_line else pl.multiple_of(c * chunk, chunk)
        x = x_ref[pl.ds(r0, chunk), :]
        ms = jnp.mean(x * x, axis=-1, keepdims=True)
        dst_ref[pl.ds(r0, chunk), :] = (x * lax.rsqrt(ms + EPS) * g_ref[...]).astype(dst_ref.dtype)
        return carry

    if straight_line:
        for c in range(rows // chunk):
            body(c, 0)
    else:
        lax.fori_loop(0, rows // chunk, body, 0)


def _silu(a):
    return a * (1.0 / (1.0 + jnp.exp(-a)))


def _inproj_kernel(x_ref, g_ref, w_ref, wgl_ref, z_ref, zg_ref, h_scr):
    @pl.when(pl.program_id(1) == 0)
    def _():
        _rmsnorm_rows(x_ref, g_ref, h_scr)
        zg_ref[...] = jnp.dot(h_scr[...], wgl_ref[...], preferred_element_type=F32)

    z_ref[...] = jnp.dot(h_scr[...], w_ref[...], preferred_element_type=F32).astype(z_ref.dtype)


def _inproj(x, g, w_main, w_gl):
    T, D = x.shape
    N = w_main.shape[1]
    tm, tn = _tile(T, 1024), _tile(N, 1024)
    est = 2 * tm * D * 4 + tm * D * 2 + 2 * D * tn * 2 + 2 * tm * tn * 2 + 2 * D * LANES * 2 + 2 * tm * LANES * 4
    return pl.pallas_call(
        _inproj_kernel,
        out_shape=(jax.ShapeDtypeStruct((T, N), BF16), jax.ShapeDtypeStruct((T, LANES), F32)),
        grid=(T // tm, N // tn),
        in_specs=[
            pl.BlockSpec((tm, D), lambda i, j: (i, 0)),
            pl.BlockSpec((1, D), lambda i, j: (0, 0)),
            pl.BlockSpec((D, tn), lambda i, j: (0, j)),
            pl.BlockSpec((D, LANES), lambda i, j: (0, 0)),
        ],
        out_specs=(
            pl.BlockSpec((tm, tn), lambda i, j: (i, j)),
            pl.BlockSpec((tm, LANES), lambda i, j: (i, 0)),
        ),
        scratch_shapes=[pltpu.VMEM((tm, D), BF16)],
        compiler_params=_params(est, 2),
        name="inproj",
    )(x, g, w_main, w_gl)


POOL_HALO = 128


POOL_HEAD = 16


def _pool_kernel(u_ref, halo_ref, band_ref, band_h_ref, wp_ref, ps_ref, o_ref):
    i = pl.program_id(0)
    tp = u_ref.shape[0]
    C = wp_ref.shape[1]
    t1 = i * tp + lax.broadcasted_iota(jnp.int32, (tp, 1), 0) + 1
    for gi, w in enumerate(POOL_WINDOWS):
        cols = slice(gi * C, (gi + 1) * C)
        u = u_ref[:, cols]
        halo = halo_ref[:, cols]
        halo = jnp.where(i > 0, halo, jnp.zeros_like(halo))
        head = jnp.dot(band_h_ref[gi], halo, preferred_element_type=F32)
        win_sum = jnp.dot(band_ref[gi], u, preferred_element_type=F32) + jnp.concatenate(
            [head, jnp.zeros((tp - POOL_HEAD, C), F32)], axis=0)
        count = jnp.minimum(t1, w).astype(F32)
        d = win_sum / count - u.astype(F32)
        y = jnp.dot(d.astype(BF16), wp_ref[gi], preferred_element_type=F32) * ps_ref[:, cols]
        o_ref[:, cols] = y.astype(o_ref.dtype)


def _pool(z, w_pool, pool_scale):
    T = z.shape[0]
    G, C, _ = w_pool.shape
    W = G * C
    tp = _tile(T, 256)
    assert tp % POOL_HALO == 0 and POOL_HALO >= POOL_HEAD >= max(POOL_WINDOWS) - 1 and G == len(POOL_WINDOWS)
    hb = tp // POOL_HALO
    win = jnp.asarray(POOL_WINDOWS, jnp.int32)[:, None, None]
    row = jnp.arange(tp, dtype=jnp.int32)[None, :, None]
    band = ((jnp.arange(tp)[None, None, :] <= row) & (jnp.arange(tp)[None, None, :] > row - win)).astype(BF16)
    band_h = (jnp.arange(POOL_HALO)[None, None, :] >= row[:, :POOL_HEAD] + (POOL_HALO + 1) - win).astype(BF16)
    est = 2 * (tp + POOL_HALO) * W * 2 + 2 * G * C * C * 2 + 2 * tp * W * 2 + 2 * G * tp * (tp + POOL_HALO) * 2
    return pl.pallas_call(
        _pool_kernel,
        out_shape=jax.ShapeDtypeStruct((T, W), BF16),
        grid=(T // tp,),
        in_specs=[
            pl.BlockSpec((tp, W), lambda i: (i, 0)),
            pl.BlockSpec((POOL_HALO, W), lambda i: (jnp.maximum(i * hb - 1, 0), 0)),
            pl.BlockSpec((G, tp, tp), lambda i: (0, 0, 0)),
            pl.BlockSpec((G, POOL_HEAD, POOL_HALO), lambda i: (0, 0, 0)),
            pl.BlockSpec((G, C, C), lambda i: (0, 0, 0)),
            pl.BlockSpec((1, W), lambda i: (0, 0)),
        ],
        out_specs=pl.BlockSpec((tp, W), lambda i: (i, 0)),
        compiler_params=_params(est, 1),
        name="pool",
    )(z, z, band, band_h, w_pool, pool_scale)


def _log_sigmoid(x):
    return jnp.minimum(x, 0.0) - jnp.log(1.0 + jnp.exp(-jnp.abs(x)))


def _gla_gate_logits(zg_ref, wgu_ref, bg_ref):
    return jnp.dot(zg_ref[...].astype(BF16), wgu_ref[...], preferred_element_type=F32) + bg_ref[...]


def _gla_cum_log_decay(logit, tri):
    g = _log_sigmoid(logit) * (1.0 / GATE_TAU)
    g_head = g.astype(BF16)
    g_rest = (g - g_head.astype(F32)).astype(BF16)
    return (jnp.dot(tri, g_head, preferred_element_type=F32)
            + jnp.dot(tri, g_rest, preferred_element_type=F32))


def _gla_kernel(q_ref, k_ref, v_ref, r_ref, zg_ref, zg_next_ref, wgu_ref, bg_ref, gn_ref, tri_ref, o_ref,
                st_ref, bc_ref, bc_next_ref, *, dk, dv):
    @pl.when(pl.program_id(0) == 0)
    def _():
        st_ref[...] = jnp.zeros_like(st_ref)
        bc_next_ref[...] = _gla_cum_log_decay(_gla_gate_logits(zg_ref, wgu_ref, bg_ref), tri_ref[...])

    tg = q_ref.shape[0]
    n_chunks = tg // CHUNK
    heads = range(GLA_HEADS)
    ks = [slice(h * dk, (h + 1) * dk) for h in heads]
    vs = [slice(h * dv, (h + 1) * dv) for h in heads]
    ri = lax.broadcasted_iota(jnp.int32, (CHUNK, CHUNK), 0)
    ci = lax.broadcasted_iota(jnp.int32, (CHUNK, CHUNK), 1)
    causal = ci <= ri
    scale = dk ** -0.5
    nt = (((1,), (1,)), ((), ()))
    tn = (((0,), (0,)), ((), ()))

    bc_ref[...] = bc_next_ref[...]
    logit_next = _gla_gate_logits(zg_next_ref, wgu_ref, bg_ref)

    def stage_a(c):
        rows = slice(c * CHUNK, (c + 1) * CHUNK)
        bcc = bc_ref[rows, :]
        b_last = bcc[CHUNK - 1:CHUNK, :]
        kf = k_ref[rows, :].astype(F32)
        q_dec = (q_ref[rows, :].astype(F32) * scale * jnp.exp(bcc)).astype(BF16)
        k_dec = (kf * jnp.exp(-bcc)).astype(BF16)
        k_end = (kf * jnp.exp(b_last - bcc)).astype(BF16)
        v = [v_ref[rows, vs[h]] for h in heads]
        att = [lax.dot_general(q_dec[:, ks[h]], k_dec[:, ks[h]], nt, preferred_element_type=F32) for h in heads]
        kv = [lax.dot_general(v[h], k_end[:, ks[h]], tn, preferred_element_type=F32) for h in heads]
        return rows, q_dec, jnp.exp(b_last), v, att, kv

    def stage_b(rows, q_dec, decay, v, att, kv):
        s_t = [st_ref[h] for h in heads]
        o_inter = [lax.dot_general(q_dec[:, ks[h]], s_t[h].astype(BF16), nt, preferred_element_type=F32)
                   for h in heads]
        for h in heads:
            st_ref[h] = s_t[h] * decay[:, ks[h]] + kv[h]
        o_intra = [jnp.dot(jnp.where(causal, att[h], 0.0).astype(BF16), v[h], preferred_element_type=F32)
                   for h in heads]
        for h in heads:
            o = o_inter[h] + o_intra[h]
            o = o * lax.rsqrt(jnp.mean(o * o, axis=-1, keepdims=True) + EPS) * gn_ref[:, vs[h]]
            o_ref[rows, vs[h]] = (o * _silu(r_ref[rows, vs[h]].astype(F32))).astype(o_ref.dtype)

    pending = stage_a(0)
    for c in range(1, n_chunks):
        upcoming = stage_a(c)
        if c == 1:
            bc_next_ref[...] = _gla_cum_log_decay(logit_next, tri_ref[...])
        stage_b(*pending)
        pending = upcoming
    if n_chunks == 1:
        bc_next_ref[...] = _gla_cum_log_decay(logit_next, tri_ref[...])
    stage_b(*pending)


def _gla(z, zg, wgu, bg, gn, *, q_off, k_off, v_off, r_off, key, width):
    T = z.shape[0]
    dk, dv = key // GLA_HEADS, width // GLA_HEADS
    tg = _tile(T, 256)
    assert tg % CHUNK == 0 and T % CHUNK == 0
    assert q_off % key == 0 and k_off % key == 0 and v_off % width == 0 and r_off % width == 0
    est = 2 * tg * (2 * key + 3 * width) * 2 + 2 * tg * LANES * 4 + GLA_HEADS * dv * dk * 4 + 6 * tg * key * 4
    row = jnp.arange(tg, dtype=jnp.int32)
    tri = ((row[None, :] <= row[:, None]) & (row[None, :] // CHUNK == row[:, None] // CHUNK)).astype(BF16)
    return pl.pallas_call(
        functools.partial(_gla_kernel, dk=dk, dv=dv),
        out_shape=jax.ShapeDtypeStruct((T, width), BF16),
        grid=(T // tg,),
        in_specs=[
            pl.BlockSpec((tg, key), lambda i: (i, q_off // key)),
            pl.BlockSpec((tg, key), lambda i: (i, k_off // key)),
            pl.BlockSpec((tg, width), lambda i: (i, v_off // width)),
            pl.BlockSpec((tg, width), lambda i: (i, r_off // width)),
            pl.BlockSpec((tg, LANES), lambda i: (i, 0)),
            pl.BlockSpec((tg, LANES), lambda i: (jnp.minimum(i + 1, T // tg - 1), 0)),
            pl.BlockSpec((LANES, key), lambda i: (0, 0)),
            pl.BlockSpec((1, key), lambda i: (0, 0)),
            pl.BlockSpec((1, width), lambda i: (0, 0)),
            pl.BlockSpec((tg, tg), lambda i: (0, 0)),
        ],
        out_specs=pl.BlockSpec((tg, width), lambda i: (i, 0)),
        scratch_shapes=[pltpu.VMEM((GLA_HEADS, dv, dk), F32), pltpu.VMEM((tg, key), F32),
                        pltpu.VMEM((tg, key), F32)],
        compiler_params=_params(est, 1),
        name="gla",
    )(z, z, z, z, zg, zg, wgu, bg, gn, tri)


def _outproj_kernel(x_ref, p_ref, a_ref, wp_ref, wa_ref, o_ref):
    o_ref[...] = (x_ref[...]
                  + jnp.dot(p_ref[...], wp_ref[...], preferred_element_type=F32)
                  + jnp.dot(a_ref[...], wa_ref[...], preferred_element_type=F32))


def _outproj(x, pool_out, gla_out, w_out):
    T, D = x.shape
    wp_rows, wa_rows = pool_out.shape[1], gla_out.shape[1]
    tm, tn = _tile(T, 1024), _tile(D, 1024)
    assert wp_rows % tn == 0 or wp_rows == w_out.shape[0]
    est = 4 * tm * tn * 4 + 2 * tm * (wp_rows + wa_rows) * 2 + 2 * (wp_rows + wa_rows) * tn * 2
    return pl.pallas_call(
        _outproj_kernel,
        out_shape=jax.ShapeDtypeStruct((T, D), F32),
        grid=(T // tm, D // tn),
        in_specs=[
            pl.BlockSpec((tm, tn), lambda i, j: (i, j)),
            pl.BlockSpec((tm, wp_rows), lambda i, j: (i, 0)),
            pl.BlockSpec((tm, wa_rows), lambda i, j: (i, 0)),
            pl.BlockSpec((wp_rows, tn), lambda i, j: (0, j)),
            pl.BlockSpec((wa_rows, tn), lambda i, j: (wp_rows // wa_rows, j)),
        ],
        out_specs=pl.BlockSpec((tm, tn), lambda i, j: (i, j)),
        compiler_params=_params(est, 2),
        name="outproj",
    )(x, pool_out, gla_out, w_out, w_out)


def _swiglu_up(h_ref, wg_ref, wu_ref):
    h = h_ref[...]
    a = jnp.dot(h, wg_ref[...].astype(BF16), preferred_element_type=F32)
    b = jnp.dot(h, wu_ref[...].astype(BF16), preferred_element_type=F32)
    return (_silu(a) * b).astype(BF16)


def _swiglu_down(t_ref, wd_ref, acc_ref, out_ref=None):
    out_ref = acc_ref if out_ref is None else out_ref
    out_ref[...] = acc_ref[...] + jnp.dot(t_ref[...], wd_ref[...].astype(BF16), preferred_element_type=F32)


def _swiglu_phase(phase, h_ref, t_ref, wg_ref, wu_ref, wd_ref, acc_ref, side_work=None, out_ref=None):
    if phase == "first":
        t_ref[...] = _swiglu_up(h_ref, wg_ref, wu_ref)
    elif phase == "last":
        _swiglu_down(t_ref, wd_ref, acc_ref, out_ref)
    else:
        t_new = _swiglu_up(h_ref, wg_ref, wu_ref)
        if side_work is not None:
            side_work()
        _swiglu_down(t_ref, wd_ref, acc_ref)
        t_ref[...] = t_new


def _ffn_kernel(x_ref, g_ref, wg_ref, wu_ref, wd_ref, fg_ref, o_ref, h_scr, t_scr, *, final_norm):
    f = pl.program_id(1)
    nf = pl.num_programs(1) - 1

    @pl.when(f == 0)
    def _():
        _rmsnorm_rows(x_ref, g_ref, h_scr)
        o_ref[...] = x_ref[...]

    args = (h_scr, t_scr, wg_ref, wu_ref, wd_ref, o_ref)
    pl.when(f == 0)(functools.partial(_swiglu_phase, "first", *args))
    pl.when((f > 0) & (f < nf))(functools.partial(_swiglu_phase, "steady", *args))
    pl.when(f == nf)(functools.partial(_swiglu_phase, "last", *args))

    if final_norm:
        @pl.when(f == nf)
        def _():
            _rmsnorm_rows(o_ref, fg_ref, o_ref)


def _dense_ffn(x, g, wg, wu, wd, fg, final_norm):
    T, D = x.shape
    F = wg.shape[1]
    tm, tf = _tile(T, 1024), _tile(F, 512)
    nf = F // tf
    wbytes = wg.dtype.itemsize
    est = 4 * tm * D * 4 + tm * D * 2 + tm * tf * 2 + 2 * 3 * D * tf * wbytes + 3 * tm * tf * 4
    return pl.pallas_call(
        functools.partial(_ffn_kernel, final_norm=final_norm),
        out_shape=jax.ShapeDtypeStruct((T, D), F32),
        grid=(T // tm, nf + 1),
        in_specs=[
            pl.BlockSpec((tm, D), lambda i, f: (i, 0)),
            pl.BlockSpec((1, D), lambda i, f: (0, 0)),
            pl.BlockSpec((D, tf), lambda i, f: (0, jnp.minimum(f, nf - 1))),
            pl.BlockSpec((D, tf), lambda i, f: (0, jnp.minimum(f, nf - 1))),
            pl.BlockSpec((tf, D), lambda i, f: (jnp.maximum(f - 1, 0), 0)),
            pl.BlockSpec((1, D), lambda i, f: (0, 0)),
        ],
        out_specs=pl.BlockSpec((tm, D), lambda i, f: (i, 0)),
        scratch_shapes=[pltpu.VMEM((tm, D), BF16), pltpu.VMEM((tm, tf), BF16)],
        compiler_params=_params(est, 2),
        name="dense_ffn",
    )(x, g, wg, wu, wd, fg)


SUBLANES = 8
ROW_E1, ROW_E2, ROW_RANK1, ROW_RANK2 = 0, 1, 2, 3
ROW_W1, ROW_W2 = 0, 1


def _router_kernel(x_ref, g_ref, wr_ref, earlier_ref, mi_ref, mf_ref, cnt_ref, h_scr, logit_scr, run_ref, *,
                   n_experts):
    @pl.when(pl.program_id(0) == 0)
    def _():
        run_ref[...] = jnp.zeros_like(run_ref)

    tm = x_ref.shape[0]
    _rmsnorm_rows(x_ref, g_ref, h_scr)
    logit_scr[...] = jnp.dot(h_scr[...], wr_ref[...], preferred_element_type=F32)
    lt = jnp.transpose(logit_scr[...])[:SUBLANES, :]
    sub = lax.broadcasted_iota(jnp.int32, (SUBLANES, tm), 0)
    neg = jnp.float32(-jnp.inf)
    l1 = jnp.where(sub < n_experts, lt, neg)
    m1 = jnp.max(l1, axis=0, keepdims=True)
    e1 = jnp.min(jnp.where(l1 == m1, sub, SUBLANES), axis=0, keepdims=True)
    l2 = jnp.where(sub == e1, neg, l1)
    m2 = jnp.max(l2, axis=0, keepdims=True)
    e2 = jnp.min(jnp.where(l2 == m2, sub, SUBLANES), axis=0, keepdims=True)
    ex = jnp.exp(m2 - m1)
    w1 = 1.0 / (1.0 + ex)
    w2 = ex / (1.0 + ex)
    onehot = jnp.where((sub == e1) | (sub == e2), 1.0, 0.0)
    before = jnp.dot(onehot, earlier_ref[...], preferred_element_type=F32) + run_ref[:, 0:1]
    rank1 = jnp.sum(jnp.where(sub == e1, before, 0.0), axis=0, keepdims=True).astype(jnp.int32)
    rank2 = jnp.sum(jnp.where(sub == e2, before, 0.0), axis=0, keepdims=True).astype(jnp.int32)
    run_ref[...] += jnp.sum(onehot, axis=1, keepdims=True)
    mi_ref[...] = jnp.where(sub == ROW_E1, e1, jnp.where(sub == ROW_E2, e2, jnp.where(
        sub == ROW_RANK1, rank1, jnp.where(sub == ROW_RANK2, rank2, 0))))
    mf_ref[...] = jnp.where(sub == ROW_W1, w1, jnp.where(sub == ROW_W2, w2, 0.0))
    cnt_ref[...] = run_ref[...]


def _router(x, g, wr, n_experts):
    T, D = x.shape
    assert n_experts <= SUBLANES
    tm = _tile(T, 512)
    tok = jnp.arange(tm, dtype=jnp.int32)
    earlier = (tok[:, None] < tok[None, :]).astype(F32)
    est = 2 * tm * D * 4 + tm * D * 2 + 2 * tm * tm * 4 + 2 * D * LANES * 2 + 4 * tm * LANES * 4
    return pl.pallas_call(
        functools.partial(_router_kernel, n_experts=n_experts),
        out_shape=(
            jax.ShapeDtypeStruct((SUBLANES, T), jnp.int32),
            jax.ShapeDtypeStruct((SUBLANES, T), F32),
            jax.ShapeDtypeStruct((SUBLANES, LANES), F32),
        ),
        grid=(T // tm,),
        in_specs=[
            pl.BlockSpec((tm, D), lambda i: (i, 0)),
            pl.BlockSpec((1, D), lambda i: (0, 0)),
            pl.BlockSpec((D, LANES), lambda i: (0, 0)),
            pl.BlockSpec((tm, tm), lambda i: (0, 0)),
        ],
        out_specs=(
            pl.BlockSpec((SUBLANES, tm), lambda i: (0, i)),
            pl.BlockSpec((SUBLANES, tm), lambda i: (0, i)),
            pl.BlockSpec((SUBLANES, LANES), lambda i: (0, 0)),
        ),
        scratch_shapes=[pltpu.VMEM((tm, D), BF16), pltpu.VMEM((tm, LANES), F32),
                        pltpu.VMEM((SUBLANES, LANES), F32)],
        compiler_params=_params(est, 1),
        name="router",
    )(x, g, wr, earlier)


def _expert_kernel(ie_ref, ist_ref, inr_ref, nu_ref, cur_tok, nxt_tok, prv_out,
                   x_hbm, g_ref, wg_ref, wu_ref, wd_ref, y_hbm,
                   xs_scr, xb_scr, t_scr, acc_scr, out_scr, gsem, ssem, *, n_tok, rows_per_step, issue_steps):
    del ie_ref, ist_ref
    j, f = pl.program_id(0), pl.program_id(1)
    nf = pl.num_programs(1) - 1
    tm = xs_scr.shape[0]
    half = tm // 2
    n_used = nu_ref[0]
    used = j < n_used

    def gather_copy(tok_ref, r):
        return pltpu.make_async_copy(x_hbm.at[pl.ds(tok_ref[0, r], 1)], xs_scr.at[pl.ds(r, 1)], gsem)

    def scatter_copy(rows_valid, r):
        dst = jnp.where(r < rows_valid, prv_out[0, r], 2 * n_tok + r)
        return pltpu.make_async_copy(out_scr.at[pl.ds(r, 1)], y_hbm.at[pl.ds(dst, 1)], ssem)

    def wait_gather():
        pltpu.make_async_copy(x_hbm.at[pl.ds(0, tm)], xs_scr, gsem).wait()

    def wait_scatter():
        pltpu.make_async_copy(out_scr, y_hbm.at[pl.ds(0, tm)], ssem).wait()

    @pl.when((j == 0) & (f == 0))
    def _():
        def issue(r, carry):
            gather_copy(cur_tok, r).start()
            return carry
        lax.fori_loop(0, tm, issue, 0)
        wait_gather()
        _rmsnorm_rows(xs_scr, g_ref, xb_scr)
        out_scr[...] = jnp.zeros_like(out_scr)

    @pl.when(used & (f == 0))
    def _():
        acc_scr[...] = jnp.zeros_like(acc_scr)

    prev_rows = jnp.where(j > 0, inr_ref[jnp.maximum(j - 1, 0)], 0)

    def side_work():
        base = (f - 1) * rows_per_step
        for rr in range(rows_per_step):
            gather_copy(nxt_tok, base + rr).start()
            scatter_copy(prev_rows, base + rr).start()

    pl.when(used & (f == nf))(wait_scatter)

    def phases(rows):
        args = (xb_scr.at[pl.ds(0, rows)], t_scr.at[pl.ds(0, rows)], wg_ref, wu_ref, wd_ref,
                acc_scr.at[pl.ds(0, rows)])
        steady = (f > 0) & (f < nf)
        pl.when(f == 0)(functools.partial(_swiglu_phase, "first", *args))
        pl.when(steady & (f <= issue_steps))(functools.partial(_swiglu_phase, "steady", *args, side_work=side_work))
        pl.when(steady & (f > issue_steps))(functools.partial(_swiglu_phase, "steady", *args))
        @pl.when(f == nf)
        def _():
            wait_gather()
            _swiglu_phase("last", *args, out_ref=out_scr.at[pl.ds(0, rows)])
            _rmsnorm_rows(xs_scr, g_ref, xb_scr, straight_line=True)

    pl.when(used & (inr_ref[j] > half))(functools.partial(phases, tm))
    pl.when(used & (inr_ref[j] <= half))(functools.partial(phases, half))

    @pl.when((j == n_used) & (f == 0))
    def _():
        def issue(r, carry):
            scatter_copy(prev_rows, r).start()
            return carry
        lax.fori_loop(0, tm, issue, 0)
        wait_scatter()


def _expert_ffn(x, g, tok_win, out_win, item_expert, item_start, item_rows, n_used, wg, wu, wd):
    T, D = x.shape
    E, _, F = wg.shape
    n_items = item_expert.shape[0]
    n_half, _, tm = tok_win.shape
    tf = _tile(F, 512)
    nf = F // tf
    assert nf >= 2
    issue_steps = _tile(tm, nf - 1)
    rows_per_step = tm // issue_steps

    def w_col(j, f, ie, ist, inr, nu):
        return (ie[j], 0, jnp.where(j < nu[0], jnp.minimum(f, nf - 1), nf - 1))

    def w_row(j, f, ie, ist, inr, nu):
        return (ie[j], jnp.where(j < nu[0], jnp.maximum(f - 1, 0), nf - 1), 0)

    def slots_of_item(shift):
        def index_map(j, f, ie, ist, inr, nu):
            return (ist[jnp.clip(j + shift, 0, n_items - 1)], 0, 0)
        return pl.BlockSpec((None, 1, tm), index_map, memory_space=pltpu.SMEM)

    wbytes = wg.dtype.itemsize
    est = (tm * D * 4 + tm * D * 2 + tm * tf * 2 + 2 * 3 * D * tf * wbytes + 2 * tm * D * 4
           + 3 * tm * tf * 4 + 3 * D * tf * 2)
    return pl.pallas_call(
        functools.partial(_expert_kernel, n_tok=T, rows_per_step=rows_per_step, issue_steps=issue_steps),
        out_shape=jax.ShapeDtypeStruct((2 * T + tm, D), F32),
        grid_spec=pltpu.PrefetchScalarGridSpec(
            num_scalar_prefetch=4,
            grid=(n_items, nf + 1),
            in_specs=[
                slots_of_item(0), slots_of_item(1), slots_of_item(-1),
                pl.BlockSpec(memory_space=pl.ANY),
                pl.BlockSpec((1, D), lambda j, f, ie, ist, inr, nu: (0, 0)),
                pl.BlockSpec((None, D, tf), w_col),
                pl.BlockSpec((None, D, tf), w_col),
                pl.BlockSpec((None, tf, D), w_row),
            ],
            out_specs=pl.BlockSpec(memory_space=pl.ANY),
            scratch_shapes=[pltpu.VMEM((tm, D), F32), pltpu.VMEM((tm, D), BF16), pltpu.VMEM((tm, tf), BF16),
                            pltpu.VMEM((tm, D), F32), pltpu.VMEM((tm, D), F32),
                            pltpu.SemaphoreType.DMA, pltpu.SemaphoreType.DMA],
        ),
        compiler_params=_params(est, 2),
        name="expert_ffn",
    )(item_expert, item_start, item_rows, n_used, tok_win, tok_win, out_win, x, g, wg, wu, wd)


def _combine_kernel(x_ref, w_ref, fg_ref, y1_ref, y2_ref, o_ref, *, final_norm):
    w = w_ref[...]
    o_ref[...] = x_ref[...] + w[:, 0:1] * y1_ref[...] + w[:, 1:2] * y2_ref[...]
    if final_norm:
        _rmsnorm_rows(o_ref, fg_ref, o_ref)


def _combine(x, top_w, y, fg, final_norm):
    T, D = x.shape
    tc = _tile(T, 512)
    est = 8 * tc * D * 4 + 2 * tc * LANES * 4
    return pl.pallas_call(
        functools.partial(_combine_kernel, final_norm=final_norm),
        out_shape=jax.ShapeDtypeStruct((T, D), F32),
        grid=(T // tc,),
        in_specs=[
            pl.BlockSpec((tc, D), lambda i: (i, 0)),
            pl.BlockSpec((tc, TOP_K), lambda i: (i, 0)),
            pl.BlockSpec((1, D), lambda i: (0, 0)),
            pl.BlockSpec((tc, D), lambda i: (i, 0)),
            pl.BlockSpec((tc, D), lambda i: (i + T // tc, 0)),
        ],
        out_specs=pl.BlockSpec((tc, D), lambda i: (i, 0)),
        compiler_params=_params(est, 1),
        name="combine",
    )(x, top_w, fg, y, y)


def _moe_ffn(x, g, w_router, wg, wu, wd, fg, final_norm):
    T, D = x.shape
    E = w_router.shape[1]
    tm = _tile(T, 1024)
    half = tm // 2
    assert (T * TOP_K) % tm == 0 and E <= half
    wr = jnp.pad(w_router, ((0, 0), (0, LANES - E))).astype(BF16)
    mi, mf, cnt = _router(x, g, wr, E)
    counts = cnt[:E, 0].astype(jnp.int32)
    padded = ((counts + half - 1) // half) * half
    cum_padded = jnp.cumsum(padded)
    pad_start = cum_padded - padded
    dest1 = pad_start[mi[ROW_E1]] + mi[ROW_RANK1]
    dest2 = pad_start[mi[ROW_E2]] + mi[ROW_RANK2]
    n_half = (T * TOP_K) // half + E
    tok = jnp.arange(T, dtype=jnp.int32)
    slot_dst = jnp.full((n_half * half,), -1, jnp.int32).at[jnp.concatenate([dest1, dest2])].set(
        jnp.concatenate([tok, tok + T])).reshape(n_half, half)

    def windows(a):
        return jnp.concatenate([a, jnp.roll(a, -1, axis=0)], axis=1).reshape(n_half, 1, tm)

    tok_win = windows(jnp.where(slot_dst < 0, 0, jnp.where(slot_dst >= T, slot_dst - T, slot_dst)))
    out_win = windows(slot_dst)
    out_win = jnp.where(out_win < 0, 2 * T + jnp.arange(tm, dtype=jnp.int32), out_win)
    n_items = (T * TOP_K) // tm + E
    items_per_expert = (padded + tm - 1) // tm
    cum_items = jnp.cumsum(items_per_expert)
    item = jnp.arange(n_items, dtype=jnp.int32)
    item_expert = jnp.minimum(jnp.sum((item[:, None] >= cum_items[None, :]).astype(jnp.int32), axis=1), E - 1)
    k = item - (cum_items - items_per_expert)[item_expert]
    n_used = cum_items[-1:]
    used = item < n_used[0]
    item_rows = jnp.where(used, jnp.clip(padded[item_expert] - k * tm, 0, tm), 0)
    item_start = jnp.where(used, (pad_start[item_expert] + k * tm) // half, 0)
    y = _expert_ffn(x, g, tok_win, out_win, item_expert, item_start, item_rows, n_used, wg, wu, wd)
    top_w = jnp.transpose(mf[jnp.array([ROW_W1, ROW_W2])])
    return _combine(x, top_w, y, fg, final_norm)


def kernel(x, mix_norm, w_in, w_pool, pool_scale, w_gate_up, b_gate, gla_norm, w_out, ffn_norm,
           dense_w_gate, dense_w_up, dense_w_down, w_router, exp_w_gate, exp_w_up, exp_w_down, final_norm):
    B, S, D = x.shape
    depth = w_in.shape[0]
    G, C = w_pool.shape[1], w_pool.shape[2]
    pool_w = G * C
    rank, key = w_gate_up.shape[1], w_gate_up.shape[2]
    width = gla_norm.shape[1]
    gate_off = pool_w + 2 * key + width
    assert B == 1 and rank <= LANES and w_in.shape[2] == gate_off + rank + width
    xt = x.reshape(S, D)
    fg = final_norm.reshape(1, D)
    for l in range(depth):
        w = w_in[l]
        w_main = jnp.concatenate([w[:, :gate_off], w[:, gate_off + rank:]], axis=1).astype(BF16)
        w_gl = jnp.pad(w[:, gate_off:gate_off + rank], ((0, 0), (0, LANES - rank))).astype(BF16)
        z, zg = _inproj(xt, mix_norm[l].reshape(1, D), w_main, w_gl)
        pool_out = _pool(z, w_pool[l].astype(BF16), pool_scale[l].reshape(1, pool_w))
        wgu = jnp.pad(w_gate_up[l], ((0, LANES - rank), (0, 0))).astype(BF16)
        gla_out = _gla(z, zg, wgu, b_gate[l].reshape(1, key), gla_norm[l].reshape(1, width),
                       q_off=pool_w, k_off=pool_w + key, v_off=pool_w + 2 * key, r_off=gate_off,
                       key=key, width=width)
        xt = _outproj(xt, pool_out, gla_out, w_out[l].astype(BF16))
        last = l == depth - 1
        i = l // 2
        if l % 2 == 0:
            xt = _dense_ffn(xt, ffn_norm[l].reshape(1, D), dense_w_gate[i].astype(BF16),
                            dense_w_up[i].astype(BF16), dense_w_down[i].astype(BF16), fg, last)
        else:
            xt = _moe_ffn(xt, ffn_norm[l].reshape(1, D), w_router[i], exp_w_gate[i], exp_w_up[i],
                          exp_w_down[i], fg, last)
    return xt.reshape(B, S, D)
```

```python
import functools

import jax
import jax.numpy as jnp
from jax import lax
from jax.experimental import pallas as pl
from jax.experimental.pallas import tpu as pltpu

EPS = 1e-6
POOL_WINDOWS = (2, 4, 8, 16)
GLA_HEADS = 4
GATE_TAU = 16.0
CHUNK = 64
TOP_K = 2
ITEM_PARTS = 4

LANES = 128
V7X_VMEM_BYTES = 64 * 1024 * 1024
VMEM_CAP_BYTES = V7X_VMEM_BYTES - 2 * 1024 * 1024

F32 = jnp.float32
BF16 = jnp.bfloat16
HIGHEST = lax.Precision.HIGHEST


def _tile(n, pref):
    t = min(n, pref)
    while n % t:
        t -= 1
    return t


def _params(vmem_estimate_bytes, n_axes):
    limit = min(VMEM_CAP_BYTES, max(32 * 1024 * 1024, int(vmem_estimate_bytes * 1.25)))
    return pltpu.CompilerParams(
        dimension_semantics=("arbitrary",) * n_axes, vmem_limit_bytes=limit)


def _rmsnorm_rows(x_ref, g_ref, dst_ref, straight_line=False):
    rows = x_ref.shape[0]
    chunk = _tile(rows, 128)

    def body(c, carry):
        r0 = c * chunk if straight_line else pl.multiple_of(c * chunk, chunk)
        x = x_ref[pl.ds(r0, chunk), :]
        ms = jnp.mean(x * x, axis=-1, keepdims=True)
        dst_ref[pl.ds(r0, chunk), :] = (x * lax.rsqrt(ms + EPS) * g_ref[...]).astype(dst_ref.dtype)
        return carry

    if straight_line:
        for c in range(rows // chunk):
            body(c, 0)
    else:
        lax.fori_loop(0, rows // chunk, body, 0)


def _silu(a):
    return a * (1.0 / (1.0 + jnp.exp(-a)))


def _inproj_kernel(x_ref, g_ref, w_ref, wgl_ref, z_ref, zg_ref, h_scr):
    @pl.when(pl.program_id(1) == 0)
    def _():
        _rmsnorm_rows(x_ref, g_ref, h_scr)
        zg_ref[...] = jnp.dot(h_scr[...], wgl_ref[...], preferred_element_type=F32)

    z_ref[...] = jnp.dot(h_scr[...], w_ref[...], preferred_element_type=F32).astype(z_ref.dtype)


def _inproj(x, g, w_main, w_gl):
    T, D = x.shape
    N = w_main.shape[1]
    tm, tn = _tile(T, 1024), _tile(N, 1024)
    est = 2 * tm * D * 4 + tm * D * 2 + 2 * D * tn * 2 + 2 * tm * tn * 2 + 2 * D * LANES * 2 + 2 * tm * LANES * 4
    return pl.pallas_call(
        _inproj_kernel,
        out_shape=(jax.ShapeDtypeStruct((T, N), BF16), jax.ShapeDtypeStruct((T, LANES), F32)),
        grid=(T // tm, N // tn),
        in_specs=[
            pl.BlockSpec((tm, D), lambda i, j: (i, 0)),
            pl.BlockSpec((1, D), lambda i, j: (0, 0)),
            pl.BlockSpec((D, tn), lambda i, j: (0, j)),
            pl.BlockSpec((D, LANES), lambda i, j: (0, 0)),
        ],
        out_specs=(
            pl.BlockSpec((tm, tn), lambda i, j: (i, j)),
            pl.BlockSpec((tm, LANES), lambda i, j: (i, 0)),
        ),
        scratch_shapes=[pltpu.VMEM((tm, D), BF16)],
        compiler_params=_params(est, 2),
        name="inproj",
    )(x, g, w_main, w_gl)


POOL_HALO = 128


POOL_HEAD = 16


def _pool_kernel(u_ref, halo_ref, band_ref, band_h_ref, wp_ref, ps_ref, o_ref):
    i = pl.program_id(0)
    tp = u_ref.shape[0]
    C = wp_ref.shape[1]
    t1 = i * tp + lax.broadcasted_iota(jnp.int32, (tp, 1), 0) + 1
    for gi, w in enumerate(POOL_WINDOWS):
        cols = slice(gi * C, (gi + 1) * C)
        u = u_ref[:, cols]
        halo = halo_ref[:, cols]
        halo = jnp.where(i > 0, halo, jnp.zeros_like(halo))
        head = jnp.dot(band_h_ref[gi], halo, preferred_element_type=F32)
        win_sum = jnp.dot(band_ref[gi], u, preferred_element_type=F32) + jnp.concatenate(
            [head, jnp.zeros((tp - POOL_HEAD, C), F32)], axis=0)
        count = jnp.minimum(t1, w).astype(F32)
        d = win_sum / count - u.astype(F32)
        y = jnp.dot(d.astype(BF16), wp_ref[gi], preferred_element_type=F32) * ps_ref[:, cols]
        o_ref[:, cols] = y.astype(o_ref.dtype)


def _pool(z, w_pool, pool_scale):
    T = z.shape[0]
    G, C, _ = w_pool.shape
    W = G * C
    tp = _tile(T, 256)
    assert tp % POOL_HALO == 0 and POOL_HALO >= POOL_HEAD >= max(POOL_WINDOWS) - 1 and G == len(POOL_WINDOWS)
    hb = tp // POOL_HALO
    win = jnp.asarray(POOL_WINDOWS, jnp.int32)[:, None, None]
    row = jnp.arange(tp, dtype=jnp.int32)[None, :, None]
    band = ((jnp.arange(tp)[None, None, :] <= row) & (jnp.arange(tp)[None, None, :] > row - win)).astype(BF16)
    band_h = (jnp.arange(POOL_HALO)[None, None, :] >= row[:, :POOL_HEAD] + (POOL_HALO + 1) - win).astype(BF16)
    est = 2 * (tp + POOL_HALO) * W * 2 + 2 * G * C * C * 2 + 2 * tp * W * 2 + 2 * G * tp * (tp + POOL_HALO) * 2
    return pl.pallas_call(
        _pool_kernel,
        out_shape=jax.ShapeDtypeStruct((T, W), BF16),
        grid=(T // tp,),
        in_specs=[
            pl.BlockSpec((tp, W), lambda i: (i, 0)),
            pl.BlockSpec((POOL_HALO, W), lambda i: (jnp.maximum(i * hb - 1, 0), 0)),
            pl.BlockSpec((G, tp, tp), lambda i: (0, 0, 0)),
            pl.BlockSpec((G, POOL_HEAD, POOL_HALO), lambda i: (0, 0, 0)),
            pl.BlockSpec((G, C, C), lambda i: (0, 0, 0)),
            pl.BlockSpec((1, W), lambda i: (0, 0)),
        ],
        out_specs=pl.BlockSpec((tp, W), lambda i: (i, 0)),
        compiler_params=_params(est, 1),
        name="pool",
    )(z, z, band, band_h, w_pool, pool_scale)


def _log_sigmoid(x):
    return jnp.minimum(x, 0.0) - jnp.log(1.0 + jnp.exp(-jnp.abs(x)))


def _gla_gate_logits(zg_ref, wgu_ref, bg_ref):
    return jnp.dot(zg_ref[...].astype(BF16), wgu_ref[...], preferred_element_type=F32) + bg_ref[...]


def _gla_cum_log_decay(logit, tri):
    g = _log_sigmoid(logit) * (1.0 / GATE_TAU)
    g_head = g.astype(BF16)
    g_rest = (g - g_head.astype(F32)).astype(BF16)
    return (jnp.dot(tri, g_head, preferred_element_type=F32)
            + jnp.dot(tri, g_rest, preferred_element_type=F32))


def _gla_kernel(q_ref, k_ref, v_ref, r_ref, zg_ref, zg_next_ref, wgu_ref, bg_ref, gn_ref, tri_ref, o_ref,
                st_ref, bc_ref, bc_next_ref, *, dk, dv):
    @pl.when(pl.program_id(0) == 0)
    def _():
        st_ref[...] = jnp.zeros_like(st_ref)
        bc_next_ref[...] = _gla_cum_log_decay(_gla_gate_logits(zg_ref, wgu_ref, bg_ref), tri_ref[...])

    tg = q_ref.shape[0]
    n_chunks = tg // CHUNK
    heads = range(GLA_HEADS)
    ks = [slice(h * dk, (h + 1) * dk) for h in heads]
    vs = [slice(h * dv, (h + 1) * dv) for h in heads]
    ri = lax.broadcasted_iota(jnp.int32, (CHUNK, CHUNK), 0)
    ci = lax.broadcasted_iota(jnp.int32, (CHUNK, CHUNK), 1)
    causal = ci <= ri
    scale = dk ** -0.5
    nt = (((1,), (1,)), ((), ()))
    tn = (((0,), (0,)), ((), ()))

    bc_ref[...] = bc_next_ref[...]
    logit_next = _gla_gate_logits(zg_next_ref, wgu_ref, bg_ref)

    def stage_a(c):
        rows = slice(c * CHUNK, (c + 1) * CHUNK)
        bcc = bc_ref[rows, :]
        b_last = bcc[CHUNK - 1:CHUNK, :]
        kf = k_ref[rows, :].astype(F32)
        q_dec = (q_ref[rows, :].astype(F32) * scale * jnp.exp(bcc)).astype(BF16)
        k_dec = (kf * jnp.exp(-bcc)).astype(BF16)
        k_end = (kf * jnp.exp(b_last - bcc)).astype(BF16)
        v = [v_ref[rows, vs[h]] for h in heads]
        att = [lax.dot_general(q_dec[:, ks[h]], k_dec[:, ks[h]], nt, preferred_element_type=F32) for h in heads]
        kv = [lax.dot_general(v[h], k_end[:, ks[h]], tn, preferred_element_type=F32) for h in heads]
        return rows, q_dec, jnp.exp(b_last), v, att, kv

    def stage_b(rows, q_dec, decay, v, att, kv):
        s_t = [st_ref[h] for h in heads]
        o_inter = [lax.dot_general(q_dec[:, ks[h]], s_t[h].astype(BF16), nt, preferred_element_type=F32)
                   for h in heads]
        for h in heads:
            st_ref[h] = s_t[h] * decay[:, ks[h]] + kv[h]
        o_intra = [jnp.dot(jnp.where(causal, att[h], 0.0).astype(BF16), v[h], preferred_element_type=F32)
                   for h in heads]
        for h in heads:
            o = o_inter[h] + o_intra[h]
            o = o * lax.rsqrt(jnp.mean(o * o, axis=-1, keepdims=True) + EPS) * gn_ref[:, vs[h]]
            o_ref[rows, vs[h]] = (o * _silu(r_ref[rows, vs[h]].astype(F32))).astype(o_ref.dtype)

    pending = stage_a(0)
    for c in range(1, n_chunks):
        upcoming = stage_a(c)
        if c == 1:
            bc_next_ref[...] = _gla_cum_log_decay(logit_next, tri_ref[...])
        stage_b(*pending)
        pending = upcoming
    if n_chunks == 1:
        bc_next_ref[...] = _gla_cum_log_decay(logit_next, tri_ref[...])
    stage_b(*pending)


def _gla(z, zg, wgu, bg, gn, *, q_off, k_off, v_off, r_off, key, width):
    T = z.shape[0]
    dk, dv = key // GLA_HEADS, width // GLA_HEADS
    tg = _tile(T, 256)
    assert tg % CHUNK == 0 and T % CHUNK == 0
    assert q_off % key == 0 and k_off % key == 0 and v_off % width == 0 and r_off % width == 0
    est = 2 * tg * (2 * key + 3 * width) * 2 + 2 * tg * LANES * 4 + GLA_HEADS * dv * dk * 4 + 6 * tg * key * 4
    row = jnp.arange(tg, dtype=jnp.int32)
    tri = ((row[None, :] <= row[:, None]) & (row[None, :] // CHUNK == row[:, None] // CHUNK)).astype(BF16)
    return pl.pallas_call(
        functools.partial(_gla_kernel, dk=dk, dv=dv),
        out_shape=jax.ShapeDtypeStruct((T, width), BF16),
        grid=(T // tg,),
        in_specs=[
            pl.BlockSpec((tg, key), lambda i: (i, q_off // key)),
            pl.BlockSpec((tg, key), lambda i: (i, k_off // key)),
            pl.BlockSpec((tg, width), lambda i: (i, v_off // width)),
            pl.BlockSpec((tg, width), lambda i: (i, r_off // width)),
            pl.BlockSpec((tg, LANES), lambda i: (i, 0)),
            pl.BlockSpec((tg, LANES), lambda i: (jnp.minimum(i + 1, T // tg - 1), 0)),
            pl.BlockSpec((LANES, key), lambda i: (0, 0)),
            pl.BlockSpec((1, key), lambda i: (0, 0)),
            pl.BlockSpec((1, width), lambda i: (0, 0)),
            pl.BlockSpec((tg, tg), lambda i: (0, 0)),
        ],
        out_specs=pl.BlockSpec((tg, width), lambda i: (i, 0)),
        scratch_shapes=[pltpu.VMEM((GLA_HEADS, dv, dk), F32), pltpu.VMEM((tg, key), F32),
                        pltpu.VMEM((tg, key), F32)],
        compiler_params=_params(est, 1),
        name="gla",
    )(z, z, z, z, zg, zg, wgu, bg, gn, tri)


def _outproj_kernel(x_ref, p_ref, a_ref, wp_ref, wa_ref, o_ref):
    o_ref[...] = (x_ref[...]
                  + jnp.dot(p_ref[...], wp_ref[...], preferred_element_type=F32)
                  + jnp.dot(a_ref[...], wa_ref[...], preferred_element_type=F32))


def _outproj(x, pool_out, gla_out, w_out):
    T, D = x.shape
    wp_rows, wa_rows = pool_out.shape[1], gla_out.shape[1]
    tm, tn = _tile(T, 1024), _tile(D, 1024)
    assert wp_rows % tn == 0 or wp_rows == w_out.shape[0]
    est = 4 * tm * tn * 4 + 2 * tm * (wp_rows + wa_rows) * 2 + 2 * (wp_rows + wa_rows) * tn * 2
    return pl.pallas_call(
        _outproj_kernel,
        out_shape=jax.ShapeDtypeStruct((T, D), F32),
        grid=(T // tm, D // tn),
        in_specs=[
            pl.BlockSpec((tm, tn), lambda i, j: (i, j)),
            pl.BlockSpec((tm, wp_rows), lambda i, j: (i, 0)),
            pl.BlockSpec((tm, wa_rows), lambda i, j: (i, 0)),
            pl.BlockSpec((wp_rows, tn), lambda i, j: (0, j)),
            pl.BlockSpec((wa_rows, tn), lambda i, j: (wp_rows // wa_rows, j)),
        ],
        out_specs=pl.BlockSpec((tm, tn), lambda i, j: (i, j)),
        compiler_params=_params(est, 2),
        name="outproj",
    )(x, pool_out, gla_out, w_out, w_out)


def _swiglu_up(h_ref, wg_ref, wu_ref):
    h = h_ref[...]
    a = jnp.dot(h, wg_ref[...].astype(BF16), preferred_element_type=F32)
    b = jnp.dot(h, wu_ref[...].astype(BF16), preferred_element_type=F32)
    return (_silu(a) * b).astype(BF16)


def _swiglu_down(t_ref, wd_ref, acc_ref, out_ref=None):
    out_ref = acc_ref if out_ref is None else out_ref
    out_ref[...] = acc_ref[...] + jnp.dot(t_ref[...], wd_ref[...].astype(BF16), preferred_element_type=F32)


def _swiglu_phase(phase, h_ref, t_ref, wg_ref, wu_ref, wd_ref, acc_ref, side_work=None, out_ref=None):
    if phase == "first":
        t_ref[...] = _swiglu_up(h_ref, wg_ref, wu_ref)
    elif phase == "last":
        _swiglu_down(t_ref, wd_ref, acc_ref, out_ref)
    else:
        t_new = _swiglu_up(h_ref, wg_ref, wu_ref)
        if side_work is not None:
            side_work()
        _swiglu_down(t_ref, wd_ref, acc_ref)
        t_ref[...] = t_new


def _ffn_kernel(x_ref, g_ref, wg_ref, wu_ref, wd_ref, fg_ref, o_ref, h_scr, t_scr, *, final_norm):
    f = pl.program_id(1)
    nf = pl.num_programs(1) - 1

    @pl.when(f == 0)
    def _():
        _rmsnorm_rows(x_ref, g_ref, h_scr)
        o_ref[...] = x_ref[...]

    args = (h_scr, t_scr, wg_ref, wu_ref, wd_ref, o_ref)
    pl.when(f == 0)(functools.partial(_swiglu_phase, "first", *args))
    pl.when((f > 0) & (f < nf))(functools.partial(_swiglu_phase, "steady", *args))
    pl.when(f == nf)(functools.partial(_swiglu_phase, "last", *args))

    if final_norm:
        @pl.when(f == nf)
        def _():
            _rmsnorm_rows(o_ref, fg_ref, o_ref)


def _dense_ffn(x, g, wg, wu, wd, fg, final_norm):
    T, D = x.shape
    F = wg.shape[1]
    tm, tf = _tile(T, 1024), _tile(F, 512)
    nf = F // tf
    wbytes = wg.dtype.itemsize
    est = 4 * tm * D * 4 + tm * D * 2 + tm * tf * 2 + 2 * 3 * D * tf * wbytes + 3 * tm * tf * 4
    return pl.pallas_call(
        functools.partial(_ffn_kernel, final_norm=final_norm),
        out_shape=jax.ShapeDtypeStruct((T, D), F32),
        grid=(T // tm, nf + 1),
        in_specs=[
            pl.BlockSpec((tm, D), lambda i, f: (i, 0)),
            pl.BlockSpec((1, D), lambda i, f: (0, 0)),
            pl.BlockSpec((D, tf), lambda i, f: (0, jnp.minimum(f, nf - 1))),
            pl.BlockSpec((D, tf), lambda i, f: (0, jnp.minimum(f, nf - 1))),
            pl.BlockSpec((tf, D), lambda i, f: (jnp.maximum(f - 1, 0), 0)),
            pl.BlockSpec((1, D), lambda i, f: (0, 0)),
        ],
        out_specs=pl.BlockSpec((tm, D), lambda i, f: (i, 0)),
        scratch_shapes=[pltpu.VMEM((tm, D), BF16), pltpu.VMEM((tm, tf), BF16)],
        compiler_params=_params(est, 2),
        name="dense_ffn",
    )(x, g, wg, wu, wd, fg)


SUBLANES = 8
ROW_E1, ROW_E2, ROW_RANK1, ROW_RANK2 = 0, 1, 2, 3
ROW_W1, ROW_W2 = 0, 1


def _router_kernel(x_ref, g_ref, wr_ref, earlier_ref, mi_ref, mf_ref, cnt_ref, h_scr, logit_scr, run_ref, *,
                   n_experts):
    @pl.when(pl.program_id(0) == 0)
    def _():
        run_ref[...] = jnp.zeros_like(run_ref)

    tm = x_ref.shape[0]
    _rmsnorm_rows(x_ref, g_ref, h_scr)
    logit_scr[...] = jnp.dot(h_scr[...], wr_ref[...], preferred_element_type=F32)
    lt = jnp.transpose(logit_scr[...])[:SUBLANES, :]
    sub = lax.broadcasted_iota(jnp.int32, (SUBLANES, tm), 0)
    neg = jnp.float32(-jnp.inf)
    l1 = jnp.where(sub < n_experts, lt, neg)
    m1 = jnp.max(l1, axis=0, keepdims=True)
    e1 = jnp.min(jnp.where(l1 == m1, sub, SUBLANES), axis=0, keepdims=True)
    l2 = jnp.where(sub == e1, neg, l1)
    m2 = jnp.max(l2, axis=0, keepdims=True)
    e2 = jnp.min(jnp.where(l2 == m2, sub, SUBLANES), axis=0, keepdims=True)
    ex = jnp.exp(m2 - m1)
    w1 = 1.0 / (1.0 + ex)
    w2 = ex / (1.0 + ex)
    onehot = jnp.where((sub == e1) | (sub == e2), 1.0, 0.0)
    before = jnp.dot(onehot, earlier_ref[...], preferred_element_type=F32) + run_ref[:, 0:1]
    rank1 = jnp.sum(jnp.where(sub == e1, before, 0.0), axis=0, keepdims=True).astype(jnp.int32)
    rank2 = jnp.sum(jnp.where(sub == e2, before, 0.0), axis=0, keepdims=True).astype(jnp.int32)
    run_ref[...] += jnp.sum(onehot, axis=1, keepdims=True)
    mi_ref[...] = jnp.where(sub == ROW_E1, e1, jnp.where(sub == ROW_E2, e2, jnp.where(
        sub == ROW_RANK1, rank1, jnp.where(sub == ROW_RANK2, rank2, 0))))
    mf_ref[...] = jnp.where(sub == ROW_W1, w1, jnp.where(sub == ROW_W2, w2, 0.0))
    cnt_ref[...] = run_ref[...]


def _router(x, g, wr, n_experts):
    T, D = x.shape
    assert n_experts <= SUBLANES
    tm = _tile(T, 512)
    tok = jnp.arange(tm, dtype=jnp.int32)
    earlier = (tok[:, None] < tok[None, :]).astype(F32)
    est = 2 * tm * D * 4 + tm * D * 2 + 2 * tm * tm * 4 + 2 * D * LANES * 2 + 4 * tm * LANES * 4
    return pl.pallas_call(
        functools.partial(_router_kernel, n_experts=n_experts),
        out_shape=(
            jax.ShapeDtypeStruct((SUBLANES, T), jnp.int32),
            jax.ShapeDtypeStruct((SUBLANES, T), F32),
            jax.ShapeDtypeStruct((SUBLANES, LANES), F32),
        ),
        grid=(T // tm,),
        in_specs=[
            pl.BlockSpec((tm, D), lambda i: (i, 0)),
            pl.BlockSpec((1, D), lambda i: (0, 0)),
            pl.BlockSpec((D, LANES), lambda i: (0, 0)),
            pl.BlockSpec((tm, tm), lambda i: (0, 0)),
        ],
        out_specs=(
            pl.BlockSpec((SUBLANES, tm), lambda i: (0, i)),
            pl.BlockSpec((SUBLANES, tm), lambda i: (0, i)),
            pl.BlockSpec((SUBLANES, LANES), lambda i: (0, 0)),
        ),
        scratch_shapes=[pltpu.VMEM((tm, D), BF16), pltpu.VMEM((tm, LANES), F32),
                        pltpu.VMEM((SUBLANES, LANES), F32)],
        compiler_params=_params(est, 1),
        name="router",
    )(x, g, wr, earlier)


def _expert_kernel(ie_ref, ist_ref, inr_ref, nu_ref, cur_tok, nxt_tok, prv_out,
                   x_hbm, g_ref, wg_ref, wu_ref, wd_ref, y_hbm,
                   xs_scr, xb_scr, t_scr, acc_scr, out_scr, gsem, ssem, *, n_tok, rows_per_step, issue_steps):
    del ie_ref, ist_ref
    j, f = pl.program_id(0), pl.program_id(1)
    nf = pl.num_programs(1) - 1
    tm = xs_scr.shape[0]
    n_used = nu_ref[0]
    used = j < n_used

    def gather_copy(tok_ref, r):
        return pltpu.make_async_copy(x_hbm.at[pl.ds(tok_ref[0, r], 1)], xs_scr.at[pl.ds(r, 1)], gsem)

    def scatter_copy(rows_valid, r):
        dst = jnp.where(r < rows_valid, prv_out[0, r], 2 * n_tok + r)
        return pltpu.make_async_copy(out_scr.at[pl.ds(r, 1)], y_hbm.at[pl.ds(dst, 1)], ssem)

    def wait_gather():
        pltpu.make_async_copy(x_hbm.at[pl.ds(0, tm)], xs_scr, gsem).wait()

    def wait_scatter():
        pltpu.make_async_copy(out_scr, y_hbm.at[pl.ds(0, tm)], ssem).wait()

    @pl.when((j == 0) & (f == 0))
    def _():
        def issue(r, carry):
            gather_copy(cur_tok, r).start()
            return carry
        lax.fori_loop(0, tm, issue, 0)
        wait_gather()
        _rmsnorm_rows(xs_scr, g_ref, xb_scr)
        out_scr[...] = jnp.zeros_like(out_scr)

    @pl.when(used & (f == 0))
    def _():
        acc_scr[...] = jnp.zeros_like(acc_scr)

    prev_rows = jnp.where(j > 0, inr_ref[jnp.maximum(j - 1, 0)], 0)

    def side_work():
        base = (f - 1) * rows_per_step
        for rr in range(rows_per_step):
            gather_copy(nxt_tok, base + rr).start()
            scatter_copy(prev_rows, base + rr).start()

    pl.when(used & (f == nf))(wait_scatter)

    def phases(rows):
        args = (xb_scr.at[pl.ds(0, rows)], t_scr.at[pl.ds(0, rows)], wg_ref, wu_ref, wd_ref,
                acc_scr.at[pl.ds(0, rows)])
        steady = (f > 0) & (f < nf)
        pl.when(f == 0)(functools.partial(_swiglu_phase, "first", *args))
        pl.when(steady & (f <= issue_steps))(functools.partial(_swiglu_phase, "steady", *args, side_work=side_work))
        pl.when(steady & (f > issue_steps))(functools.partial(_swiglu_phase, "steady", *args))
        @pl.when(f == nf)
        def _():
            wait_gather()
            _swiglu_phase("last", *args, out_ref=out_scr.at[pl.ds(0, rows)])
            _rmsnorm_rows(xs_scr, g_ref, xb_scr, straight_line=True)

    for parts in range(1, ITEM_PARTS + 1):
        rows = parts * (tm // ITEM_PARTS)
        pl.when(used & (inr_ref[j] == rows))(functools.partial(phases, rows))

    @pl.when((j == n_used) & (f == 0))
    def _():
        def issue(r, carry):
            scatter_copy(prev_rows, r).start()
            return carry
        lax.fori_loop(0, tm, issue, 0)
        wait_scatter()


def _expert_ffn(x, g, tok_win, out_win, item_expert, item_start, item_rows, n_used, wg, wu, wd):
    T, D = x.shape
    E, _, F = wg.shape
    n_items = item_expert.shape[0]
    tm = tok_win.shape[2]
    tf = _tile(F, 512)
    nf = F // tf
    assert nf >= 2
    issue_steps = _tile(tm, nf - 1)
    rows_per_step = tm // issue_steps

    def w_col(j, f, ie, ist, inr, nu):
        return (ie[j], 0, jnp.where(j < nu[0], jnp.minimum(f, nf - 1), nf - 1))

    def w_row(j, f, ie, ist, inr, nu):
        return (ie[j], jnp.where(j < nu[0], jnp.maximum(f - 1, 0), nf - 1), 0)

    def slots_of_item(shift):
        def index_map(j, f, ie, ist, inr, nu):
            return (ist[jnp.clip(j + shift, 0, n_items - 1)], 0, 0)
        return pl.BlockSpec((None, 1, tm), index_map, memory_space=pltpu.SMEM)

    wbytes = wg.dtype.itemsize
    est = (tm * D * 4 + tm * D * 2 + tm * tf * 2 + 2 * 3 * D * tf * wbytes + 2 * tm * D * 4
           + 3 * tm * tf * 4 + 3 * D * tf * 2)
    return pl.pallas_call(
        functools.partial(_expert_kernel, n_tok=T, rows_per_step=rows_per_step, issue_steps=issue_steps),
        out_shape=jax.ShapeDtypeStruct((2 * T + tm, D), F32),
        grid_spec=pltpu.PrefetchScalarGridSpec(
            num_scalar_prefetch=4,
            grid=(n_items, nf + 1),
            in_specs=[
                slots_of_item(0), slots_of_item(1), slots_of_item(-1),
                pl.BlockSpec(memory_space=pl.ANY),
                pl.BlockSpec((1, D), lambda j, f, ie, ist, inr, nu: (0, 0)),
                pl.BlockSpec((None, D, tf), w_col),
                pl.BlockSpec((None, D, tf), w_col),
                pl.BlockSpec((None, tf, D), w_row),
            ],
            out_specs=pl.BlockSpec(memory_space=pl.ANY),
            scratch_shapes=[pltpu.VMEM((tm, D), F32), pltpu.VMEM((tm, D), BF16), pltpu.VMEM((tm, tf), BF16),
                            pltpu.VMEM((tm, D), F32), pltpu.VMEM((tm, D), F32),
                            pltpu.SemaphoreType.DMA, pltpu.SemaphoreType.DMA],
        ),
        compiler_params=_params(est, 2),
        name="expert_ffn",
    )(item_expert, item_start, item_rows, n_used, tok_win, tok_win, out_win, x, g, wg, wu, wd)


def _combine_kernel(x_ref, w_ref, fg_ref, y1_ref, y2_ref, o_ref, *, final_norm):
    w = w_ref[...]
    o_ref[...] = x_ref[...] + w[:, 0:1] * y1_ref[...] + w[:, 1:2] * y2_ref[...]
    if final_norm:
        _rmsnorm_rows(o_ref, fg_ref, o_ref)


def _combine(x, top_w, y, fg, final_norm):
    T, D = x.shape
    tc = _tile(T, 512)
    est = 8 * tc * D * 4 + 2 * tc * LANES * 4
    return pl.pallas_call(
        functools.partial(_combine_kernel, final_norm=final_norm),
        out_shape=jax.ShapeDtypeStruct((T, D), F32),
        grid=(T // tc,),
        in_specs=[
            pl.BlockSpec((tc, D), lambda i: (i, 0)),
            pl.BlockSpec((tc, TOP_K), lambda i: (i, 0)),
            pl.BlockSpec((1, D), lambda i: (0, 0)),
            pl.BlockSpec((tc, D), lambda i: (i, 0)),
            pl.BlockSpec((tc, D), lambda i: (i + T // tc, 0)),
        ],
        out_specs=pl.BlockSpec((tc, D), lambda i: (i, 0)),
        compiler_params=_params(est, 1),
        name="combine",
    )(x, top_w, fg, y, y)


def _moe_ffn(x, g, w_router, wg, wu, wd, fg, final_norm):
    T, D = x.shape
    E = w_router.shape[1]
    tm = _tile(T, 1024)
    part = tm // ITEM_PARTS
    assert (T * TOP_K) % tm == 0 and E <= part and part % 16 == 0
    wr = jnp.pad(w_router, ((0, 0), (0, LANES - E))).astype(BF16)
    mi, mf, cnt = _router(x, g, wr, E)
    experts = jnp.arange(E, dtype=jnp.int32)

    def per_expert(table, e):
        return jnp.sum(jnp.where(e[:, None] == experts[None, :], table[None, :], 0), axis=1)

    counts = cnt[:E, 0].astype(jnp.int32)
    padded = ((counts + part - 1) // part) * part
    cum_padded = jnp.cumsum(padded)
    pad_start = cum_padded - padded
    dest1 = per_expert(pad_start, mi[ROW_E1]) + mi[ROW_RANK1]
    dest2 = per_expert(pad_start, mi[ROW_E2]) + mi[ROW_RANK2]
    n_parts = (T * TOP_K) // part + E
    tok = jnp.arange(T, dtype=jnp.int32)
    slot_dst = jnp.full((n_parts * part,), -1, jnp.int32).at[jnp.concatenate([dest1, dest2])].set(
        jnp.concatenate([tok, tok + T]), unique_indices=True).reshape(n_parts, part)

    def windows(a):
        return jnp.concatenate([jnp.roll(a, -s, axis=0) for s in range(ITEM_PARTS)], axis=1).reshape(n_parts, 1, tm)

    tok_win = windows(jnp.where(slot_dst < 0, 0, jnp.where(slot_dst >= T, slot_dst - T, slot_dst)))
    out_win = windows(slot_dst)
    out_win = jnp.where(out_win < 0, 2 * T + jnp.arange(tm, dtype=jnp.int32), out_win)
    n_items = (T * TOP_K) // tm + E
    items_per_expert = (padded + tm - 1) // tm
    cum_items = jnp.cumsum(items_per_expert)
    item = jnp.arange(n_items, dtype=jnp.int32)
    item_expert = jnp.minimum(jnp.sum((item[:, None] >= cum_items[None, :]).astype(jnp.int32), axis=1), E - 1)
    k = item - per_expert(cum_items - items_per_expert, item_expert)
    n_used = cum_items[-1:]
    used = item < n_used[0]
    item_rows = jnp.where(used, jnp.clip(per_expert(padded, item_expert) - k * tm, 0, tm), 0)
    item_start = jnp.where(used, (per_expert(pad_start, item_expert) + k * tm) // part, 0)
    y = _expert_ffn(x, g, tok_win, out_win, item_expert, item_start, item_rows, n_used, wg, wu, wd)
    top_w = jnp.transpose(mf[jnp.array([ROW_W1, ROW_W2])])
    return _combine(x, top_w, y, fg, final_norm)


def kernel(x, mix_norm, w_in, w_pool, pool_scale, w_gate_up, b_gate, gla_norm, w_out, ffn_norm,
           dense_w_gate, dense_w_up, dense_w_down, w_router, exp_w_gate, exp_w_up, exp_w_down, final_norm):
    B, S, D = x.shape
    depth = w_in.shape[0]
    G, C = w_pool.shape[1], w_pool.shape[2]
    pool_w = G * C
    rank, key = w_gate_up.shape[1], w_gate_up.shape[2]
    width = gla_norm.shape[1]
    gate_off = pool_w + 2 * key + width
    assert B == 1 and rank <= LANES and w_in.shape[2] == gate_off + rank + width
    xt = x.reshape(S, D)
    fg = final_norm.reshape(1, D)
    for l in range(depth):
        w = w_in[l]
        w_main = jnp.concatenate([w[:, :gate_off], w[:, gate_off + rank:]], axis=1).astype(BF16)
        w_gl = jnp.pad(w[:, gate_off:gate_off + rank], ((0, 0), (0, LANES - rank))).astype(BF16)
        z, zg = _inproj(xt, mix_norm[l].reshape(1, D), w_main, w_gl)
        pool_out = _pool(z, w_pool[l].astype(BF16), pool_scale[l].reshape(1, pool_w))
        wgu = jnp.pad(w_gate_up[l], ((0, LANES - rank), (0, 0))).astype(BF16)
        gla_out = _gla(z, zg, wgu, b_gate[l].reshape(1, key), gla_norm[l].reshape(1, width),
                       q_off=pool_w, k_off=pool_w + key, v_off=pool_w + 2 * key, r_off=gate_off,
                       key=key, width=width)
        xt = _outproj(xt, pool_out, gla_out, w_out[l].astype(BF16))
        last = l == depth - 1
        i = l // 2
        if l % 2 == 0:
            xt = _dense_ffn(xt, ffn_norm[l].reshape(1, D), dense_w_gate[i].astype(BF16),
                            dense_w_up[i].astype(BF16), dense_w_down[i].astype(BF16), fg, last)
        else:
            xt = _moe_ffn(xt, ffn_norm[l].reshape(1, D), w_router[i], exp_w_gate[i], exp_w_up[i],
                          exp_w_down[i], fg, last)
    return xt.reshape(B, S, D)
```

```python
import functools

import jax
import jax.numpy as jnp
from jax import lax
from jax.experimental import pallas as pl
from jax.experimental.pallas import tpu as pltpu

EPS = 1e-6
POOL_WINDOWS = (2, 4, 8, 16)
GLA_HEADS = 4
GATE_TAU = 16.0
CHUNK = 64
TOP_K = 2
ITEM_PARTS = 2

LANES = 128
V7X_VMEM_BYTES = 64 * 1024 * 1024
VMEM_CAP_BYTES = V7X_VMEM_BYTES - 2 * 1024 * 1024

F32 = jnp.float32
BF16 = jnp.bfloat16
HIGHEST = lax.Precision.HIGHEST


def _tile(n, pref):
    t = min(n, pref)
    while n % t:
        t -= 1
    return t


def _params(vmem_estimate_bytes, n_axes):
    limit = min(VMEM_CAP_BYTES, max(32 * 1024 * 1024, int(vmem_estimate_bytes * 1.25)))
    return pltpu.CompilerParams(
        dimension_semantics=("arbitrary",) * n_axes, vmem_limit_bytes=limit)


def _rmsnorm_rows(x_ref, g_ref, dst_ref, straight_line=False):
    rows = x_ref.shape[0]
    chunk = _tile(rows, 128)

    def body(c, carry):
        r0 = c * chunk if straight_line else pl.multiple_of(c * chunk, chunk)
        x = x_ref[pl.ds(r0, chunk), :]
        ms = jnp.mean(x * x, axis=-1, keepdims=True)
        dst_ref[pl.ds(r0, chunk), :] = (x * lax.rsqrt(ms + EPS) * g_ref[...]).astype(dst_ref.dtype)
        return carry

    if straight_line:
        for c in range(rows // chunk):
            body(c, 0)
    else:
        lax.fori_loop(0, rows // chunk, body, 0)


def _silu(a):
    return a * (1.0 / (1.0 + jnp.exp(-a)))


def _inproj_kernel(x_ref, g_ref, w_ref, wgl_ref, z_ref, zg_ref, h_scr):
    @pl.when(pl.program_id(1) == 0)
    def _():
        _rmsnorm_rows(x_ref, g_ref, h_scr)
        zg_ref[...] = jnp.dot(h_scr[...], wgl_ref[...], preferred_element_type=F32)

    z_ref[...] = jnp.dot(h_scr[...], w_ref[...], preferred_element_type=F32).astype(z_ref.dtype)


def _inproj(x, g, w_main, w_gl):
    T, D = x.shape
    N = w_main.shape[1]
    tm, tn = _tile(T, 1024), _tile(N, 1024)
    est = 2 * tm * D * 4 + tm * D * 2 + 2 * D * tn * 2 + 2 * tm * tn * 2 + 2 * D * LANES * 2 + 2 * tm * LANES * 4
    return pl.pallas_call(
        _inproj_kernel,
        out_shape=(jax.ShapeDtypeStruct((T, N), BF16), jax.ShapeDtypeStruct((T, LANES), F32)),
        grid=(T // tm, N // tn),
        in_specs=[
            pl.BlockSpec((tm, D), lambda i, j: (i, 0)),
            pl.BlockSpec((1, D), lambda i, j: (0, 0)),
            pl.BlockSpec((D, tn), lambda i, j: (0, j)),
            pl.BlockSpec((D, LANES), lambda i, j: (0, 0)),
        ],
        out_specs=(
            pl.BlockSpec((tm, tn), lambda i, j: (i, j)),
            pl.BlockSpec((tm, LANES), lambda i, j: (i, 0)),
        ),
        scratch_shapes=[pltpu.VMEM((tm, D), BF16)],
        compiler_params=_params(est, 2),
        name="inproj",
    )(x, g, w_main, w_gl)


POOL_HALO = 128


POOL_HEAD = 16


def _pool_kernel(u_ref, halo_ref, band_ref, band_h_ref, wp_ref, ps_ref, o_ref):
    i = pl.program_id(0)
    tp = u_ref.shape[0]
    C = wp_ref.shape[1]
    t1 = i * tp + lax.broadcasted_iota(jnp.int32, (tp, 1), 0) + 1
    for gi, w in enumerate(POOL_WINDOWS):
        cols = slice(gi * C, (gi + 1) * C)
        u = u_ref[:, cols]
        halo = halo_ref[:, cols]
        halo = jnp.where(i > 0, halo, jnp.zeros_like(halo))
        head = jnp.dot(band_h_ref[gi], halo, preferred_element_type=F32)
        win_sum = jnp.dot(band_ref[gi], u, preferred_element_type=F32) + jnp.concatenate(
            [head, jnp.zeros((tp - POOL_HEAD, C), F32)], axis=0)
        count = jnp.minimum(t1, w).astype(F32)
        d = win_sum / count - u.astype(F32)
        y = jnp.dot(d.astype(BF16), wp_ref[gi], preferred_element_type=F32) * ps_ref[:, cols]
        o_ref[:, cols] = y.astype(o_ref.dtype)


def _pool(z, w_pool, pool_scale):
    T = z.shape[0]
    G, C, _ = w_pool.shape
    W = G * C
    tp = _tile(T, 256)
    assert tp % POOL_HALO == 0 and POOL_HALO >= POOL_HEAD >= max(POOL_WINDOWS) - 1 and G == len(POOL_WINDOWS)
    hb = tp // POOL_HALO
    win = jnp.asarray(POOL_WINDOWS, jnp.int32)[:, None, None]
    row = jnp.arange(tp, dtype=jnp.int32)[None, :, None]
    band = ((jnp.arange(tp)[None, None, :] <= row) & (jnp.arange(tp)[None, None, :] > row - win)).astype(BF16)
    band_h = (jnp.arange(POOL_HALO)[None, None, :] >= row[:, :POOL_HEAD] + (POOL_HALO + 1) - win).astype(BF16)
    est = 2 * (tp + POOL_HALO) * W * 2 + 2 * G * C * C * 2 + 2 * tp * W * 2 + 2 * G * tp * (tp + POOL_HALO) * 2
    return pl.pallas_call(
        _pool_kernel,
        out_shape=jax.ShapeDtypeStruct((T, W), BF16),
        grid=(T // tp,),
        in_specs=[
            pl.BlockSpec((tp, W), lambda i: (i, 0)),
            pl.BlockSpec((POOL_HALO, W), lambda i: (jnp.maximum(i * hb - 1, 0), 0)),
            pl.BlockSpec((G, tp, tp), lambda i: (0, 0, 0)),
            pl.BlockSpec((G, POOL_HEAD, POOL_HALO), lambda i: (0, 0, 0)),
            pl.BlockSpec((G, C, C), lambda i: (0, 0, 0)),
            pl.BlockSpec((1, W), lambda i: (0, 0)),
        ],
        out_specs=pl.BlockSpec((tp, W), lambda i: (i, 0)),
        compiler_params=_params(est, 1),
        name="pool",
    )(z, z, band, band_h, w_pool, pool_scale)


def _log_sigmoid(x):
    return jnp.minimum(x, 0.0) - jnp.log(1.0 + jnp.exp(-jnp.abs(x)))


def _gla_gate_logits(zg_ref, wgu_ref, bg_ref):
    return jnp.dot(zg_ref[...].astype(BF16), wgu_ref[...], preferred_element_type=F32) + bg_ref[...]


def _gla_cum_log_decay(logit, tri):
    g = _log_sigmoid(logit) * (1.0 / GATE_TAU)
    g_head = g.astype(BF16)
    g_rest = (g - g_head.astype(F32)).astype(BF16)
    return (jnp.dot(tri, g_head, preferred_element_type=F32)
            + jnp.dot(tri, g_rest, preferred_element_type=F32))


def _gla_kernel(q_ref, k_ref, v_ref, r_ref, zg_ref, zg_next_ref, wgu_ref, bg_ref, gn_ref, tri_ref, o_ref,
                st_ref, bc_ref, bc_next_ref, *, dk, dv):
    @pl.when(pl.program_id(0) == 0)
    def _():
        st_ref[...] = jnp.zeros_like(st_ref)
        bc_next_ref[...] = _gla_cum_log_decay(_gla_gate_logits(zg_ref, wgu_ref, bg_ref), tri_ref[...])

    tg = q_ref.shape[0]
    n_chunks = tg // CHUNK
    heads = range(GLA_HEADS)
    ks = [slice(h * dk, (h + 1) * dk) for h in heads]
    vs = [slice(h * dv, (h + 1) * dv) for h in heads]
    ri = lax.broadcasted_iota(jnp.int32, (CHUNK, CHUNK), 0)
    ci = lax.broadcasted_iota(jnp.int32, (CHUNK, CHUNK), 1)
    causal = ci <= ri
    scale = dk ** -0.5
    nt = (((1,), (1,)), ((), ()))
    tn = (((0,), (0,)), ((), ()))

    bc_ref[...] = bc_next_ref[...]
    logit_next = _gla_gate_logits(zg_next_ref, wgu_ref, bg_ref)

    def stage_a(c):
        rows = slice(c * CHUNK, (c + 1) * CHUNK)
        bcc = bc_ref[rows, :]
        b_last = bcc[CHUNK - 1:CHUNK, :]
        kf = k_ref[rows, :].astype(F32)
        q_dec = (q_ref[rows, :].astype(F32) * scale * jnp.exp(bcc)).astype(BF16)
        k_dec = (kf * jnp.exp(-bcc)).astype(BF16)
        k_end = (kf * jnp.exp(b_last - bcc)).astype(BF16)
        v = [v_ref[rows, vs[h]] for h in heads]
        att = [lax.dot_general(q_dec[:, ks[h]], k_dec[:, ks[h]], nt, preferred_element_type=F32) for h in heads]
        kv = [lax.dot_general(v[h], k_end[:, ks[h]], tn, preferred_element_type=F32) for h in heads]
        return rows, q_dec, jnp.exp(b_last), v, att, kv

    def stage_b(rows, q_dec, decay, v, att, kv):
        s_t = [st_ref[h] for h in heads]
        o_inter = [lax.dot_general(q_dec[:, ks[h]], s_t[h].astype(BF16), nt, preferred_element_type=F32)
                   for h in heads]
        for h in heads:
            st_ref[h] = s_t[h] * decay[:, ks[h]] + kv[h]
        o_intra = [jnp.dot(jnp.where(causal, att[h], 0.0).astype(BF16), v[h], preferred_element_type=F32)
                   for h in heads]
        for h in heads:
            o = o_inter[h] + o_intra[h]
            o = o * lax.rsqrt(jnp.mean(o * o, axis=-1, keepdims=True) + EPS) * gn_ref[:, vs[h]]
            o_ref[rows, vs[h]] = (o * _silu(r_ref[rows, vs[h]].astype(F32))).astype(o_ref.dtype)

    pending = stage_a(0)
    for c in range(1, n_chunks):
        upcoming = stage_a(c)
        if c == 1:
            bc_next_ref[...] = _gla_cum_log_decay(logit_next, tri_ref[...])
        stage_b(*pending)
        pending = upcoming
    if n_chunks == 1:
        bc_next_ref[...] = _gla_cum_log_decay(logit_next, tri_ref[...])
    stage_b(*pending)


def _gla(z, zg, wgu, bg, gn, *, q_off, k_off, v_off, r_off, key, width):
    T = z.shape[0]
    dk, dv = key // GLA_HEADS, width // GLA_HEADS
    tg = _tile(T, 256)
    assert tg % CHUNK == 0 and T % CHUNK == 0
    assert q_off % key == 0 and k_off % key == 0 and v_off % width == 0 and r_off % width == 0
    est = 2 * tg * (2 * key + 3 * width) * 2 + 2 * tg * LANES * 4 + GLA_HEADS * dv * dk * 4 + 6 * tg * key * 4
    row = jnp.arange(tg, dtype=jnp.int32)
    tri = ((row[None, :] <= row[:, None]) & (row[None, :] // CHUNK == row[:, None] // CHUNK)).astype(BF16)
    return pl.pallas_call(
        functools.partial(_gla_kernel, dk=dk, dv=dv),
        out_shape=jax.ShapeDtypeStruct((T, width), BF16),
        grid=(T // tg,),
        in_specs=[
            pl.BlockSpec((tg, key), lambda i: (i, q_off // key)),
            pl.BlockSpec((tg, key), lambda i: (i, k_off // key)),
            pl.BlockSpec((tg, width), lambda i: (i, v_off // width)),
            pl.BlockSpec((tg, width), lambda i: (i, r_off // width)),
            pl.BlockSpec((tg, LANES), lambda i: (i, 0)),
            pl.BlockSpec((tg, LANES), lambda i: (jnp.minimum(i + 1, T // tg - 1), 0)),
            pl.BlockSpec((LANES, key), lambda i: (0, 0)),
            pl.BlockSpec((1, key), lambda i: (0, 0)),
            pl.BlockSpec((1, width), lambda i: (0, 0)),
            pl.BlockSpec((tg, tg), lambda i: (0, 0)),
        ],
        out_specs=pl.BlockSpec((tg, width), lambda i: (i, 0)),
        scratch_shapes=[pltpu.VMEM((GLA_HEADS, dv, dk), F32), pltpu.VMEM((tg, key), F32),
                        pltpu.VMEM((tg, key), F32)],
        compiler_params=_params(est, 1),
        name="gla",
    )(z, z, z, z, zg, zg, wgu, bg, gn, tri)


def _outproj_kernel(x_ref, p_ref, a_ref, wp_ref, wa_ref, o_ref):
    o_ref[...] = (x_ref[...]
                  + jnp.dot(p_ref[...], wp_ref[...], preferred_element_type=F32)
                  + jnp.dot(a_ref[...], wa_ref[...], preferred_element_type=F32))


def _outproj(x, pool_out, gla_out, w_out):
    T, D = x.shape
    wp_rows, wa_rows = pool_out.shape[1], gla_out.shape[1]
    tm, tn = _tile(T, 1024), _tile(D, 1024)
    assert wp_rows % tn == 0 or wp_rows == w_out.shape[0]
    est = 4 * tm * tn * 4 + 2 * tm * (wp_rows + wa_rows) * 2 + 2 * (wp_rows + wa_rows) * tn * 2
    return pl.pallas_call(
        _outproj_kernel,
        out_shape=jax.ShapeDtypeStruct((T, D), F32),
        grid=(T // tm, D // tn),
        in_specs=[
            pl.BlockSpec((tm, tn), lambda i, j: (i, j)),
            pl.BlockSpec((tm, wp_rows), lambda i, j: (i, 0)),
            pl.BlockSpec((tm, wa_rows), lambda i, j: (i, 0)),
            pl.BlockSpec((wp_rows, tn), lambda i, j: (0, j)),
            pl.BlockSpec((wa_rows, tn), lambda i, j: (wp_rows // wa_rows, j)),
        ],
        out_specs=pl.BlockSpec((tm, tn), lambda i, j: (i, j)),
        compiler_params=_params(est, 2),
        name="outproj",
    )(x, pool_out, gla_out, w_out, w_out)


def _swiglu_up(h_ref, wg_ref, wu_ref):
    h = h_ref[...]
    a = jnp.dot(h, wg_ref[...].astype(BF16), preferred_element_type=F32)
    b = jnp.dot(h, wu_ref[...].astype(BF16), preferred_element_type=F32)
    return (_silu(a) * b).astype(BF16)


def _swiglu_down(t_ref, wd_ref, acc_ref, out_ref=None):
    out_ref = acc_ref if out_ref is None else out_ref
    out_ref[...] = acc_ref[...] + jnp.dot(t_ref[...], wd_ref[...].astype(BF16), preferred_element_type=F32)


def _swiglu_phase(phase, h_ref, t_ref, wg_ref, wu_ref, wd_ref, acc_ref, side_work=None, out_ref=None):
    if phase == "first":
        t_ref[...] = _swiglu_up(h_ref, wg_ref, wu_ref)
    elif phase == "last":
        _swiglu_down(t_ref, wd_ref, acc_ref, out_ref)
    else:
        t_new = _swiglu_up(h_ref, wg_ref, wu_ref)
        if side_work is not None:
            side_work()
        _swiglu_down(t_ref, wd_ref, acc_ref)
        t_ref[...] = t_new


def _ffn_kernel(x_ref, g_ref, wg_ref, wu_ref, wd_ref, fg_ref, o_ref, h_scr, t_scr, *, final_norm):
    f = pl.program_id(1)
    nf = pl.num_programs(1) - 1

    @pl.when(f == 0)
    def _():
        _rmsnorm_rows(x_ref, g_ref, h_scr)
        o_ref[...] = x_ref[...]

    args = (h_scr, t_scr, wg_ref, wu_ref, wd_ref, o_ref)
    pl.when(f == 0)(functools.partial(_swiglu_phase, "first", *args))
    pl.when((f > 0) & (f < nf))(functools.partial(_swiglu_phase, "steady", *args))
    pl.when(f == nf)(functools.partial(_swiglu_phase, "last", *args))

    if final_norm:
        @pl.when(f == nf)
        def _():
            _rmsnorm_rows(o_ref, fg_ref, o_ref)


def _dense_ffn(x, g, wg, wu, wd, fg, final_norm):
    T, D = x.shape
    F = wg.shape[1]
    tm, tf = _tile(T, 1024), _tile(F, 512)
    nf = F // tf
    wbytes = wg.dtype.itemsize
    est = 4 * tm * D * 4 + tm * D * 2 + tm * tf * 2 + 2 * 3 * D * tf * wbytes + 3 * tm * tf * 4
    return pl.pallas_call(
        functools.partial(_ffn_kernel, final_norm=final_norm),
        out_shape=jax.ShapeDtypeStruct((T, D), F32),
        grid=(T // tm, nf + 1),
        in_specs=[
            pl.BlockSpec((tm, D), lambda i, f: (i, 0)),
            pl.BlockSpec((1, D), lambda i, f: (0, 0)),
            pl.BlockSpec((D, tf), lambda i, f: (0, jnp.minimum(f, nf - 1))),
            pl.BlockSpec((D, tf), lambda i, f: (0, jnp.minimum(f, nf - 1))),
            pl.BlockSpec((tf, D), lambda i, f: (jnp.maximum(f - 1, 0), 0)),
            pl.BlockSpec((1, D), lambda i, f: (0, 0)),
        ],
        out_specs=pl.BlockSpec((tm, D), lambda i, f: (i, 0)),
        scratch_shapes=[pltpu.VMEM((tm, D), BF16), pltpu.VMEM((tm, tf), BF16)],
        compiler_params=_params(est, 2),
        name="dense_ffn",
    )(x, g, wg, wu, wd, fg)


SUBLANES = 8
ROW_E1, ROW_E2, ROW_RANK1, ROW_RANK2 = 0, 1, 2, 3
ROW_W1, ROW_W2 = 0, 1


def _router_kernel(x_ref, g_ref, wr_ref, earlier_ref, mi_ref, mf_ref, cnt_ref, h_scr, logit_scr, run_ref, *,
                   n_experts):
    @pl.when(pl.program_id(0) == 0)
    def _():
        run_ref[...] = jnp.zeros_like(run_ref)

    tm = x_ref.shape[0]
    _rmsnorm_rows(x_ref, g_ref, h_scr)
    logit_scr[...] = jnp.dot(h_scr[...], wr_ref[...], preferred_element_type=F32)
    lt = jnp.transpose(logit_scr[...])[:SUBLANES, :]
    sub = lax.broadcasted_iota(jnp.int32, (SUBLANES, tm), 0)
    neg = jnp.float32(-jnp.inf)
    l1 = jnp.where(sub < n_experts, lt, neg)
    m1 = jnp.max(l1, axis=0, keepdims=True)
    e1 = jnp.min(jnp.where(l1 == m1, sub, SUBLANES), axis=0, keepdims=True)
    l2 = jnp.where(sub == e1, neg, l1)
    m2 = jnp.max(l2, axis=0, keepdims=True)
    e2 = jnp.min(jnp.where(l2 == m2, sub, SUBLANES), axis=0, keepdims=True)
    ex = jnp.exp(m2 - m1)
    w1 = 1.0 / (1.0 + ex)
    w2 = ex / (1.0 + ex)
    onehot = jnp.where((sub == e1) | (sub == e2), 1.0, 0.0)
    before = jnp.dot(onehot, earlier_ref[...], preferred_element_type=F32) + run_ref[:, 0:1]
    rank1 = jnp.sum(jnp.where(sub == e1, before, 0.0), axis=0, keepdims=True).astype(jnp.int32)
    rank2 = jnp.sum(jnp.where(sub == e2, before, 0.0), axis=0, keepdims=True).astype(jnp.int32)
    run_ref[...] += jnp.sum(onehot, axis=1, keepdims=True)
    mi_ref[...] = jnp.where(sub == ROW_E1, e1, jnp.where(sub == ROW_E2, e2, jnp.where(
        sub == ROW_RANK1, rank1, jnp.where(sub == ROW_RANK2, rank2, 0))))
    mf_ref[...] = jnp.where(sub == ROW_W1, w1, jnp.where(sub == ROW_W2, w2, 0.0))
    cnt_ref[...] = run_ref[...]


def _router(x, g, wr, n_experts):
    T, D = x.shape
    assert n_experts <= SUBLANES
    tm = _tile(T, 512)
    tok = jnp.arange(tm, dtype=jnp.int32)
    earlier = (tok[:, None] < tok[None, :]).astype(F32)
    est = 2 * tm * D * 4 + tm * D * 2 + 2 * tm * tm * 4 + 2 * D * LANES * 2 + 4 * tm * LANES * 4
    return pl.pallas_call(
        functools.partial(_router_kernel, n_experts=n_experts),
        out_shape=(
            jax.ShapeDtypeStruct((SUBLANES, T), jnp.int32),
            jax.ShapeDtypeStruct((SUBLANES, T), F32),
            jax.ShapeDtypeStruct((SUBLANES, LANES), F32),
        ),
        grid=(T // tm,),
        in_specs=[
            pl.BlockSpec((tm, D), lambda i: (i, 0)),
            pl.BlockSpec((1, D), lambda i: (0, 0)),
            pl.BlockSpec((D, LANES), lambda i: (0, 0)),
            pl.BlockSpec((tm, tm), lambda i: (0, 0)),
        ],
        out_specs=(
            pl.BlockSpec((SUBLANES, tm), lambda i: (0, i)),
            pl.BlockSpec((SUBLANES, tm), lambda i: (0, i)),
            pl.BlockSpec((SUBLANES, LANES), lambda i: (0, 0)),
        ),
        scratch_shapes=[pltpu.VMEM((tm, D), BF16), pltpu.VMEM((tm, LANES), F32),
                        pltpu.VMEM((SUBLANES, LANES), F32)],
        compiler_params=_params(est, 1),
        name="router",
    )(x, g, wr, earlier)


def _expert_kernel(ie_ref, ist_ref, inr_ref, nu_ref, cur_tok, nxt_tok, prv_out,
                   x_hbm, g_ref, wg_ref, wu_ref, wd_ref, y_hbm,
                   xs_scr, xb_scr, t_scr, acc_scr, out_scr, gsem, ssem, *, n_tok, rows_per_step, issue_steps):
    del ie_ref, ist_ref
    j, f = pl.program_id(0), pl.program_id(1)
    nf = pl.num_programs(1) - 1
    tm = xs_scr.shape[0]
    n_used = nu_ref[0]
    used = j < n_used

    def gather_copy(tok_ref, r):
        return pltpu.make_async_copy(x_hbm.at[pl.ds(tok_ref[0, r], 1)], xs_scr.at[pl.ds(r, 1)], gsem)

    def scatter_copy(rows_valid, r):
        dst = jnp.where(r < rows_valid, prv_out[0, r], 2 * n_tok + r)
        return pltpu.make_async_copy(out_scr.at[pl.ds(r, 1)], y_hbm.at[pl.ds(dst, 1)], ssem)

    def wait_gather():
        pltpu.make_async_copy(x_hbm.at[pl.ds(0, tm)], xs_scr, gsem).wait()

    def wait_scatter():
        pltpu.make_async_copy(out_scr, y_hbm.at[pl.ds(0, tm)], ssem).wait()

    @pl.when((j == 0) & (f == 0))
    def _():
        def issue(r, carry):
            gather_copy(cur_tok, r).start()
            return carry
        lax.fori_loop(0, tm, issue, 0)
        wait_gather()
        _rmsnorm_rows(xs_scr, g_ref, xb_scr)
        out_scr[...] = jnp.zeros_like(out_scr)

    @pl.when(used & (f == 0))
    def _():
        acc_scr[...] = jnp.zeros_like(acc_scr)

    prev_rows = jnp.where(j > 0, inr_ref[jnp.maximum(j - 1, 0)], 0)

    def side_work():
        base = (f - 1) * rows_per_step
        for rr in range(rows_per_step):
            gather_copy(nxt_tok, base + rr).start()
            scatter_copy(prev_rows, base + rr).start()

    pl.when(used & (f == nf))(wait_scatter)

    def phases(rows):
        args = (xb_scr.at[pl.ds(0, rows)], t_scr.at[pl.ds(0, rows)], wg_ref, wu_ref, wd_ref,
                acc_scr.at[pl.ds(0, rows)])
        steady = (f > 0) & (f < nf)
        pl.when(f == 0)(functools.partial(_swiglu_phase, "first", *args))
        pl.when(steady & (f <= issue_steps))(functools.partial(_swiglu_phase, "steady", *args, side_work=side_work))
        pl.when(steady & (f > issue_steps))(functools.partial(_swiglu_phase, "steady", *args))
        @pl.when(f == nf)
        def _():
            wait_gather()
            _swiglu_phase("last", *args, out_ref=out_scr.at[pl.ds(0, rows)])
            _rmsnorm_rows(xs_scr, g_ref, xb_scr, straight_line=True)

    for parts in range(1, ITEM_PARTS + 1):
        rows = parts * (tm // ITEM_PARTS)
        pl.when(used & (inr_ref[j] == rows))(functools.partial(phases, rows))

    @pl.when((j == n_used) & (f == 0))
    def _():
        def issue(r, carry):
            scatter_copy(prev_rows, r).start()
            return carry
        lax.fori_loop(0, tm, issue, 0)
        wait_scatter()


def _expert_ffn(x, g, tok_win, out_win, item_expert, item_start, item_rows, n_used, wg, wu, wd):
    T, D = x.shape
    E, _, F = wg.shape
    n_items = item_expert.shape[0]
    tm = tok_win.shape[2]
    tf = _tile(F, 512)
    nf = F // tf
    assert nf >= 2
    issue_steps = _tile(tm, nf - 1)
    rows_per_step = tm // issue_steps

    def w_col(j, f, ie, ist, inr, nu):
        return (ie[j], 0, jnp.where(j < nu[0], jnp.minimum(f, nf - 1), nf - 1))

    def w_row(j, f, ie, ist, inr, nu):
        return (ie[j], jnp.where(j < nu[0], jnp.maximum(f - 1, 0), nf - 1), 0)

    def slots_of_item(shift):
        def index_map(j, f, ie, ist, inr, nu):
            return (ist[jnp.clip(j + shift, 0, n_items - 1)], 0, 0)
        return pl.BlockSpec((None, 1, tm), index_map, memory_space=pltpu.SMEM)

    wbytes = wg.dtype.itemsize
    est = (tm * D * 4 + tm * D * 2 + tm * tf * 2 + 2 * 3 * D * tf * wbytes + 2 * tm * D * 4
           + 3 * tm * tf * 4 + 3 * D * tf * 2)
    return pl.pallas_call(
        functools.partial(_expert_kernel, n_tok=T, rows_per_step=rows_per_step, issue_steps=issue_steps),
        out_shape=jax.ShapeDtypeStruct((2 * T + tm, D), F32),
        grid_spec=pltpu.PrefetchScalarGridSpec(
            num_scalar_prefetch=4,
            grid=(n_items, nf + 1),
            in_specs=[
                slots_of_item(0), slots_of_item(1), slots_of_item(-1),
                pl.BlockSpec(memory_space=pl.ANY),
                pl.BlockSpec((1, D), lambda j, f, ie, ist, inr, nu: (0, 0)),
                pl.BlockSpec((None, D, tf), w_col),
                pl.BlockSpec((None, D, tf), w_col),
                pl.BlockSpec((None, tf, D), w_row),
            ],
            out_specs=pl.BlockSpec(memory_space=pl.ANY),
            scratch_shapes=[pltpu.VMEM((tm, D), F32), pltpu.VMEM((tm, D), BF16), pltpu.VMEM((tm, tf), BF16),
                            pltpu.VMEM((tm, D), F32), pltpu.VMEM((tm, D), F32),
                            pltpu.SemaphoreType.DMA, pltpu.SemaphoreType.DMA],
        ),
        compiler_params=_params(est, 2),
        name="expert_ffn",
    )(item_expert, item_start, item_rows, n_used, tok_win, tok_win, out_win, x, g, wg, wu, wd)


def _combine_kernel(x_ref, w_ref, fg_ref, y1_ref, y2_ref, o_ref, *, final_norm):
    w = w_ref[...]
    o_ref[...] = x_ref[...] + w[:, 0:1] * y1_ref[...] + w[:, 1:2] * y2_ref[...]
    if final_norm:
        _rmsnorm_rows(o_ref, fg_ref, o_ref)


def _combine(x, top_w, y, fg, final_norm):
    T, D = x.shape
    tc = _tile(T, 512)
    est = 8 * tc * D * 4 + 2 * tc * LANES * 4
    return pl.pallas_call(
        functools.partial(_combine_kernel, final_norm=final_norm),
        out_shape=jax.ShapeDtypeStruct((T, D), F32),
        grid=(T // tc,),
        in_specs=[
            pl.BlockSpec((tc, D), lambda i: (i, 0)),
            pl.BlockSpec((tc, TOP_K), lambda i: (i, 0)),
            pl.BlockSpec((1, D), lambda i: (0, 0)),
            pl.BlockSpec((tc, D), lambda i: (i, 0)),
            pl.BlockSpec((tc, D), lambda i: (i + T // tc, 0)),
        ],
        out_specs=pl.BlockSpec((tc, D), lambda i: (i, 0)),
        compiler_params=_params(est, 1),
        name="combine",
    )(x, top_w, fg, y, y)


def _moe_ffn(x, g, w_router, wg, wu, wd, fg, final_norm):
    T, D = x.shape
    E = w_router.shape[1]
    tm = _tile(T, 1024)
    part = tm // ITEM_PARTS
    assert (T * TOP_K) % tm == 0 and E <= part and part % 16 == 0
    wr = jnp.pad(w_router, ((0, 0), (0, LANES - E))).astype(BF16)
    mi, mf, cnt = _router(x, g, wr, E)
    experts = jnp.arange(E, dtype=jnp.int32)

    def per_expert(table, e):
        return jnp.sum(jnp.where(e[:, None] == experts[None, :], table[None, :], 0), axis=1)

    counts = cnt[:E, 0].astype(jnp.int32)
    padded = ((counts + part - 1) // part) * part
    cum_padded = jnp.cumsum(padded)
    pad_start = cum_padded - padded
    dest1 = per_expert(pad_start, mi[ROW_E1]) + mi[ROW_RANK1]
    dest2 = per_expert(pad_start, mi[ROW_E2]) + mi[ROW_RANK2]
    n_parts = (T * TOP_K) // part + E
    tok = jnp.arange(T, dtype=jnp.int32)
    slot_dst = jnp.full((n_parts * part,), -1, jnp.int32).at[jnp.concatenate([dest1, dest2])].set(
        jnp.concatenate([tok, tok + T]), unique_indices=True).reshape(n_parts, part)

    def windows(a):
        return jnp.concatenate([jnp.roll(a, -s, axis=0) for s in range(ITEM_PARTS)], axis=1).reshape(n_parts, 1, tm)

    tok_win = windows(jnp.where(slot_dst < 0, 0, jnp.where(slot_dst >= T, slot_dst - T, slot_dst)))
    out_win = windows(slot_dst)
    out_win = jnp.where(out_win < 0, 2 * T + jnp.arange(tm, dtype=jnp.int32), out_win)
    n_items = (T * TOP_K) // tm + E
    items_per_expert = (padded + tm - 1) // tm
    cum_items = jnp.cumsum(items_per_expert)
    item = jnp.arange(n_items, dtype=jnp.int32)
    item_expert = jnp.minimum(jnp.sum((item[:, None] >= cum_items[None, :]).astype(jnp.int32), axis=1), E - 1)
    k = item - per_expert(cum_items - items_per_expert, item_expert)
    n_used = cum_items[-1:]
    used = item < n_used[0]
    item_rows = jnp.where(used, jnp.clip(per_expert(padded, item_expert) - k * tm, 0, tm), 0)
    item_start = jnp.where(used, (per_expert(pad_start, item_expert) + k * tm) // part, 0)
    y = _expert_ffn(x, g, tok_win, out_win, item_expert, item_start, item_rows, n_used, wg, wu, wd)
    top_w = jnp.transpose(mf[jnp.array([ROW_W1, ROW_W2])])
    return _combine(x, top_w, y, fg, final_norm)


def kernel(x, mix_norm, w_in, w_pool, pool_scale, w_gate_up, b_gate, gla_norm, w_out, ffn_norm,
           dense_w_gate, dense_w_up, dense_w_down, w_router, exp_w_gate, exp_w_up, exp_w_down, final_norm):
    B, S, D = x.shape
    depth = w_in.shape[0]
    G, C = w_pool.shape[1], w_pool.shape[2]
    pool_w = G * C
    rank, key = w_gate_up.shape[1], w_gate_up.shape[2]
    width = gla_norm.shape[1]
    gate_off = pool_w + 2 * key + width
    assert B == 1 and rank <= LANES and w_in.shape[2] == gate_off + rank + width
    xt = x.reshape(S, D)
    fg = final_norm.reshape(1, D)
    for l in range(depth):
        w = w_in[l]
        w_main = jnp.concatenate([w[:, :gate_off], w[:, gate_off + rank:]], axis=1).astype(BF16)
        w_gl = jnp.pad(w[:, gate_off:gate_off + rank], ((0, 0), (0, LANES - rank))).astype(BF16)
        z, zg = _inproj(xt, mix_norm[l].reshape(1, D), w_main, w_gl)
        pool_out = _pool(z, w_pool[l].astype(BF16), pool_scale[l].reshape(1, pool_w))
        wgu = jnp.pad(w_gate_up[l], ((0, LANES - rank), (0, 0))).astype(BF16)
        gla_out = _gla(z, zg, wgu, b_gate[l].reshape(1, key), gla_norm[l].reshape(1, width),
                       q_off=pool_w, k_off=pool_w + key, v_off=pool_w + 2 * key, r_off=gate_off,
                       key=key, width=width)
        xt = _outproj(xt, pool_out, gla_out, w_out[l].astype(BF16))
        last = l == depth - 1
        i = l // 2
        if l % 2 == 0:
            xt = _dense_ffn(xt, ffn_norm[l].reshape(1, D), dense_w_gate[i].astype(BF16),
                            dense_w_up[i].astype(BF16), dense_w_down[i].astype(BF16), fg, last)
        else:
            xt = _moe_ffn(xt, ffn_norm[l].reshape(1, D), w_router[i], exp_w_gate[i], exp_w_up[i],
                          exp_w_down[i], fg, last)
    return xt.reshape(B, S, D)
```

```python
import functools

import jax
import jax.numpy as jnp
from jax import lax
from jax.experimental import pallas as pl
from jax.experimental.pallas import tpu as pltpu

EPS = 1e-6
POOL_WINDOWS = (2, 4, 8, 16)
GLA_HEADS = 4
GATE_TAU = 16.0
CHUNK = 64
TOP_K = 2
ITEM_PARTS = 2

LANES = 128
V7X_VMEM_BYTES = 64 * 1024 * 1024
VMEM_CAP_BYTES = V7X_VMEM_BYTES - 2 * 1024 * 1024

F32 = jnp.float32
BF16 = jnp.bfloat16
HIGHEST = lax.Precision.HIGHEST


def _tile(n, pref):
    t = min(n, pref)
    while n % t:
        t -= 1
    return t


def _params(vmem_estimate_bytes, n_axes):
    limit = min(VMEM_CAP_BYTES, max(32 * 1024 * 1024, int(vmem_estimate_bytes * 1.25)))
    return pltpu.CompilerParams(
        dimension_semantics=("arbitrary",) * n_axes, vmem_limit_bytes=limit)


def _rmsnorm_rows(x_ref, g_ref, dst_ref, straight_line=False):
    rows = x_ref.shape[0]
    chunk = _tile(rows, 128)

    def body(c, carry):
        r0 = c * chunk if straight_line else pl.multiple_of(c * chunk, chunk)
        x = x_ref[pl.ds(r0, chunk), :]
        ms = jnp.mean(x * x, axis=-1, keepdims=True)
        dst_ref[pl.ds(r0, chunk), :] = (x * lax.rsqrt(ms + EPS) * g_ref[...]).astype(dst_ref.dtype)
        return carry

    if straight_line:
        for c in range(rows // chunk):
            body(c, 0)
    else:
        lax.fori_loop(0, rows // chunk, body, 0)


def _silu(a):
    return a * (1.0 / (1.0 + jnp.exp(-a)))


def _inproj_kernel(x_ref, g_ref, w_ref, wgl_ref, z_ref, zg_ref, h_scr):
    @pl.when(pl.program_id(1) == 0)
    def _():
        _rmsnorm_rows(x_ref, g_ref, h_scr)
        zg_ref[...] = jnp.dot(h_scr[...], wgl_ref[...], preferred_element_type=F32)

    z_ref[...] = jnp.dot(h_scr[...], w_ref[...], preferred_element_type=F32).astype(z_ref.dtype)


def _inproj(x, g, w_main, w_gl):
    T, D = x.shape
    N = w_main.shape[1]
    tm, tn = _tile(T, 1024), _tile(N, 1024)
    est = 2 * tm * D * 4 + tm * D * 2 + 2 * D * tn * 2 + 2 * tm * tn * 2 + 2 * D * LANES * 2 + 2 * tm * LANES * 4
    return pl.pallas_call(
        _inproj_kernel,
        out_shape=(jax.ShapeDtypeStruct((T, N), BF16), jax.ShapeDtypeStruct((T, LANES), F32)),
        grid=(T // tm, N // tn),
        in_specs=[
            pl.BlockSpec((tm, D), lambda i, j: (i, 0)),
            pl.BlockSpec((1, D), lambda i, j: (0, 0)),
            pl.BlockSpec((D, tn), lambda i, j: (0, j)),
            pl.BlockSpec((D, LANES), lambda i, j: (0, 0)),
        ],
        out_specs=(
            pl.BlockSpec((tm, tn), lambda i, j: (i, j)),
            pl.BlockSpec((tm, LANES), lambda i, j: (i, 0)),
        ),
        scratch_shapes=[pltpu.VMEM((tm, D), BF16)],
        compiler_params=_params(est, 2),
        name="inproj",
    )(x, g, w_main, w_gl)


POOL_HALO = 128


POOL_HEAD = 16


def _pool_kernel(u_ref, halo_ref, band_ref, band_h_ref, wp_ref, ps_ref, o_ref):
    i = pl.program_id(0)
    tp = u_ref.shape[0]
    C = wp_ref.shape[1]
    t1 = i * tp + lax.broadcasted_iota(jnp.int32, (tp, 1), 0) + 1
    for gi, w in enumerate(POOL_WINDOWS):
        cols = slice(gi * C, (gi + 1) * C)
        u = u_ref[:, cols]
        halo = halo_ref[:, cols]
        halo = jnp.where(i > 0, halo, jnp.zeros_like(halo))
        head = jnp.dot(band_h_ref[gi], halo, preferred_element_type=F32)
        win_sum = jnp.dot(band_ref[gi], u, preferred_element_type=F32) + jnp.concatenate(
            [head, jnp.zeros((tp - POOL_HEAD, C), F32)], axis=0)
        count = jnp.minimum(t1, w).astype(F32)
        d = win_sum / count - u.astype(F32)
        y = jnp.dot(d.astype(BF16), wp_ref[gi], preferred_element_type=F32) * ps_ref[:, cols]
        o_ref[:, cols] = y.astype(o_ref.dtype)


def _pool(z, w_pool, pool_scale):
    T = z.shape[0]
    G, C, _ = w_pool.shape
    W = G * C
    tp = _tile(T, 256)
    assert tp % POOL_HALO == 0 and POOL_HALO >= POOL_HEAD >= max(POOL_WINDOWS) - 1 and G == len(POOL_WINDOWS)
    hb = tp // POOL_HALO
    win = jnp.asarray(POOL_WINDOWS, jnp.int32)[:, None, None]
    row = jnp.arange(tp, dtype=jnp.int32)[None, :, None]
    band = ((jnp.arange(tp)[None, None, :] <= row) & (jnp.arange(tp)[None, None, :] > row - win)).astype(BF16)
    band_h = (jnp.arange(POOL_HALO)[None, None, :] >= row[:, :POOL_HEAD] + (POOL_HALO + 1) - win).astype(BF16)
    est = 2 * (tp + POOL_HALO) * W * 2 + 2 * G * C * C * 2 + 2 * tp * W * 2 + 2 * G * tp * (tp + POOL_HALO) * 2
    return pl.pallas_call(
        _pool_kernel,
        out_shape=jax.ShapeDtypeStruct((T, W), BF16),
        grid=(T // tp,),
        in_specs=[
            pl.BlockSpec((tp, W), lambda i: (i, 0)),
            pl.BlockSpec((POOL_HALO, W), lambda i: (jnp.maximum(i * hb - 1, 0), 0)),
            pl.BlockSpec((G, tp, tp), lambda i: (0, 0, 0)),
            pl.BlockSpec((G, POOL_HEAD, POOL_HALO), lambda i: (0, 0, 0)),
            pl.BlockSpec((G, C, C), lambda i: (0, 0, 0)),
            pl.BlockSpec((1, W), lambda i: (0, 0)),
        ],
        out_specs=pl.BlockSpec((tp, W), lambda i: (i, 0)),
        compiler_params=_params(est, 1),
        name="pool",
    )(z, z, band, band_h, w_pool, pool_scale)


def _log_sigmoid(x):
    return jnp.minimum(x, 0.0) - jnp.log(1.0 + jnp.exp(-jnp.abs(x)))


def _gla_gate_logits(zg_ref, wgu_ref, bg_ref):
    return jnp.dot(zg_ref[...].astype(BF16), wgu_ref[...], preferred_element_type=F32) + bg_ref[...]


def _gla_cum_log_decay(logit, tri):
    g = _log_sigmoid(logit) * (1.0 / GATE_TAU)
    g_head = g.astype(BF16)
    g_rest = (g - g_head.astype(F32)).astype(BF16)
    return (jnp.dot(tri, g_head, preferred_element_type=F32)
            + jnp.dot(tri, g_rest, preferred_element_type=F32))


def _gla_kernel(q_ref, k_ref, v_ref, r_ref, zg_ref, zg_next_ref, wgu_ref, bg_ref, gn_ref, tri_ref, o_ref,
                st_ref, bc_ref, bc_next_ref, *, dk, dv):
    @pl.when(pl.program_id(0) == 0)
    def _():
        st_ref[...] = jnp.zeros_like(st_ref)
        bc_next_ref[...] = _gla_cum_log_decay(_gla_gate_logits(zg_ref, wgu_ref, bg_ref), tri_ref[...])

    tg = q_ref.shape[0]
    n_chunks = tg // CHUNK
    heads = range(GLA_HEADS)
    ks = [slice(h * dk, (h + 1) * dk) for h in heads]
    vs = [slice(h * dv, (h + 1) * dv) for h in heads]
    ri = lax.broadcasted_iota(jnp.int32, (CHUNK, CHUNK), 0)
    ci = lax.broadcasted_iota(jnp.int32, (CHUNK, CHUNK), 1)
    causal = ci <= ri
    scale = dk ** -0.5
    nt = (((1,), (1,)), ((), ()))
    tn = (((0,), (0,)), ((), ()))

    bc_ref[...] = bc_next_ref[...]
    logit_next = _gla_gate_logits(zg_next_ref, wgu_ref, bg_ref)

    def stage_a(c):
        rows = slice(c * CHUNK, (c + 1) * CHUNK)
        bcc = bc_ref[rows, :]
        b_last = bcc[CHUNK - 1:CHUNK, :]
        kf = k_ref[rows, :].astype(F32)
        q_dec = (q_ref[rows, :].astype(F32) * scale * jnp.exp(bcc)).astype(BF16)
        k_dec = (kf * jnp.exp(-bcc)).astype(BF16)
        k_end = (kf * jnp.exp(b_last - bcc)).astype(BF16)
        v = [v_ref[rows, vs[h]] for h in heads]
        att = [lax.dot_general(q_dec[:, ks[h]], k_dec[:, ks[h]], nt, preferred_element_type=F32) for h in heads]
        kv = [lax.dot_general(v[h], k_end[:, ks[h]], tn, preferred_element_type=F32) for h in heads]
        return rows, q_dec, jnp.exp(b_last), v, att, kv

    def stage_b(rows, q_dec, decay, v, att, kv):
        s_t = [st_ref[h] for h in heads]
        o_inter = [lax.dot_general(q_dec[:, ks[h]], s_t[h].astype(BF16), nt, preferred_element_type=F32)
                   for h in heads]
        for h in heads:
            st_ref[h] = s_t[h] * decay[:, ks[h]] + kv[h]
        o_intra = [jnp.dot(jnp.where(causal, att[h], 0.0).astype(BF16), v[h], preferred_element_type=F32)
                   for h in heads]
        for h in heads:
            o = o_inter[h] + o_intra[h]
            o = o * lax.rsqrt(jnp.mean(o * o, axis=-1, keepdims=True) + EPS) * gn_ref[:, vs[h]]
            o_ref[rows, vs[h]] = (o * _silu(r_ref[rows, vs[h]].astype(F32))).astype(o_ref.dtype)

    pending = stage_a(0)
    for c in range(1, n_chunks):
        upcoming = stage_a(c)
        if c == 1:
            bc_next_ref[...] = _gla_cum_log_decay(logit_next, tri_ref[...])
        stage_b(*pending)
        pending = upcoming
    if n_chunks == 1:
        bc_next_ref[...] = _gla_cum_log_decay(logit_next, tri_ref[...])
    stage_b(*pending)


def _gla(z, zg, wgu, bg, gn, *, q_off, k_off, v_off, r_off, key, width):
    T = z.shape[0]
    dk, dv = key // GLA_HEADS, width // GLA_HEADS
    tg = _tile(T, 256)
    assert tg % CHUNK == 0 and T % CHUNK == 0
    assert q_off % key == 0 and k_off % key == 0 and v_off % width == 0 and r_off % width == 0
    est = 2 * tg * (2 * key + 3 * width) * 2 + 2 * tg * LANES * 4 + GLA_HEADS * dv * dk * 4 + 6 * tg * key * 4
    row = jnp.arange(tg, dtype=jnp.int32)
    tri = ((row[None, :] <= row[:, None]) & (row[None, :] // CHUNK == row[:, None] // CHUNK)).astype(BF16)
    return pl.pallas_call(
        functools.partial(_gla_kernel, dk=dk, dv=dv),
        out_shape=jax.ShapeDtypeStruct((T, width), BF16),
        grid=(T // tg,),
        in_specs=[
            pl.BlockSpec((tg, key), lambda i: (i, q_off // key)),
            pl.BlockSpec((tg, key), lambda i: (i, k_off // key)),
            pl.BlockSpec((tg, width), lambda i: (i, v_off // width)),
            pl.BlockSpec((tg, width), lambda i: (i, r_off // width)),
            pl.BlockSpec((tg, LANES), lambda i: (i, 0)),
            pl.BlockSpec((tg, LANES), lambda i: (jnp.minimum(i + 1, T // tg - 1), 0)),
            pl.BlockSpec((LANES, key), lambda i: (0, 0)),
            pl.BlockSpec((1, key), lambda i: (0, 0)),
            pl.BlockSpec((1, width), lambda i: (0, 0)),
            pl.BlockSpec((tg, tg), lambda i: (0, 0)),
        ],
        out_specs=pl.BlockSpec((tg, width), lambda i: (i, 0)),
        scratch_shapes=[pltpu.VMEM((GLA_HEADS, dv, dk), F32), pltpu.VMEM((tg, key), F32),
                        pltpu.VMEM((tg, key), F32)],
        compiler_params=_params(est, 1),
        name="gla",
    )(z, z, z, z, zg, zg, wgu, bg, gn, tri)


def _outproj_kernel(x_ref, p_ref, a_ref, wp_ref, wa_ref, o_ref):
    o_ref[...] = (x_ref[...]
                  + jnp.dot(p_ref[...], wp_ref[...], preferred_element_type=F32)
                  + jnp.dot(a_ref[...], wa_ref[...], preferred_element_type=F32))


def _outproj(x, pool_out, gla_out, w_out):
    T, D = x.shape
    wp_rows, wa_rows = pool_out.shape[1], gla_out.shape[1]
    tm, tn = _tile(T, 1024), _tile(D, 1024)
    assert wp_rows % tn == 0 or wp_rows == w_out.shape[0]
    est = 4 * tm * tn * 4 + 2 * tm * (wp_rows + wa_rows) * 2 + 2 * (wp_rows + wa_rows) * tn * 2
    return pl.pallas_call(
        _outproj_kernel,
        out_shape=jax.ShapeDtypeStruct((T, D), F32),
        grid=(T // tm, D // tn),
        in_specs=[
            pl.BlockSpec((tm, tn), lambda i, j: (i, j)),
            pl.BlockSpec((tm, wp_rows), lambda i, j: (i, 0)),
            pl.BlockSpec((tm, wa_rows), lambda i, j: (i, 0)),
            pl.BlockSpec((wp_rows, tn), lambda i, j: (0, j)),
            pl.BlockSpec((wa_rows, tn), lambda i, j: (wp_rows // wa_rows, j)),
        ],
        out_specs=pl.BlockSpec((tm, tn), lambda i, j: (i, j)),
        compiler_params=_params(est, 2),
        name="outproj",
    )(x, pool_out, gla_out, w_out, w_out)


def _swiglu_up(h_ref, wg_ref, wu_ref):
    h = h_ref[...]
    a = jnp.dot(h, wg_ref[...].astype(BF16), preferred_element_type=F32)
    b = jnp.dot(h, wu_ref[...].astype(BF16), preferred_element_type=F32)
    return (_silu(a) * b).astype(BF16)


def _swiglu_down(t_ref, wd_ref, acc_ref, out_ref=None):
    out_ref = acc_ref if out_ref is None else out_ref
    out_ref[...] = acc_ref[...] + jnp.dot(t_ref[...], wd_ref[...].astype(BF16), preferred_element_type=F32)


def _swiglu_phase(phase, h_ref, t_ref, wg_ref, wu_ref, wd_ref, acc_ref, side_work=None, out_ref=None):
    if phase == "first":
        t_ref[...] = _swiglu_up(h_ref, wg_ref, wu_ref)
    elif phase == "last":
        _swiglu_down(t_ref, wd_ref, acc_ref, out_ref)
    else:
        t_new = _swiglu_up(h_ref, wg_ref, wu_ref)
        if side_work is not None:
            side_work()
        _swiglu_down(t_ref, wd_ref, acc_ref)
        t_ref[...] = t_new


def _ffn_kernel(x_ref, g_ref, wg_ref, wu_ref, wd_ref, fg_ref, o_ref, h_scr, t_scr, *, final_norm):
    f = pl.program_id(1)
    nf = pl.num_programs(1) - 1

    @pl.when(f == 0)
    def _():
        _rmsnorm_rows(x_ref, g_ref, h_scr)
        o_ref[...] = x_ref[...]

    args = (h_scr, t_scr, wg_ref, wu_ref, wd_ref, o_ref)
    pl.when(f == 0)(functools.partial(_swiglu_phase, "first", *args))
    pl.when((f > 0) & (f < nf))(functools.partial(_swiglu_phase, "steady", *args))
    pl.when(f == nf)(functools.partial(_swiglu_phase, "last", *args))

    if final_norm:
        @pl.when(f == nf)
        def _():
            _rmsnorm_rows(o_ref, fg_ref, o_ref)


def _dense_ffn(x, g, wg, wu, wd, fg, final_norm):
    T, D = x.shape
    F = wg.shape[1]
    tm, tf = _tile(T, 1024), _tile(F, 512)
    nf = F // tf
    wbytes = wg.dtype.itemsize
    est = 4 * tm * D * 4 + tm * D * 2 + tm * tf * 2 + 2 * 3 * D * tf * wbytes + 3 * tm * tf * 4
    return pl.pallas_call(
        functools.partial(_ffn_kernel, final_norm=final_norm),
        out_shape=jax.ShapeDtypeStruct((T, D), F32),
        grid=(T // tm, nf + 1),
        in_specs=[
            pl.BlockSpec((tm, D), lambda i, f: (i, 0)),
            pl.BlockSpec((1, D), lambda i, f: (0, 0)),
            pl.BlockSpec((D, tf), lambda i, f: (0, jnp.minimum(f, nf - 1))),
            pl.BlockSpec((D, tf), lambda i, f: (0, jnp.minimum(f, nf - 1))),
            pl.BlockSpec((tf, D), lambda i, f: (jnp.maximum(f - 1, 0), 0)),
            pl.BlockSpec((1, D), lambda i, f: (0, 0)),
        ],
        out_specs=pl.BlockSpec((tm, D), lambda i, f: (i, 0)),
        scratch_shapes=[pltpu.VMEM((tm, D), BF16), pltpu.VMEM((tm, tf), BF16)],
        compiler_params=_params(est, 2),
        name="dense_ffn",
    )(x, g, wg, wu, wd, fg)


SUBLANES = 8
ROW_E1, ROW_E2, ROW_RANK1, ROW_RANK2 = 0, 1, 2, 3
ROW_W1, ROW_W2 = 0, 1


def _router_kernel(x_ref, g_ref, wr_ref, earlier_ref, mi_ref, mf_ref, cnt_ref, h_scr, logit_scr, run_ref, *,
                   n_experts):
    @pl.when(pl.program_id(0) == 0)
    def _():
        run_ref[...] = jnp.zeros_like(run_ref)

    tm = x_ref.shape[0]
    _rmsnorm_rows(x_ref, g_ref, h_scr)
    logit_scr[...] = jnp.dot(h_scr[...], wr_ref[...], preferred_element_type=F32)
    lt = jnp.transpose(logit_scr[...])[:SUBLANES, :]
    sub = lax.broadcasted_iota(jnp.int32, (SUBLANES, tm), 0)
    neg = jnp.float32(-jnp.inf)
    l1 = jnp.where(sub < n_experts, lt, neg)
    m1 = jnp.max(l1, axis=0, keepdims=True)
    e1 = jnp.min(jnp.where(l1 == m1, sub, SUBLANES), axis=0, keepdims=True)
    l2 = jnp.where(sub == e1, neg, l1)
    m2 = jnp.max(l2, axis=0, keepdims=True)
    e2 = jnp.min(jnp.where(l2 == m2, sub, SUBLANES), axis=0, keepdims=True)
    ex = jnp.exp(m2 - m1)
    w1 = 1.0 / (1.0 + ex)
    w2 = ex / (1.0 + ex)
    onehot = jnp.where((sub == e1) | (sub == e2), 1.0, 0.0)
    before = jnp.dot(onehot, earlier_ref[...], preferred_element_type=F32) + run_ref[:, 0:1]
    rank1 = jnp.sum(jnp.where(sub == e1, before, 0.0), axis=0, keepdims=True).astype(jnp.int32)
    rank2 = jnp.sum(jnp.where(sub == e2, before, 0.0), axis=0, keepdims=True).astype(jnp.int32)
    run_ref[...] += jnp.sum(onehot, axis=1, keepdims=True)
    mi_ref[...] = jnp.where(sub == ROW_E1, e1, jnp.where(sub == ROW_E2, e2, jnp.where(
        sub == ROW_RANK1, rank1, jnp.where(sub == ROW_RANK2, rank2, 0))))
    mf_ref[...] = jnp.where(sub == ROW_W1, w1, jnp.where(sub == ROW_W2, w2, 0.0))
    cnt_ref[...] = run_ref[...]


def _router(x, g, wr, n_experts):
    T, D = x.shape
    assert n_experts <= SUBLANES
    tm = _tile(T, 512)
    tok = jnp.arange(tm, dtype=jnp.int32)
    earlier = (tok[:, None] < tok[None, :]).astype(F32)
    est = 2 * tm * D * 4 + tm * D * 2 + 2 * tm * tm * 4 + 2 * D * LANES * 2 + 4 * tm * LANES * 4
    return pl.pallas_call(
        functools.partial(_router_kernel, n_experts=n_experts),
        out_shape=(
            jax.ShapeDtypeStruct((SUBLANES, T), jnp.int32),
            jax.ShapeDtypeStruct((SUBLANES, T), F32),
            jax.ShapeDtypeStruct((SUBLANES, LANES), F32),
        ),
        grid=(T // tm,),
        in_specs=[
            pl.BlockSpec((tm, D), lambda i: (i, 0)),
            pl.BlockSpec((1, D), lambda i: (0, 0)),
            pl.BlockSpec((D, LANES), lambda i: (0, 0)),
            pl.BlockSpec((tm, tm), lambda i: (0, 0)),
        ],
        out_specs=(
            pl.BlockSpec((SUBLANES, tm), lambda i: (0, i)),
            pl.BlockSpec((SUBLANES, tm), lambda i: (0, i)),
            pl.BlockSpec((SUBLANES, LANES), lambda i: (0, 0)),
        ),
        scratch_shapes=[pltpu.VMEM((tm, D), BF16), pltpu.VMEM((tm, LANES), F32),
                        pltpu.VMEM((SUBLANES, LANES), F32)],
        compiler_params=_params(est, 1),
        name="router",
    )(x, g, wr, earlier)


def _expert_kernel(ie_ref, ist_ref, inr_ref, nu_ref, cur_tok, nxt_tok, prv_out,
                   x_hbm, g_ref, wg_ref, wu_ref, wd_ref, y_hbm,
                   xs_scr, xb_scr, t_scr, acc_scr, out_scr, gsem, ssem, *, n_tok, rows_per_step, issue_steps):
    del ie_ref, ist_ref
    j, f = pl.program_id(0), pl.program_id(1)
    nf = pl.num_programs(1) - 1
    tm = xs_scr.shape[0]
    n_used = nu_ref[0]
    used = j < n_used

    def gather_copy(tok_ref, r):
        return pltpu.make_async_copy(x_hbm.at[pl.ds(tok_ref[0, r], 1)], xs_scr.at[pl.ds(r, 1)], gsem)

    def scatter_copy(rows_valid, r):
        dst = jnp.where(r < rows_valid, prv_out[0, r], 2 * n_tok + r)
        return pltpu.make_async_copy(out_scr.at[pl.ds(r, 1)], y_hbm.at[pl.ds(dst, 1)], ssem)

    def wait_gather():
        pltpu.make_async_copy(x_hbm.at[pl.ds(0, tm)], xs_scr, gsem).wait()

    def wait_scatter():
        pltpu.make_async_copy(out_scr, y_hbm.at[pl.ds(0, tm)], ssem).wait()

    @pl.when((j == 0) & (f == 0))
    def _():
        def issue(r, carry):
            gather_copy(cur_tok, r).start()
            return carry
        lax.fori_loop(0, tm, issue, 0)
        wait_gather()
        _rmsnorm_rows(xs_scr, g_ref, xb_scr)
        out_scr[...] = jnp.zeros_like(out_scr)

    @pl.when(used & (f == 0))
    def _():
        acc_scr[...] = jnp.zeros_like(acc_scr)

    prev_rows = jnp.where(j > 0, inr_ref[jnp.maximum(j - 1, 0)], 0)

    def side_work():
        base = (f - 1) * rows_per_step
        for rr in range(rows_per_step):
            gather_copy(nxt_tok, base + rr).start()
            scatter_copy(prev_rows, base + rr).start()

    pl.when(used & (f == nf))(wait_scatter)

    def phases(rows):
        args = (xb_scr.at[pl.ds(0, rows)], t_scr.at[pl.ds(0, rows)], wg_ref, wu_ref, wd_ref,
                acc_scr.at[pl.ds(0, rows)])
        steady = (f > 0) & (f < nf)
        pl.when(f == 0)(functools.partial(_swiglu_phase, "first", *args))
        pl.when(steady & (f <= issue_steps))(functools.partial(_swiglu_phase, "steady", *args, side_work=side_work))
        pl.when(steady & (f > issue_steps))(functools.partial(_swiglu_phase, "steady", *args))
        @pl.when(f == nf)
        def _():
            wait_gather()
            _swiglu_phase("last", *args, out_ref=out_scr.at[pl.ds(0, rows)])
            _rmsnorm_rows(xs_scr, g_ref, xb_scr, straight_line=True)

    for parts in range(1, ITEM_PARTS + 1):
        rows = parts * (tm // ITEM_PARTS)
        pl.when(used & (inr_ref[j] == rows))(functools.partial(phases, rows))

    @pl.when((j == n_used) & (f == 0))
    def _():
        def issue(r, carry):
            scatter_copy(prev_rows, r).start()
            return carry
        lax.fori_loop(0, tm, issue, 0)
        wait_scatter()


def _expert_ffn(x, g, tok_win, out_win, item_expert, item_start, item_rows, n_used, wg, wu, wd):
    T, D = x.shape
    E, _, F = wg.shape
    n_items = item_expert.shape[0]
    tm = tok_win.shape[2]
    tf = _tile(F, 512)
    nf = F // tf
    assert nf >= 2
    issue_steps = _tile(tm, nf - 1)
    rows_per_step = tm // issue_steps

    def w_col(j, f, ie, ist, inr, nu):
        return (ie[j], 0, jnp.where(j < nu[0], jnp.minimum(f, nf - 1), nf - 1))

    def w_row(j, f, ie, ist, inr, nu):
        return (ie[j], jnp.where(j < nu[0], jnp.maximum(f - 1, 0), nf - 1), 0)

    def slots_of_item(shift):
        def index_map(j, f, ie, ist, inr, nu):
            return (ist[jnp.clip(j + shift, 0, n_items - 1)], 0, 0)
        return pl.BlockSpec((None, 1, tm), index_map, memory_space=pltpu.SMEM)

    wbytes = wg.dtype.itemsize
    est = (tm * D * 4 + tm * D * 2 + tm * tf * 2 + 2 * 3 * D * tf * wbytes + 2 * tm * D * 4
           + 3 * tm * tf * 4 + 3 * D * tf * 2)
    return pl.pallas_call(
        functools.partial(_expert_kernel, n_tok=T, rows_per_step=rows_per_step, issue_steps=issue_steps),
        out_shape=jax.ShapeDtypeStruct((2 * T + tm, D), F32),
        grid_spec=pltpu.PrefetchScalarGridSpec(
            num_scalar_prefetch=4,
            grid=(n_items, nf + 1),
            in_specs=[
                slots_of_item(0), slots_of_item(1), slots_of_item(-1),
                pl.BlockSpec(memory_space=pl.ANY),
                pl.BlockSpec((1, D), lambda j, f, ie, ist, inr, nu: (0, 0)),
                pl.BlockSpec((None, D, tf), w_col),
                pl.BlockSpec((None, D, tf), w_col),
                pl.BlockSpec((None, tf, D), w_row),
            ],
            out_specs=pl.BlockSpec(memory_space=pl.ANY),
            scratch_shapes=[pltpu.VMEM((tm, D), F32), pltpu.VMEM((tm, D), BF16), pltpu.VMEM((tm, tf), BF16),
                            pltpu.VMEM((tm, D), F32), pltpu.VMEM((tm, D), F32),
                            pltpu.SemaphoreType.DMA, pltpu.SemaphoreType.DMA],
        ),
        compiler_params=_params(est, 2),
        name="expert_ffn",
    )(item_expert, item_start, item_rows, n_used, tok_win, tok_win, out_win, x, g, wg, wu, wd)


def _combine_kernel(x_ref, w_ref, fg_ref, y1_ref, y2_ref, o_ref, *, final_norm):
    w = w_ref[...]
    o_ref[...] = x_ref[...] + w[:, 0:1] * y1_ref[...] + w[:, 1:2] * y2_ref[...]
    if final_norm:
        _rmsnorm_rows(o_ref, fg_ref, o_ref)


def _combine(x, top_w, y, fg, final_norm):
    T, D = x.shape
    tc = _tile(T, 512)
    est = 8 * tc * D * 4 + 2 * tc * LANES * 4
    return pl.pallas_call(
        functools.partial(_combine_kernel, final_norm=final_norm),
        out_shape=jax.ShapeDtypeStruct((T, D), F32),
        grid=(T // tc,),
        in_specs=[
            pl.BlockSpec((tc, D), lambda i: (i, 0)),
            pl.BlockSpec((tc, TOP_K), lambda i: (i, 0)),
            pl.BlockSpec((1, D), lambda i: (0, 0)),
            pl.BlockSpec((tc, D), lambda i: (i, 0)),
            pl.BlockSpec((tc, D), lambda i: (i + T // tc, 0)),
        ],
        out_specs=pl.BlockSpec((tc, D), lambda i: (i, 0)),
        compiler_params=_params(est, 1),
        name="combine",
    )(x, top_w, fg, y, y)


def _moe_ffn(x, g, w_router, wg, wu, wd, fg, final_norm):
    T, D = x.shape
    E = w_router.shape[1]
    tm = _tile(T, 1024)
    part = tm // ITEM_PARTS
    assert (T * TOP_K) % tm == 0 and E <= part and part % 16 == 0
    wr = jnp.pad(w_router, ((0, 0), (0, LANES - E))).astype(BF16)
    mi, mf, cnt = _router(x, g, wr, E)
    experts = jnp.arange(E, dtype=jnp.int32)

    def per_expert(table, e):
        return jnp.sum(jnp.where(e[:, None] == experts[None, :], table[None, :], 0), axis=1)

    counts = cnt[:E, 0].astype(jnp.int32)
    padded = ((counts + part - 1) // part) * part
    cum_padded = jnp.cumsum(padded)
    pad_start = cum_padded - padded
    dest1 = per_expert(pad_start, mi[ROW_E1]) + mi[ROW_RANK1]
    dest2 = per_expert(pad_start, mi[ROW_E2]) + mi[ROW_RANK2]
    n_parts = (T * TOP_K) // part + E
    tok = jnp.arange(T, dtype=jnp.int32)
    slot_dst = jnp.full((n_parts * part,), -1, jnp.int32).at[jnp.concatenate([dest1, dest2])].set(
        jnp.concatenate([tok, tok + T]), unique_indices=True).reshape(n_parts, part)

    def windows(a):
        return jnp.concatenate([jnp.roll(a, -s, axis=0) for s in range(ITEM_PARTS)], axis=1).reshape(n_parts, 1, tm)

    tok_win = windows(jnp.where(slot_dst < 0, 0, jnp.where(slot_dst >= T, slot_dst - T, slot_dst)))
    out_win = windows(slot_dst)
    out_win = jnp.where(out_win < 0, 2 * T + jnp.arange(tm, dtype=jnp.int32), out_win)
    n_items = (T * TOP_K) // tm + E
    items_per_expert = (padded + tm - 1) // tm
    cum_items = jnp.cumsum(items_per_expert)
    item = jnp.arange(n_items, dtype=jnp.int32)
    item_expert = jnp.minimum(jnp.sum((item[:, None] >= cum_items[None, :]).astype(jnp.int32), axis=1), E - 1)
    k = item - per_expert(cum_items - items_per_expert, item_expert)
    n_used = cum_items[-1:]
    used = item < n_used[0]
    item_rows = jnp.where(used, jnp.clip(per_expert(padded, item_expert) - k * tm, 0, tm), 0)
    item_start = jnp.where(used, (per_expert(pad_start, item_expert) + k * tm) // part, 0)
    y = _expert_ffn(x, g, tok_win, out_win, item_expert, item_start, item_rows, n_used, wg, wu, wd)
    top_w = jnp.transpose(mf[jnp.array([ROW_W1, ROW_W2])])
    return _combine(x, top_w, y, fg, final_norm)


def kernel(x, mix_norm, w_in, w_pool, pool_scale, w_gate_up, b_gate, gla_norm, w_out, ffn_norm,
           dense_w_gate, dense_w_up, dense_w_down, w_router, exp_w_gate, exp_w_up, exp_w_down, final_norm):
    B, S, D = x.shape
    depth = w_in.shape[0]
    G, C = w_pool.shape[1], w_pool.shape[2]
    pool_w = G * C
    rank, key = w_gate_up.shape[1], w_gate_up.shape[2]
    width = gla_norm.shape[1]
    gate_off = pool_w + 2 * key + width
    assert B == 1 and rank <= LANES and w_in.shape[2] == gate_off + rank + width
    xt = x.reshape(S, D)
    fg = final_norm.reshape(1, D)
    for l in range(depth):
        w = w_in[l]
        n_main = gate_off + width
        col = jnp.arange(n_main, dtype=jnp.int32)[None, :]
        w_main = jnp.where(col < gate_off, w[:, :n_main], w[:, rank:]).astype(BF16)
        w_gl = jnp.pad(w[:, gate_off:gate_off + rank], ((0, 0), (0, LANES - rank))).astype(BF16)
        z, zg = _inproj(xt, mix_norm[l].reshape(1, D), w_main, w_gl)
        pool_out = _pool(z, w_pool[l].astype(BF16), pool_scale[l].reshape(1, pool_w))
        wgu = jnp.pad(w_gate_up[l], ((0, LANES - rank), (0, 0))).astype(BF16)
        gla_out = _gla(z, zg, wgu, b_gate[l].reshape(1, key), gla_norm[l].reshape(1, width),
                       q_off=pool_w, k_off=pool_w + key, v_off=pool_w + 2 * key, r_off=gate_off,
                       key=key, width=width)
        xt = _outproj(xt, pool_out, gla_out, w_out[l].astype(BF16))
        last = l == depth - 1
        i = l // 2
        if l % 2 == 0:
            xt = _dense_ffn(xt, ffn_norm[l].reshape(1, D), dense_w_gate[i].astype(BF16),
                            dense_w_up[i].astype(BF16), dense_w_down[i].astype(BF16), fg, last)
        else:
            xt = _moe_ffn(xt, ffn_norm[l].reshape(1, D), w_router[i], exp_w_gate[i], exp_w_up[i],
                          exp_w_down[i], fg, last)
    return xt.reshape(B, S, D)
```

```python
import functools

import jax
import jax.numpy as jnp
from jax import lax
from jax.experimental import pallas as pl
from jax.experimental.pallas import tpu as pltpu

EPS = 1e-6
POOL_WINDOWS = (2, 4, 8, 16)
GLA_HEADS = 4
GATE_TAU = 16.0
CHUNK = 64
TOP_K = 2
ITEM_PARTS = 2

LANES = 128
V7X_VMEM_BYTES = 64 * 1024 * 1024
VMEM_CAP_BYTES = V7X_VMEM_BYTES - 2 * 1024 * 1024

F32 = jnp.float32
BF16 = jnp.bfloat16
HIGHEST = lax.Precision.HIGHEST


def _tile(n, pref):
    t = min(n, pref)
    while n % t:
        t -= 1
    return t


def _params(vmem_estimate_bytes, n_axes):
    limit = min(VMEM_CAP_BYTES, max(32 * 1024 * 1024, int(vmem_estimate_bytes * 1.25)))
    return pltpu.CompilerParams(
        dimension_semantics=("arbitrary",) * n_axes, vmem_limit_bytes=limit)


def _rmsnorm_rows(x_ref, g_ref, dst_ref, straight_line=False):
    rows = x_ref.shape[0]
    chunk = _tile(rows, 128)

    def body(c, carry):
        r0 = c * chunk if straight_line else pl.multiple_of(c * chunk, chunk)
        x = x_ref[pl.ds(r0, chunk), :]
        ms = jnp.mean(x * x, axis=-1, keepdims=True)
        dst_ref[pl.ds(r0, chunk), :] = (x * lax.rsqrt(ms + EPS) * g_ref[...]).astype(dst_ref.dtype)
        return carry

    if straight_line:
        for c in range(rows // chunk):
            body(c, 0)
    else:
        lax.fori_loop(0, rows // chunk, body, 0)


def _silu(a):
    return a * (1.0 / (1.0 + jnp.exp(-a)))


def _inproj_kernel(x_ref, g_ref, w_ref, wgl_ref, z_ref, zg_ref, h_scr):
    @pl.when(pl.program_id(1) == 0)
    def _():
        _rmsnorm_rows(x_ref, g_ref, h_scr)
        zg_ref[...] = jnp.dot(h_scr[...], wgl_ref[...], preferred_element_type=F32)

    z_ref[...] = jnp.dot(h_scr[...], w_ref[...], preferred_element_type=F32).astype(z_ref.dtype)


def _inproj(x, g, w_main, w_gl):
    T, D = x.shape
    N = w_main.shape[1]
    tm, tn = _tile(T, 1024), _tile(N, 1024)
    est = 2 * tm * D * 4 + tm * D * 2 + 2 * D * tn * 2 + 2 * tm * tn * 2 + 2 * D * LANES * 2 + 2 * tm * LANES * 4
    return pl.pallas_call(
        _inproj_kernel,
        out_shape=(jax.ShapeDtypeStruct((T, N), BF16), jax.ShapeDtypeStruct((T, LANES), F32)),
        grid=(T // tm, N // tn),
        in_specs=[
            pl.BlockSpec((tm, D), lambda i, j: (i, 0)),
            pl.BlockSpec((1, D), lambda i, j: (0, 0)),
            pl.BlockSpec((D, tn), lambda i, j: (0, j)),
            pl.BlockSpec((D, LANES), lambda i, j: (0, 0)),
        ],
        out_specs=(
            pl.BlockSpec((tm, tn), lambda i, j: (i, j)),
            pl.BlockSpec((tm, LANES), lambda i, j: (i, 0)),
        ),
        scratch_shapes=[pltpu.VMEM((tm, D), BF16)],
        compiler_params=_params(est, 2),
        name="inproj",
    )(x, g, w_main, w_gl)


POOL_HALO = 128


POOL_HEAD = 16


def _pool_kernel(u_ref, halo_ref, band_ref, band_h_ref, wp_ref, ps_ref, o_ref):
    i = pl.program_id(0)
    tp = u_ref.shape[0]
    C = wp_ref.shape[1]
    t1 = i * tp + lax.broadcasted_iota(jnp.int32, (tp, 1), 0) + 1
    for gi, w in enumerate(POOL_WINDOWS):
        cols = slice(gi * C, (gi + 1) * C)
        u = u_ref[:, cols]
        halo = halo_ref[:, cols]
        halo = jnp.where(i > 0, halo, jnp.zeros_like(halo))
        head = jnp.dot(band_h_ref[gi], halo, preferred_element_type=F32)
        win_sum = jnp.dot(band_ref[gi], u, preferred_element_type=F32) + jnp.concatenate(
            [head, jnp.zeros((tp - POOL_HEAD, C), F32)], axis=0)
        count = jnp.minimum(t1, w).astype(F32)
        d = win_sum / count - u.astype(F32)
        y = jnp.dot(d.astype(BF16), wp_ref[gi], preferred_element_type=F32) * ps_ref[:, cols]
        o_ref[:, cols] = y.astype(o_ref.dtype)


def _pool(z, w_pool, pool_scale):
    T = z.shape[0]
    G, C, _ = w_pool.shape
    W = G * C
    tp = _tile(T, 256)
    assert tp % POOL_HALO == 0 and POOL_HALO >= POOL_HEAD >= max(POOL_WINDOWS) - 1 and G == len(POOL_WINDOWS)
    hb = tp // POOL_HALO
    win = jnp.asarray(POOL_WINDOWS, jnp.int32)[:, None, None]
    row = jnp.arange(tp, dtype=jnp.int32)[None, :, None]
    band = ((jnp.arange(tp)[None, None, :] <= row) & (jnp.arange(tp)[None, None, :] > row - win)).astype(BF16)
    band_h = (jnp.arange(POOL_HALO)[None, None, :] >= row[:, :POOL_HEAD] + (POOL_HALO + 1) - win).astype(BF16)
    est = 2 * (tp + POOL_HALO) * W * 2 + 2 * G * C * C * 2 + 2 * tp * W * 2 + 2 * G * tp * (tp + POOL_HALO) * 2
    return pl.pallas_call(
        _pool_kernel,
        out_shape=jax.ShapeDtypeStruct((T, W), BF16),
        grid=(T // tp,),
        in_specs=[
            pl.BlockSpec((tp, W), lambda i: (i, 0)),
            pl.BlockSpec((POOL_HALO, W), lambda i: (jnp.maximum(i * hb - 1, 0), 0)),
            pl.BlockSpec((G, tp, tp), lambda i: (0, 0, 0)),
            pl.BlockSpec((G, POOL_HEAD, POOL_HALO), lambda i: (0, 0, 0)),
            pl.BlockSpec((G, C, C), lambda i: (0, 0, 0)),
            pl.BlockSpec((1, W), lambda i: (0, 0)),
        ],
        out_specs=pl.BlockSpec((tp, W), lambda i: (i, 0)),
        compiler_params=_params(est, 1),
        name="pool",
    )(z, z, band, band_h, w_pool, pool_scale)


def _log_sigmoid(x):
    return jnp.minimum(x, 0.0) - jnp.log(1.0 + jnp.exp(-jnp.abs(x)))


def _gla_gate_logits(zg_ref, wgu_ref, bg_ref):
    return jnp.dot(zg_ref[...].astype(BF16), wgu_ref[...], preferred_element_type=F32) + bg_ref[...]


def _gla_cum_log_decay(logit, tri):
    g = _log_sigmoid(logit) * (1.0 / GATE_TAU)
    g_head = g.astype(BF16)
    g_rest = (g - g_head.astype(F32)).astype(BF16)
    return (jnp.dot(tri, g_head, preferred_element_type=F32)
            + jnp.dot(tri, g_rest, preferred_element_type=F32))


def _gla_kernel(q_ref, k_ref, v_ref, r_ref, zg_ref, zg_next_ref, wgu_ref, bg_ref, gn_ref, tri_ref, o_ref,
                st_ref, bc_ref, bc_next_ref, *, dk, dv):
    @pl.when(pl.program_id(0) == 0)
    def _():
        st_ref[...] = jnp.zeros_like(st_ref)
        bc_next_ref[...] = _gla_cum_log_decay(_gla_gate_logits(zg_ref, wgu_ref, bg_ref), tri_ref[...])

    tg = q_ref.shape[0]
    n_chunks = tg // CHUNK
    heads = range(GLA_HEADS)
    ks = [slice(h * dk, (h + 1) * dk) for h in heads]
    vs = [slice(h * dv, (h + 1) * dv) for h in heads]
    ri = lax.broadcasted_iota(jnp.int32, (CHUNK, CHUNK), 0)
    ci = lax.broadcasted_iota(jnp.int32, (CHUNK, CHUNK), 1)
    causal = ci <= ri
    scale = dk ** -0.5
    nt = (((1,), (1,)), ((), ()))
    tn = (((0,), (0,)), ((), ()))

    bc_ref[...] = bc_next_ref[...]
    logit_next = _gla_gate_logits(zg_next_ref, wgu_ref, bg_ref)

    def stage_a(c):
        rows = slice(c * CHUNK, (c + 1) * CHUNK)
        bcc = bc_ref[rows, :]
        b_last = bcc[CHUNK - 1:CHUNK, :]
        kf = k_ref[rows, :].astype(F32)
        q_dec = (q_ref[rows, :].astype(F32) * scale * jnp.exp(bcc)).astype(BF16)
        k_dec = (kf * jnp.exp(-bcc)).astype(BF16)
        k_end = (kf * jnp.exp(b_last - bcc)).astype(BF16)
        v = [v_ref[rows, vs[h]] for h in heads]
        att = [lax.dot_general(q_dec[:, ks[h]], k_dec[:, ks[h]], nt, preferred_element_type=F32) for h in heads]
        kv = [lax.dot_general(v[h], k_end[:, ks[h]], tn, preferred_element_type=F32) for h in heads]
        return rows, q_dec, jnp.exp(b_last), v, att, kv

    def stage_b(rows, q_dec, decay, v, att, kv):
        s_t = [st_ref[h] for h in heads]
        o_inter = [lax.dot_general(q_dec[:, ks[h]], s_t[h].astype(BF16), nt, preferred_element_type=F32)
                   for h in heads]
        for h in heads:
            st_ref[h] = s_t[h] * decay[:, ks[h]] + kv[h]
        o_intra = [jnp.dot(jnp.where(causal, att[h], 0.0).astype(BF16), v[h], preferred_element_type=F32)
                   for h in heads]
        for h in heads:
            o = o_inter[h] + o_intra[h]
            o = o * lax.rsqrt(jnp.mean(o * o, axis=-1, keepdims=True) + EPS) * gn_ref[:, vs[h]]
            o_ref[rows, vs[h]] = (o * _silu(r_ref[rows, vs[h]].astype(F32))).astype(o_ref.dtype)

    pending = stage_a(0)
    for c in range(1, n_chunks):
        upcoming = stage_a(c)
        if c == 1:
            bc_next_ref[...] = _gla_cum_log_decay(logit_next, tri_ref[...])
        stage_b(*pending)
        pending = upcoming
    if n_chunks == 1:
        bc_next_ref[...] = _gla_cum_log_decay(logit_next, tri_ref[...])
    stage_b(*pending)


def _gla(z, zg, wgu, bg, gn, *, q_off, k_off, v_off, r_off, key, width):
    T = z.shape[0]
    dk, dv = key // GLA_HEADS, width // GLA_HEADS
    tg = _tile(T, 256)
    assert tg % CHUNK == 0 and T % CHUNK == 0
    assert q_off % key == 0 and k_off % key == 0 and v_off % width == 0 and r_off % width == 0
    est = 2 * tg * (2 * key + 3 * width) * 2 + 2 * tg * LANES * 4 + GLA_HEADS * dv * dk * 4 + 6 * tg * key * 4
    row = jnp.arange(tg, dtype=jnp.int32)
    tri = ((row[None, :] <= row[:, None]) & (row[None, :] // CHUNK == row[:, None] // CHUNK)).astype(BF16)
    return pl.pallas_call(
        functools.partial(_gla_kernel, dk=dk, dv=dv),
        out_shape=jax.ShapeDtypeStruct((T, width), BF16),
        grid=(T // tg,),
        in_specs=[
            pl.BlockSpec((tg, key), lambda i: (i, q_off // key)),
            pl.BlockSpec((tg, key), lambda i: (i, k_off // key)),
            pl.BlockSpec((tg, width), lambda i: (i, v_off // width)),
            pl.BlockSpec((tg, width), lambda i: (i, r_off // width)),
            pl.BlockSpec((tg, LANES), lambda i: (i, 0)),
            pl.BlockSpec((tg, LANES), lambda i: (jnp.minimum(i + 1, T // tg - 1), 0)),
            pl.BlockSpec((LANES, key), lambda i: (0, 0)),
            pl.BlockSpec((1, key), lambda i: (0, 0)),
            pl.BlockSpec((1, width), lambda i: (0, 0)),
            pl.BlockSpec((tg, tg), lambda i: (0, 0)),
        ],
        out_specs=pl.BlockSpec((tg, width), lambda i: (i, 0)),
        scratch_shapes=[pltpu.VMEM((GLA_HEADS, dv, dk), F32), pltpu.VMEM((tg, key), F32),
                        pltpu.VMEM((tg, key), F32)],
        compiler_params=_params(est, 1),
        name="gla",
    )(z, z, z, z, zg, zg, wgu, bg, gn, tri)


def _outproj_kernel(x_ref, p_ref, a_ref, wp_ref, wa_ref, o_ref):
    o_ref[...] = (x_ref[...]
                  + jnp.dot(p_ref[...], wp_ref[...], preferred_element_type=F32)
                  + jnp.dot(a_ref[...], wa_ref[...], preferred_element_type=F32))


def _outproj(x, pool_out, gla_out, w_out):
    T, D = x.shape
    wp_rows, wa_rows = pool_out.shape[1], gla_out.shape[1]
    tm, tn = _tile(T, 1024), _tile(D, 1024)
    assert wp_rows % tn == 0 or wp_rows == w_out.shape[0]
    est = 4 * tm * tn * 4 + 2 * tm * (wp_rows + wa_rows) * 2 + 2 * (wp_rows + wa_rows) * tn * 2
    return pl.pallas_call(
        _outproj_kernel,
        out_shape=jax.ShapeDtypeStruct((T, D), F32),
        grid=(T // tm, D // tn),
        in_specs=[
            pl.BlockSpec((tm, tn), lambda i, j: (i, j)),
            pl.BlockSpec((tm, wp_rows), lambda i, j: (i, 0)),
            pl.BlockSpec((tm, wa_rows), lambda i, j: (i, 0)),
            pl.BlockSpec((wp_rows, tn), lambda i, j: (0, j)),
            pl.BlockSpec((wa_rows, tn), lambda i, j: (wp_rows // wa_rows, j)),
        ],
        out_specs=pl.BlockSpec((tm, tn), lambda i, j: (i, j)),
        compiler_params=_params(est, 2),
        name="outproj",
    )(x, pool_out, gla_out, w_out, w_out)


def _swiglu_up(h_ref, wg_ref, wu_ref):
    h = h_ref[...]
    a = jnp.dot(h, wg_ref[...].astype(BF16), preferred_element_type=F32)
    b = jnp.dot(h, wu_ref[...].astype(BF16), preferred_element_type=F32)
    return (_silu(a) * b).astype(BF16)


def _swiglu_down(t_ref, wd_ref, acc_ref, out_ref=None):
    out_ref = acc_ref if out_ref is None else out_ref
    out_ref[...] = acc_ref[...] + jnp.dot(t_ref[...], wd_ref[...].astype(BF16), preferred_element_type=F32)


def _swiglu_phase(phase, h_ref, t_ref, wg_ref, wu_ref, wd_ref, acc_ref, side_work=None, out_ref=None):
    if phase == "first":
        t_ref[...] = _swiglu_up(h_ref, wg_ref, wu_ref)
    elif phase == "last":
        _swiglu_down(t_ref, wd_ref, acc_ref, out_ref)
    else:
        t_new = _swiglu_up(h_ref, wg_ref, wu_ref)
        if side_work is not None:
            side_work()
        _swiglu_down(t_ref, wd_ref, acc_ref)
        t_ref[...] = t_new


def _ffn_kernel(x_ref, g_ref, wg_ref, wu_ref, wd_ref, fg_ref, o_ref, h_scr, t_scr, *, final_norm):
    f = pl.program_id(1)
    nf = pl.num_programs(1) - 1

    @pl.when(f == 0)
    def _():
        _rmsnorm_rows(x_ref, g_ref, h_scr)
        o_ref[...] = x_ref[...]

    args = (h_scr, t_scr, wg_ref, wu_ref, wd_ref, o_ref)
    pl.when(f == 0)(functools.partial(_swiglu_phase, "first", *args))
    pl.when((f > 0) & (f < nf))(functools.partial(_swiglu_phase, "steady", *args))
    pl.when(f == nf)(functools.partial(_swiglu_phase, "last", *args))

    if final_norm:
        @pl.when(f == nf)
        def _():
            _rmsnorm_rows(o_ref, fg_ref, o_ref)


def _dense_ffn(x, g, wg, wu, wd, fg, final_norm):
    T, D = x.shape
    F = wg.shape[1]
    tm, tf = _tile(T, 1024), _tile(F, 512)
    nf = F // tf
    wbytes = wg.dtype.itemsize
    est = 4 * tm * D * 4 + tm * D * 2 + tm * tf * 2 + 2 * 3 * D * tf * wbytes + 3 * tm * tf * 4
    return pl.pallas_call(
        functools.partial(_ffn_kernel, final_norm=final_norm),
        out_shape=jax.ShapeDtypeStruct((T, D), F32),
        grid=(T // tm, nf + 1),
        in_specs=[
            pl.BlockSpec((tm, D), lambda i, f: (i, 0)),
            pl.BlockSpec((1, D), lambda i, f: (0, 0)),
            pl.BlockSpec((D, tf), lambda i, f: (0, jnp.minimum(f, nf - 1))),
            pl.BlockSpec((D, tf), lambda i, f: (0, jnp.minimum(f, nf - 1))),
            pl.BlockSpec((tf, D), lambda i, f: (jnp.maximum(f - 1, 0), 0)),
            pl.BlockSpec((1, D), lambda i, f: (0, 0)),
        ],
        out_specs=pl.BlockSpec((tm, D), lambda i, f: (i, 0)),
        scratch_shapes=[pltpu.VMEM((tm, D), BF16), pltpu.VMEM((tm, tf), BF16)],
        compiler_params=_params(est, 2),
        name="dense_ffn",
    )(x, g, wg, wu, wd, fg)


SUBLANES = 8
ROW_E1, ROW_E2, ROW_RANK1, ROW_RANK2 = 0, 1, 2, 3
ROW_W1, ROW_W2 = 0, 1


def _router_kernel(x_ref, g_ref, wr_ref, earlier_ref, mi_ref, mf_ref, cnt_ref, h_scr, logit_scr, run_ref, *,
                   n_experts):
    @pl.when(pl.program_id(0) == 0)
    def _():
        run_ref[...] = jnp.zeros_like(run_ref)

    tm = x_ref.shape[0]
    _rmsnorm_rows(x_ref, g_ref, h_scr)
    logit_scr[...] = jnp.dot(h_scr[...], wr_ref[...], preferred_element_type=F32)
    lt = jnp.transpose(logit_scr[...])[:SUBLANES, :]
    sub = lax.broadcasted_iota(jnp.int32, (SUBLANES, tm), 0)
    neg = jnp.float32(-jnp.inf)
    l1 = jnp.where(sub < n_experts, lt, neg)
    m1 = jnp.max(l1, axis=0, keepdims=True)
    e1 = jnp.min(jnp.where(l1 == m1, sub, SUBLANES), axis=0, keepdims=True)
    l2 = jnp.where(sub == e1, neg, l1)
    m2 = jnp.max(l2, axis=0, keepdims=True)
    e2 = jnp.min(jnp.where(l2 == m2, sub, SUBLANES), axis=0, keepdims=True)
    ex = jnp.exp(m2 - m1)
    w1 = 1.0 / (1.0 + ex)
    w2 = ex / (1.0 + ex)
    onehot = jnp.where((sub == e1) | (sub == e2), 1.0, 0.0)
    before = jnp.dot(onehot, earlier_ref[...], preferred_element_type=F32) + run_ref[:, 0:1]
    rank1 = jnp.sum(jnp.where(sub == e1, before, 0.0), axis=0, keepdims=True).astype(jnp.int32)
    rank2 = jnp.sum(jnp.where(sub == e2, before, 0.0), axis=0, keepdims=True).astype(jnp.int32)
    run_ref[...] += jnp.sum(onehot, axis=1, keepdims=True)
    mi_ref[...] = jnp.where(sub == ROW_E1, e1, jnp.where(sub == ROW_E2, e2, jnp.where(
        sub == ROW_RANK1, rank1, jnp.where(sub == ROW_RANK2, rank2, 0))))
    mf_ref[...] = jnp.where(sub == ROW_W1, w1, jnp.where(sub == ROW_W2, w2, 0.0))
    cnt_ref[...] = run_ref[...]


def _router(x, g, wr, n_experts):
    T, D = x.shape
    assert n_experts <= SUBLANES
    tm = _tile(T, 512)
    tok = jnp.arange(tm, dtype=jnp.int32)
    earlier = (tok[:, None] < tok[None, :]).astype(F32)
    est = 2 * tm * D * 4 + tm * D * 2 + 2 * tm * tm * 4 + 2 * D * LANES * 2 + 4 * tm * LANES * 4
    return pl.pallas_call(
        functools.partial(_router_kernel, n_experts=n_experts),
        out_shape=(
            jax.ShapeDtypeStruct((SUBLANES, T), jnp.int32),
            jax.ShapeDtypeStruct((SUBLANES, T), F32),
            jax.ShapeDtypeStruct((SUBLANES, LANES), F32),
        ),
        grid=(T // tm,),
        in_specs=[
            pl.BlockSpec((tm, D), lambda i: (i, 0)),
            pl.BlockSpec((1, D), lambda i: (0, 0)),
            pl.BlockSpec((D, LANES), lambda i: (0, 0)),
            pl.BlockSpec((tm, tm), lambda i: (0, 0)),
        ],
        out_specs=(
            pl.BlockSpec((SUBLANES, tm), lambda i: (0, i)),
            pl.BlockSpec((SUBLANES, tm), lambda i: (0, i)),
            pl.BlockSpec((SUBLANES, LANES), lambda i: (0, 0)),
        ),
        scratch_shapes=[pltpu.VMEM((tm, D), BF16), pltpu.VMEM((tm, LANES), F32),
                        pltpu.VMEM((SUBLANES, LANES), F32)],
        compiler_params=_params(est, 1),
        name="router",
    )(x, g, wr, earlier)


def _expert_kernel(ie_ref, ist_ref, inr_ref, nu_ref, cur_tok, nxt_tok, prv_out,
                   x_hbm, g_ref, wg_ref, wu_ref, wd_ref, y_hbm,
                   xs_scr, xb_scr, t_scr, acc_scr, out_scr, gsem, ssem, *, n_tok, rows_per_step, n_issue):
    del ie_ref, ist_ref
    j, f = pl.program_id(0), pl.program_id(1)
    nf = pl.num_programs(1) - 1
    tm = xb_scr.shape[0]
    n_used = nu_ref[0]
    used = j < n_used
    xs_rows = xs_scr.at[pl.ds(0, tm)]

    def gather_copy(tok_ref, r):
        tok = tok_ref[0, jnp.minimum(r, tm - 1)]
        return pltpu.make_async_copy(x_hbm.at[pl.ds(tok, 1)], xs_scr.at[pl.ds(r, 1)], gsem)

    def scatter_copy(rows_valid, r):
        rc = jnp.minimum(r, tm - 1)
        dst = jnp.where(r < rows_valid, prv_out[0, rc], 2 * n_tok + r)
        return pltpu.make_async_copy(out_scr.at[pl.ds(rc, 1)], y_hbm.at[pl.ds(dst, 1)], ssem)

    def wait_gather():
        pltpu.make_async_copy(x_hbm.at[pl.ds(0, n_issue)], xs_scr.at[pl.ds(0, n_issue)], gsem).wait()

    def wait_scatter():
        pltpu.make_async_copy(xs_scr.at[pl.ds(0, n_issue)], y_hbm.at[pl.ds(0, n_issue)], ssem).wait()

    @pl.when((j == 0) & (f == 0))
    def _():
        def issue(r, carry):
            gather_copy(cur_tok, r).start()
            return carry
        lax.fori_loop(0, n_issue, issue, 0)
        wait_gather()
        _rmsnorm_rows(xs_rows, g_ref, xb_scr)
        out_scr[...] = jnp.zeros_like(out_scr)

    @pl.when(used & (f == 0))
    def _():
        acc_scr[...] = jnp.zeros_like(acc_scr)

    prev_rows = jnp.where(j > 0, inr_ref[jnp.maximum(j - 1, 0)], 0)

    def side_work():
        base = (f - 1) * rows_per_step
        for rr in range(rows_per_step):
            gather_copy(nxt_tok, base + rr).start()
            scatter_copy(prev_rows, base + rr).start()

    pl.when(used & (f == nf))(wait_scatter)

    def phases(rows):
        args = (xb_scr.at[pl.ds(0, rows)], t_scr.at[pl.ds(0, rows)], wg_ref, wu_ref, wd_ref,
                acc_scr.at[pl.ds(0, rows)])
        steady = (f > 0) & (f < nf)
        pl.when(f == 0)(functools.partial(_swiglu_phase, "first", *args))
        pl.when(steady)(functools.partial(_swiglu_phase, "steady", *args, side_work=side_work))
        @pl.when(f == nf)
        def _():
            wait_gather()
            _swiglu_phase("last", *args, out_ref=out_scr.at[pl.ds(0, rows)])
            _rmsnorm_rows(xs_rows, g_ref, xb_scr, straight_line=True)

    for parts in range(1, ITEM_PARTS + 1):
        rows = parts * (tm // ITEM_PARTS)
        pl.when(used & (inr_ref[j] == rows))(functools.partial(phases, rows))

    @pl.when((j == n_used) & (f == 0))
    def _():
        def issue(r, carry):
            scatter_copy(prev_rows, r).start()
            return carry
        lax.fori_loop(0, n_issue, issue, 0)
        wait_scatter()


def _expert_ffn(x, g, tok_win, out_win, item_expert, item_start, item_rows, n_used, wg, wu, wd):
    T, D = x.shape
    E, _, F = wg.shape
    n_items = item_expert.shape[0]
    tm = tok_win.shape[2]
    tf = _tile(F, 512)
    nf = F // tf
    assert nf >= 2
    rows_per_step = -(-tm // ((nf - 1) * SUBLANES)) * SUBLANES
    n_issue = issue_rows = rows_per_step * (nf - 1)

    def w_col(j, f, ie, ist, inr, nu):
        return (ie[j], 0, jnp.where(j < nu[0], jnp.minimum(f, nf - 1), nf - 1))

    def w_row(j, f, ie, ist, inr, nu):
        return (ie[j], jnp.where(j < nu[0], jnp.maximum(f - 1, 0), nf - 1), 0)

    def slots_of_item(shift):
        def index_map(j, f, ie, ist, inr, nu):
            return (ist[jnp.clip(j + shift, 0, n_items - 1)], 0, 0)
        return pl.BlockSpec((None, 1, tm), index_map, memory_space=pltpu.SMEM)

    wbytes = wg.dtype.itemsize
    est = (tm * D * 4 + tm * D * 2 + tm * tf * 2 + 2 * 3 * D * tf * wbytes + 2 * tm * D * 4
           + 3 * tm * tf * 4 + 3 * D * tf * 2)
    return pl.pallas_call(
        functools.partial(_expert_kernel, n_tok=T, rows_per_step=rows_per_step, n_issue=n_issue),
        out_shape=jax.ShapeDtypeStruct((2 * T + issue_rows, D), F32),
        grid_spec=pltpu.PrefetchScalarGridSpec(
            num_scalar_prefetch=4,
            grid=(n_items, nf + 1),
            in_specs=[
                slots_of_item(0), slots_of_item(1), slots_of_item(-1),
                pl.BlockSpec(memory_space=pl.ANY),
                pl.BlockSpec((1, D), lambda j, f, ie, ist, inr, nu: (0, 0)),
                pl.BlockSpec((None, D, tf), w_col),
                pl.BlockSpec((None, D, tf), w_col),
                pl.BlockSpec((None, tf, D), w_row),
            ],
            out_specs=pl.BlockSpec(memory_space=pl.ANY),
            scratch_shapes=[pltpu.VMEM((issue_rows, D), F32), pltpu.VMEM((tm, D), BF16), pltpu.VMEM((tm, tf), BF16),
                            pltpu.VMEM((tm, D), F32), pltpu.VMEM((tm, D), F32),
                            pltpu.SemaphoreType.DMA, pltpu.SemaphoreType.DMA],
        ),
        compiler_params=_params(est, 2),
        name="expert_ffn",
    )(item_expert, item_start, item_rows, n_used, tok_win, tok_win, out_win, x, g, wg, wu, wd)


def _combine_kernel(x_ref, w_ref, fg_ref, y1_ref, y2_ref, o_ref, *, final_norm):
    w = w_ref[...]
    o_ref[...] = x_ref[...] + w[:, 0:1] * y1_ref[...] + w[:, 1:2] * y2_ref[...]
    if final_norm:
        _rmsnorm_rows(o_ref, fg_ref, o_ref)


def _combine(x, top_w, y, fg, final_norm):
    T, D = x.shape
    tc = _tile(T, 512)
    est = 8 * tc * D * 4 + 2 * tc * LANES * 4
    return pl.pallas_call(
        functools.partial(_combine_kernel, final_norm=final_norm),
        out_shape=jax.ShapeDtypeStruct((T, D), F32),
        grid=(T // tc,),
        in_specs=[
            pl.BlockSpec((tc, D), lambda i: (i, 0)),
            pl.BlockSpec((tc, TOP_K), lambda i: (i, 0)),
            pl.BlockSpec((1, D), lambda i: (0, 0)),
            pl.BlockSpec((tc, D), lambda i: (i, 0)),
            pl.BlockSpec((tc, D), lambda i: (i + T // tc, 0)),
        ],
        out_specs=pl.BlockSpec((tc, D), lambda i: (i, 0)),
        compiler_params=_params(est, 1),
        name="combine",
    )(x, top_w, fg, y, y)


def _moe_ffn(x, g, w_router, wg, wu, wd, fg, final_norm):
    T, D = x.shape
    E = w_router.shape[1]
    tm = _tile(T, 1024)
    part = tm // ITEM_PARTS
    assert (T * TOP_K) % tm == 0 and E <= part and part % 16 == 0
    wr = jnp.pad(w_router, ((0, 0), (0, LANES - E))).astype(BF16)
    mi, mf, cnt = _router(x, g, wr, E)
    experts = jnp.arange(E, dtype=jnp.int32)

    def per_expert(table, e):
        return jnp.sum(jnp.where(e[:, None] == experts[None, :], table[None, :], 0), axis=1)

    counts = cnt[:E, 0].astype(jnp.int32)
    padded = ((counts + part - 1) // part) * part
    cum_padded = jnp.cumsum(padded)
    pad_start = cum_padded - padded
    dest1 = per_expert(pad_start, mi[ROW_E1]) + mi[ROW_RANK1]
    dest2 = per_expert(pad_start, mi[ROW_E2]) + mi[ROW_RANK2]
    n_parts = (T * TOP_K) // part + E
    tok = jnp.arange(T, dtype=jnp.int32)
    slot_dst = jnp.full((n_parts * part,), -1, jnp.int32).at[jnp.concatenate([dest1, dest2])].set(
        jnp.concatenate([tok, tok + T]), unique_indices=True).reshape(n_parts, part)

    def windows(a):
        return jnp.concatenate([jnp.roll(a, -s, axis=0) for s in range(ITEM_PARTS)], axis=1).reshape(n_parts, 1, tm)

    tok_win = windows(jnp.where(slot_dst < 0, 0, jnp.where(slot_dst >= T, slot_dst - T, slot_dst)))
    out_win = windows(slot_dst)
    out_win = jnp.where(out_win < 0, 2 * T + jnp.arange(tm, dtype=jnp.int32), out_win)
    n_items = (T * TOP_K) // tm + E
    items_per_expert = (padded + tm - 1) // tm
    cum_items = jnp.cumsum(items_per_expert)
    item = jnp.arange(n_items, dtype=jnp.int32)
    item_expert = jnp.minimum(jnp.sum((item[:, None] >= cum_items[None, :]).astype(jnp.int32), axis=1), E - 1)
    k = item - per_expert(cum_items - items_per_expert, item_expert)
    n_used = cum_items[-1:]
    used = item < n_used[0]
    item_rows = jnp.where(used, jnp.clip(per_expert(padded, item_expert) - k * tm, 0, tm), 0)
    item_start = jnp.where(used, (per_expert(pad_start, item_expert) + k * tm) // part, 0)
    y = _expert_ffn(x, g, tok_win, out_win, item_expert, item_start, item_rows, n_used, wg, wu, wd)
    top_w = jnp.transpose(mf[jnp.array([ROW_W1, ROW_W2])])
    return _combine(x, top_w, y, fg, final_norm)


def kernel(x, mix_norm, w_in, w_pool, pool_scale, w_gate_up, b_gate, gla_norm, w_out, ffn_norm,
           dense_w_gate, dense_w_up, dense_w_down, w_router, exp_w_gate, exp_w_up, exp_w_down, final_norm):
    B, S, D = x.shape
    depth = w_in.shape[0]
    G, C = w_pool.shape[1], w_pool.shape[2]
    pool_w = G * C
    rank, key = w_gate_up.shape[1], w_gate_up.shape[2]
    width = gla_norm.shape[1]
    gate_off = pool_w + 2 * key + width
    assert B == 1 and rank <= LANES and w_in.shape[2] == gate_off + rank + width
    xt = x.reshape(S, D)
    fg = final_norm.reshape(1, D)
    for l in range(depth):
        w = w_in[l]
        w_main = jnp.concatenate([w[:, :gate_off], w[:, gate_off + rank:]], axis=1).astype(BF16)
        w_gl = jnp.pad(w[:, gate_off:gate_off + rank], ((0, 0), (0, LANES - rank))).astype(BF16)
        z, zg = _inproj(xt, mix_norm[l].reshape(1, D), w_main, w_gl)
        pool_out = _pool(z, w_pool[l].astype(BF16), pool_scale[l].reshape(1, pool_w))
        wgu = jnp.pad(w_gate_up[l], ((0, LANES - rank), (0, 0))).astype(BF16)
        gla_out = _gla(z, zg, wgu, b_gate[l].reshape(1, key), gla_norm[l].reshape(1, width),
                       q_off=pool_w, k_off=pool_w + key, v_off=pool_w + 2 * key, r_off=gate_off,
                       key=key, width=width)
        xt = _outproj(xt, pool_out, gla_out, w_out[l].astype(BF16))
        last = l == depth - 1
        i = l // 2
        if l % 2 == 0:
            xt = _dense_ffn(xt, ffn_norm[l].reshape(1, D), dense_w_gate[i].astype(BF16),
                            dense_w_up[i].astype(BF16), dense_w_down[i].astype(BF16), fg, last)
        else:
            xt = _moe_ffn(xt, ffn_norm[l].reshape(1, D), w_router[i], exp_w_gate[i], exp_w_up[i],
                          exp_w_down[i], fg, last)
    return xt.reshape(B, S, D)
```

```python
import functools

import jax
import jax.numpy as jnp
from jax import lax
from jax.experimental import pallas as pl
from jax.experimental.pallas import tpu as pltpu

EPS = 1e-6
POOL_WINDOWS = (2, 4, 8, 16)
GLA_HEADS = 4
GATE_TAU = 16.0
CHUNK = 64
TOP_K = 2
ITEM_PARTS = 4

LANES = 128
V7X_VMEM_BYTES = 64 * 1024 * 1024
VMEM_CAP_BYTES = V7X_VMEM_BYTES - 2 * 1024 * 1024

F32 = jnp.float32
BF16 = jnp.bfloat16
HIGHEST = lax.Precision.HIGHEST


def _tile(n, pref):
    t = min(n, pref)
    while n % t:
        t -= 1
    return t


def _params(vmem_estimate_bytes, n_axes):
    limit = min(VMEM_CAP_BYTES, max(32 * 1024 * 1024, int(vmem_estimate_bytes * 1.25)))
    return pltpu.CompilerParams(
        dimension_semantics=("arbitrary",) * n_axes, vmem_limit_bytes=limit)


def _rmsnorm_rows(x_ref, g_ref, dst_ref, straight_line=False):
    rows = x_ref.shape[0]
    chunk = _tile(rows, 128)

    def body(c, carry):
        r0 = c * chunk if straight_line else pl.multiple_of(c * chunk, chunk)
        x = x_ref[pl.ds(r0, chunk), :]
        ms = jnp.mean(x * x, axis=-1, keepdims=True)
        dst_ref[pl.ds(r0, chunk), :] = (x * lax.rsqrt(ms + EPS) * g_ref[...]).astype(dst_ref.dtype)
        return carry

    if straight_line:
        for c in range(rows // chunk):
            body(c, 0)
    else:
        lax.fori_loop(0, rows // chunk, body, 0)


def _silu(a):
    return a * (1.0 / (1.0 + jnp.exp(-a)))


def _inproj_kernel(x_ref, g_ref, w_ref, wgl_ref, z_ref, zg_ref, h_scr):
    @pl.when(pl.program_id(1) == 0)
    def _():
        _rmsnorm_rows(x_ref, g_ref, h_scr)
        zg_ref[...] = jnp.dot(h_scr[...], wgl_ref[...], preferred_element_type=F32)

    z_ref[...] = jnp.dot(h_scr[...], w_ref[...], preferred_element_type=F32).astype(z_ref.dtype)


def _inproj(x, g, w_main, w_gl):
    T, D = x.shape
    N = w_main.shape[1]
    tm, tn = _tile(T, 1024), _tile(N, 1024)
    est = 2 * tm * D * 4 + tm * D * 2 + 2 * D * tn * 2 + 2 * tm * tn * 2 + 2 * D * LANES * 2 + 2 * tm * LANES * 4
    return pl.pallas_call(
        _inproj_kernel,
        out_shape=(jax.ShapeDtypeStruct((T, N), BF16), jax.ShapeDtypeStruct((T, LANES), F32)),
        grid=(T // tm, N // tn),
        in_specs=[
            pl.BlockSpec((tm, D), lambda i, j: (i, 0)),
            pl.BlockSpec((1, D), lambda i, j: (0, 0)),
            pl.BlockSpec((D, tn), lambda i, j: (0, j)),
            pl.BlockSpec((D, LANES), lambda i, j: (0, 0)),
        ],
        out_specs=(
            pl.BlockSpec((tm, tn), lambda i, j: (i, j)),
            pl.BlockSpec((tm, LANES), lambda i, j: (i, 0)),
        ),
        scratch_shapes=[pltpu.VMEM((tm, D), BF16)],
        compiler_params=_params(est, 2),
        name="inproj",
    )(x, g, w_main, w_gl)


POOL_HALO = 128


POOL_HEAD = 16


def _pool_kernel(u_ref, halo_ref, band_ref, band_h_ref, wp_ref, ps_ref, o_ref):
    i = pl.program_id(0)
    tp = u_ref.shape[0]
    C = wp_ref.shape[1]
    t1 = i * tp + lax.broadcasted_iota(jnp.int32, (tp, 1), 0) + 1
    for gi, w in enumerate(POOL_WINDOWS):
        cols = slice(gi * C, (gi + 1) * C)
        u = u_ref[:, cols]
        halo = halo_ref[:, cols]
        halo = jnp.where(i > 0, halo, jnp.zeros_like(halo))
        head = jnp.dot(band_h_ref[gi], halo, preferred_element_type=F32)
        win_sum = jnp.dot(band_ref[gi], u, preferred_element_type=F32) + jnp.concatenate(
            [head, jnp.zeros((tp - POOL_HEAD, C), F32)], axis=0)
        count = jnp.minimum(t1, w).astype(F32)
        d = win_sum / count - u.astype(F32)
        y = jnp.dot(d.astype(BF16), wp_ref[gi], preferred_element_type=F32) * ps_ref[:, cols]
        o_ref[:, cols] = y.astype(o_ref.dtype)


def _pool(z, w_pool, pool_scale):
    T = z.shape[0]
    G, C, _ = w_pool.shape
    W = G * C
    tp = _tile(T, 256)
    assert tp % POOL_HALO == 0 and POOL_HALO >= POOL_HEAD >= max(POOL_WINDOWS) - 1 and G == len(POOL_WINDOWS)
    hb = tp // POOL_HALO
    win = jnp.asarray(POOL_WINDOWS, jnp.int32)[:, None, None]
    row = jnp.arange(tp, dtype=jnp.int32)[None, :, None]
    band = ((jnp.arange(tp)[None, None, :] <= row) & (jnp.arange(tp)[None, None, :] > row - win)).astype(BF16)
    band_h = (jnp.arange(POOL_HALO)[None, None, :] >= row[:, :POOL_HEAD] + (POOL_HALO + 1) - win).astype(BF16)
    est = 2 * (tp + POOL_HALO) * W * 2 + 2 * G * C * C * 2 + 2 * tp * W * 2 + 2 * G * tp * (tp + POOL_HALO) * 2
    return pl.pallas_call(
        _pool_kernel,
        out_shape=jax.ShapeDtypeStruct((T, W), BF16),
        grid=(T // tp,),
        in_specs=[
            pl.BlockSpec((tp, W), lambda i: (i, 0)),
            pl.BlockSpec((POOL_HALO, W), lambda i: (jnp.maximum(i * hb - 1, 0), 0)),
            pl.BlockSpec((G, tp, tp), lambda i: (0, 0, 0)),
            pl.BlockSpec((G, POOL_HEAD, POOL_HALO), lambda i: (0, 0, 0)),
            pl.BlockSpec((G, C, C), lambda i: (0, 0, 0)),
            pl.BlockSpec((1, W), lambda i: (0, 0)),
        ],
        out_specs=pl.BlockSpec((tp, W), lambda i: (i, 0)),
        compiler_params=_params(est, 1),
        name="pool",
    )(z, z, band, band_h, w_pool, pool_scale)


def _log_sigmoid(x):
    return jnp.minimum(x, 0.0) - jnp.log(1.0 + jnp.exp(-jnp.abs(x)))


def _gla_gate_logits(zg_ref, wgu_ref, bg_ref):
    return jnp.dot(zg_ref[...].astype(BF16), wgu_ref[...], preferred_element_type=F32) + bg_ref[...]


def _gla_cum_log_decay(logit, tri):
    g = _log_sigmoid(logit) * (1.0 / GATE_TAU)
    g_head = g.astype(BF16)
    g_rest = (g - g_head.astype(F32)).astype(BF16)
    return (jnp.dot(tri, g_head, preferred_element_type=F32)
            + jnp.dot(tri, g_rest, preferred_element_type=F32))


def _gla_kernel(q_ref, k_ref, v_ref, r_ref, zg_ref, zg_next_ref, wgu_ref, bg_ref, gn_ref, tri_ref, o_ref,
                st_ref, bc_ref, bc_next_ref, *, dk, dv):
    @pl.when(pl.program_id(0) == 0)
    def _():
        st_ref[...] = jnp.zeros_like(st_ref)
        bc_next_ref[...] = _gla_cum_log_decay(_gla_gate_logits(zg_ref, wgu_ref, bg_ref), tri_ref[...])

    tg = q_ref.shape[0]
    n_chunks = tg // CHUNK
    heads = range(GLA_HEADS)
    ks = [slice(h * dk, (h + 1) * dk) for h in heads]
    vs = [slice(h * dv, (h + 1) * dv) for h in heads]
    ri = lax.broadcasted_iota(jnp.int32, (CHUNK, CHUNK), 0)
    ci = lax.broadcasted_iota(jnp.int32, (CHUNK, CHUNK), 1)
    causal = ci <= ri
    scale = dk ** -0.5
    nt = (((1,), (1,)), ((), ()))
    tn = (((0,), (0,)), ((), ()))

    bc_ref[...] = bc_next_ref[...]
    logit_next = _gla_gate_logits(zg_next_ref, wgu_ref, bg_ref)

    def stage_a(c):
        rows = slice(c * CHUNK, (c + 1) * CHUNK)
        bcc = bc_ref[rows, :]
        b_last = bcc[CHUNK - 1:CHUNK, :]
        kf = k_ref[rows, :].astype(F32)
        q_dec = (q_ref[rows, :].astype(F32) * scale * jnp.exp(bcc)).astype(BF16)
        k_dec = (kf * jnp.exp(-bcc)).astype(BF16)
        k_end = (kf * jnp.exp(b_last - bcc)).astype(BF16)
        v = [v_ref[rows, vs[h]] for h in heads]
        att = [lax.dot_general(q_dec[:, ks[h]], k_dec[:, ks[h]], nt, preferred_element_type=F32) for h in heads]
        kv = [lax.dot_general(v[h], k_end[:, ks[h]], tn, preferred_element_type=F32) for h in heads]
        return rows, q_dec, jnp.exp(b_last), v, att, kv

    def stage_b(rows, q_dec, decay, v, att, kv):
        s_t = [st_ref[h] for h in heads]
        o_inter = [lax.dot_general(q_dec[:, ks[h]], s_t[h].astype(BF16), nt, preferred_element_type=F32)
                   for h in heads]
        for h in heads:
            st_ref[h] = s_t[h] * decay[:, ks[h]] + kv[h]
        o_intra = [jnp.dot(jnp.where(causal, att[h], 0.0).astype(BF16), v[h], preferred_element_type=F32)
                   for h in heads]
        for h in heads:
            o = o_inter[h] + o_intra[h]
            o = o * lax.rsqrt(jnp.mean(o * o, axis=-1, keepdims=True) + EPS) * gn_ref[:, vs[h]]
            o_ref[rows, vs[h]] = (o * _silu(r_ref[rows, vs[h]].astype(F32))).astype(o_ref.dtype)

    pending = stage_a(0)
    for c in range(1, n_chunks):
        upcoming = stage_a(c)
        if c == 1:
            bc_next_ref[...] = _gla_cum_log_decay(logit_next, tri_ref[...])
        stage_b(*pending)
        pending = upcoming
    if n_chunks == 1:
        bc_next_ref[...] = _gla_cum_log_decay(logit_next, tri_ref[...])
    stage_b(*pending)


def _gla(z, zg, wgu, bg, gn, *, q_off, k_off, v_off, r_off, key, width):
    T = z.shape[0]
    dk, dv = key // GLA_HEADS, width // GLA_HEADS
    tg = _tile(T, 256)
    assert tg % CHUNK == 0 and T % CHUNK == 0
    assert q_off % key == 0 and k_off % key == 0 and v_off % width == 0 and r_off % width == 0
    est = 2 * tg * (2 * key + 3 * width) * 2 + 2 * tg * LANES * 4 + GLA_HEADS * dv * dk * 4 + 6 * tg * key * 4
    row = jnp.arange(tg, dtype=jnp.int32)
    tri = ((row[None, :] <= row[:, None]) & (row[None, :] // CHUNK == row[:, None] // CHUNK)).astype(BF16)
    return pl.pallas_call(
        functools.partial(_gla_kernel, dk=dk, dv=dv),
        out_shape=jax.ShapeDtypeStruct((T, width), BF16),
        grid=(T // tg,),
        in_specs=[
            pl.BlockSpec((tg, key), lambda i: (i, q_off // key)),
            pl.BlockSpec((tg, key), lambda i: (i, k_off // key)),
            pl.BlockSpec((tg, width), lambda i: (i, v_off // width)),
            pl.BlockSpec((tg, width), lambda i: (i, r_off // width)),
            pl.BlockSpec((tg, LANES), lambda i: (i, 0)),
            pl.BlockSpec((tg, LANES), lambda i: (jnp.minimum(i + 1, T // tg - 1), 0)),
            pl.BlockSpec((LANES, key), lambda i: (0, 0)),
            pl.BlockSpec((1, key), lambda i: (0, 0)),
            pl.BlockSpec((1, width), lambda i: (0, 0)),
            pl.BlockSpec((tg, tg), lambda i: (0, 0)),
        ],
        out_specs=pl.BlockSpec((tg, width), lambda i: (i, 0)),
        scratch_shapes=[pltpu.VMEM((GLA_HEADS, dv, dk), F32), pltpu.VMEM((tg, key), F32),
                        pltpu.VMEM((tg, key), F32)],
        compiler_params=_params(est, 1),
        name="gla",
    )(z, z, z, z, zg, zg, wgu, bg, gn, tri)


def _outproj_kernel(x_ref, p_ref, a_ref, wp_ref, wa_ref, o_ref):
    o_ref[...] = (x_ref[...]
                  + jnp.dot(p_ref[...], wp_ref[...], preferred_element_type=F32)
                  + jnp.dot(a_ref[...], wa_ref[...], preferred_element_type=F32))


def _outproj(x, pool_out, gla_out, w_out):
    T, D = x.shape
    wp_rows, wa_rows = pool_out.shape[1], gla_out.shape[1]
    tm, tn = _tile(T, 1024), _tile(D, 1024)
    assert wp_rows % tn == 0 or wp_rows == w_out.shape[0]
    est = 4 * tm * tn * 4 + 2 * tm * (wp_rows + wa_rows) * 2 + 2 * (wp_rows + wa_rows) * tn * 2
    return pl.pallas_call(
        _outproj_kernel,
        out_shape=jax.ShapeDtypeStruct((T, D), F32),
        grid=(T // tm, D // tn),
        in_specs=[
            pl.BlockSpec((tm, tn), lambda i, j: (i, j)),
            pl.BlockSpec((tm, wp_rows), lambda i, j: (i, 0)),
            pl.BlockSpec((tm, wa_rows), lambda i, j: (i, 0)),
            pl.BlockSpec((wp_rows, tn), lambda i, j: (0, j)),
            pl.BlockSpec((wa_rows, tn), lambda i, j: (wp_rows // wa_rows, j)),
        ],
        out_specs=pl.BlockSpec((tm, tn), lambda i, j: (i, j)),
        compiler_params=_params(est, 2),
        name="outproj",
    )(x, pool_out, gla_out, w_out, w_out)


def _swiglu_up(h_ref, wg_ref, wu_ref):
    h = h_ref[...]
    a = jnp.dot(h, wg_ref[...].astype(BF16), preferred_element_type=F32)
    b = jnp.dot(h, wu_ref[...].astype(BF16), preferred_element_type=F32)
    return (_silu(a) * b).astype(BF16)


def _swiglu_down(t_ref, wd_ref, acc_ref, out_ref=None):
    out_ref = acc_ref if out_ref is None else out_ref
    out_ref[...] = acc_ref[...] + jnp.dot(t_ref[...], wd_ref[...].astype(BF16), preferred_element_type=F32)


def _swiglu_phase(phase, h_ref, t_ref, wg_ref, wu_ref, wd_ref, acc_ref, side_work=None, out_ref=None):
    if phase == "first":
        t_ref[...] = _swiglu_up(h_ref, wg_ref, wu_ref)
    elif phase == "last":
        _swiglu_down(t_ref, wd_ref, acc_ref, out_ref)
    else:
        t_new = _swiglu_up(h_ref, wg_ref, wu_ref)
        if side_work is not None:
            side_work()
        _swiglu_down(t_ref, wd_ref, acc_ref)
        t_ref[...] = t_new


def _ffn_kernel(x_ref, g_ref, wg_ref, wu_ref, wd_ref, fg_ref, o_ref, h_scr, t_scr, *, final_norm):
    f = pl.program_id(1)
    nf = pl.num_programs(1) - 1

    @pl.when(f == 0)
    def _():
        _rmsnorm_rows(x_ref, g_ref, h_scr)
        o_ref[...] = x_ref[...]

    args = (h_scr, t_scr, wg_ref, wu_ref, wd_ref, o_ref)
    pl.when(f == 0)(functools.partial(_swiglu_phase, "first", *args))
    pl.when((f > 0) & (f < nf))(functools.partial(_swiglu_phase, "steady", *args))
    pl.when(f == nf)(functools.partial(_swiglu_phase, "last", *args))

    if final_norm:
        @pl.when(f == nf)
        def _():
            _rmsnorm_rows(o_ref, fg_ref, o_ref)


def _dense_ffn(x, g, wg, wu, wd, fg, final_norm):
    T, D = x.shape
    F = wg.shape[1]
    tm, tf = _tile(T, 1024), _tile(F, 512)
    nf = F // tf
    wbytes = wg.dtype.itemsize
    est = 4 * tm * D * 4 + tm * D * 2 + tm * tf * 2 + 2 * 3 * D * tf * wbytes + 3 * tm * tf * 4
    return pl.pallas_call(
        functools.partial(_ffn_kernel, final_norm=final_norm),
        out_shape=jax.ShapeDtypeStruct((T, D), F32),
        grid=(T // tm, nf + 1),
        in_specs=[
            pl.BlockSpec((tm, D), lambda i, f: (i, 0)),
            pl.BlockSpec((1, D), lambda i, f: (0, 0)),
            pl.BlockSpec((D, tf), lambda i, f: (0, jnp.minimum(f, nf - 1))),
            pl.BlockSpec((D, tf), lambda i, f: (0, jnp.minimum(f, nf - 1))),
            pl.BlockSpec((tf, D), lambda i, f: (jnp.maximum(f - 1, 0), 0)),
            pl.BlockSpec((1, D), lambda i, f: (0, 0)),
        ],
        out_specs=pl.BlockSpec((tm, D), lambda i, f: (i, 0)),
        scratch_shapes=[pltpu.VMEM((tm, D), BF16), pltpu.VMEM((tm, tf), BF16)],
        compiler_params=_params(est, 2),
        name="dense_ffn",
    )(x, g, wg, wu, wd, fg)


SUBLANES = 8
ROW_E1, ROW_E2, ROW_RANK1, ROW_RANK2 = 0, 1, 2, 3
ROW_W1, ROW_W2 = 0, 1


def _router_kernel(x_ref, g_ref, wr_ref, earlier_ref, mi_ref, mf_ref, cnt_ref, h_scr, logit_scr, run_ref, *,
                   n_experts):
    @pl.when(pl.program_id(0) == 0)
    def _():
        run_ref[...] = jnp.zeros_like(run_ref)

    tm = x_ref.shape[0]
    _rmsnorm_rows(x_ref, g_ref, h_scr)
    logit_scr[...] = jnp.dot(h_scr[...], wr_ref[...], preferred_element_type=F32)
    lt = jnp.transpose(logit_scr[...])[:SUBLANES, :]
    sub = lax.broadcasted_iota(jnp.int32, (SUBLANES, tm), 0)
    neg = jnp.float32(-jnp.inf)
    l1 = jnp.where(sub < n_experts, lt, neg)
    m1 = jnp.max(l1, axis=0, keepdims=True)
    e1 = jnp.min(jnp.where(l1 == m1, sub, SUBLANES), axis=0, keepdims=True)
    l2 = jnp.where(sub == e1, neg, l1)
    m2 = jnp.max(l2, axis=0, keepdims=True)
    e2 = jnp.min(jnp.where(l2 == m2, sub, SUBLANES), axis=0, keepdims=True)
    ex = jnp.exp(m2 - m1)
    w1 = 1.0 / (1.0 + ex)
    w2 = ex / (1.0 + ex)
    onehot = jnp.where((sub == e1) | (sub == e2), 1.0, 0.0)
    before = jnp.dot(onehot, earlier_ref[...], preferred_element_type=F32) + run_ref[:, 0:1]
    rank1 = jnp.sum(jnp.where(sub == e1, before, 0.0), axis=0, keepdims=True).astype(jnp.int32)
    rank2 = jnp.sum(jnp.where(sub == e2, before, 0.0), axis=0, keepdims=True).astype(jnp.int32)
    run_ref[...] += jnp.sum(onehot, axis=1, keepdims=True)
    mi_ref[...] = jnp.where(sub == ROW_E1, e1, jnp.where(sub == ROW_E2, e2, jnp.where(
        sub == ROW_RANK1, rank1, jnp.where(sub == ROW_RANK2, rank2, 0))))
    mf_ref[...] = jnp.where(sub == ROW_W1, w1, jnp.where(sub == ROW_W2, w2, 0.0))
    cnt_ref[...] = run_ref[...]


def _router(x, g, wr, n_experts):
    T, D = x.shape
    assert n_experts <= SUBLANES
    tm = _tile(T, 512)
    tok = jnp.arange(tm, dtype=jnp.int32)
    earlier = (tok[:, None] < tok[None, :]).astype(F32)
    est = 2 * tm * D * 4 + tm * D * 2 + 2 * tm * tm * 4 + 2 * D * LANES * 2 + 4 * tm * LANES * 4
    return pl.pallas_call(
        functools.partial(_router_kernel, n_experts=n_experts),
        out_shape=(
            jax.ShapeDtypeStruct((SUBLANES, T), jnp.int32),
            jax.ShapeDtypeStruct((SUBLANES, T), F32),
            jax.ShapeDtypeStruct((SUBLANES, LANES), F32),
        ),
        grid=(T // tm,),
        in_specs=[
            pl.BlockSpec((tm, D), lambda i: (i, 0)),
            pl.BlockSpec((1, D), lambda i: (0, 0)),
            pl.BlockSpec((D, LANES), lambda i: (0, 0)),
            pl.BlockSpec((tm, tm), lambda i: (0, 0)),
        ],
        out_specs=(
            pl.BlockSpec((SUBLANES, tm), lambda i: (0, i)),
            pl.BlockSpec((SUBLANES, tm), lambda i: (0, i)),
            pl.BlockSpec((SUBLANES, LANES), lambda i: (0, 0)),
        ),
        scratch_shapes=[pltpu.VMEM((tm, D), BF16), pltpu.VMEM((tm, LANES), F32),
                        pltpu.VMEM((SUBLANES, LANES), F32)],
        compiler_params=_params(est, 1),
        name="router",
    )(x, g, wr, earlier)


def _expert_kernel(ie_ref, ist_ref, inr_ref, nu_ref, cur_tok, nxt_tok, prv_out,
                   x_hbm, g_ref, wg_ref, wu_ref, wd_ref, y_hbm,
                   xs_scr, xb_scr, t_scr, acc_scr, out_scr, gsem, ssem, *, n_tok, rows_per_step, n_issue):
    del ie_ref, ist_ref
    j, f = pl.program_id(0), pl.program_id(1)
    nf = pl.num_programs(1) - 1
    tm = xb_scr.shape[0]
    n_used = nu_ref[0]
    used = j < n_used
    xs_rows = xs_scr.at[pl.ds(0, tm)]

    def gather_copy(tok_ref, r):
        tok = tok_ref[0, jnp.minimum(r, tm - 1)]
        return pltpu.make_async_copy(x_hbm.at[pl.ds(tok, 1)], xs_scr.at[pl.ds(r, 1)], gsem)

    def scatter_copy(rows_valid, r):
        rc = jnp.minimum(r, tm - 1)
        dst = jnp.where(r < rows_valid, prv_out[0, rc], 2 * n_tok + r)
        return pltpu.make_async_copy(out_scr.at[pl.ds(rc, 1)], y_hbm.at[pl.ds(dst, 1)], ssem)

    def wait_gather():
        pltpu.make_async_copy(x_hbm.at[pl.ds(0, n_issue)], xs_scr.at[pl.ds(0, n_issue)], gsem).wait()

    def wait_scatter():
        pltpu.make_async_copy(xs_scr.at[pl.ds(0, n_issue)], y_hbm.at[pl.ds(0, n_issue)], ssem).wait()

    @pl.when((j == 0) & (f == 0))
    def _():
        def issue(r, carry):
            gather_copy(cur_tok, r).start()
            return carry
        lax.fori_loop(0, n_issue, issue, 0)
        wait_gather()
        _rmsnorm_rows(xs_rows, g_ref, xb_scr)
        out_scr[...] = jnp.zeros_like(out_scr)

    @pl.when(used & (f == 0))
    def _():
        acc_scr[...] = jnp.zeros_like(acc_scr)

    prev_rows = jnp.where(j > 0, inr_ref[jnp.maximum(j - 1, 0)], 0)

    def side_work():
        base = (f - 1) * rows_per_step
        for rr in range(rows_per_step):
            gather_copy(nxt_tok, base + rr).start()
            scatter_copy(prev_rows, base + rr).start()

    pl.when(used & (f == nf))(wait_scatter)

    def phases(rows):
        args = (xb_scr.at[pl.ds(0, rows)], t_scr.at[pl.ds(0, rows)], wg_ref, wu_ref, wd_ref,
                acc_scr.at[pl.ds(0, rows)])
        steady = (f > 0) & (f < nf)
        pl.when(f == 0)(functools.partial(_swiglu_phase, "first", *args))
        pl.when(steady)(functools.partial(_swiglu_phase, "steady", *args, side_work=side_work))
        pl.when(f == nf)(functools.partial(_swiglu_phase, "last", *args, out_ref=out_scr.at[pl.ds(0, rows)]))

    for parts in range(1, ITEM_PARTS + 1):
        rows = parts * (tm // ITEM_PARTS)
        pl.when(used & (inr_ref[j] == rows))(functools.partial(phases, rows))

    @pl.when(used & (f == nf))
    def _():
        wait_gather()
        _rmsnorm_rows(xs_rows, g_ref, xb_scr)

    @pl.when((j == n_used) & (f == 0))
    def _():
        def issue(r, carry):
            scatter_copy(prev_rows, r).start()
            return carry
        lax.fori_loop(0, n_issue, issue, 0)
        wait_scatter()


def _expert_ffn(x, g, tok_win, out_win, item_expert, item_start, item_rows, n_used, wg, wu, wd):
    T, D = x.shape
    E, _, F = wg.shape
    n_items = item_expert.shape[0]
    tm = tok_win.shape[2]
    tf = _tile(F, 512)
    nf = F // tf
    assert nf >= 2
    rows_per_step = -(-tm // ((nf - 1) * SUBLANES)) * SUBLANES
    n_issue = issue_rows = rows_per_step * (nf - 1)

    def w_col(j, f, ie, ist, inr, nu):
        return (ie[j], 0, jnp.where(j < nu[0], jnp.minimum(f, nf - 1), nf - 1))

    def w_row(j, f, ie, ist, inr, nu):
        return (ie[j], jnp.where(j < nu[0], jnp.maximum(f - 1, 0), nf - 1), 0)

    def slots_of_item(shift):
        def index_map(j, f, ie, ist, inr, nu):
            return (ist[jnp.clip(j + shift, 0, n_items - 1)], 0, 0)
        return pl.BlockSpec((None, 1, tm), index_map, memory_space=pltpu.SMEM)

    wbytes = wg.dtype.itemsize
    est = (tm * D * 4 + tm * D * 2 + tm * tf * 2 + 2 * 3 * D * tf * wbytes + 2 * tm * D * 4
           + 3 * tm * tf * 4 + 3 * D * tf * 2)
    return pl.pallas_call(
        functools.partial(_expert_kernel, n_tok=T, rows_per_step=rows_per_step, n_issue=n_issue),
        out_shape=jax.ShapeDtypeStruct((2 * T + issue_rows, D), F32),
        grid_spec=pltpu.PrefetchScalarGridSpec(
            num_scalar_prefetch=4,
            grid=(n_items, nf + 1),
            in_specs=[
                slots_of_item(0), slots_of_item(1), slots_of_item(-1),
                pl.BlockSpec(memory_space=pl.ANY),
                pl.BlockSpec((1, D), lambda j, f, ie, ist, inr, nu: (0, 0)),
                pl.BlockSpec((None, D, tf), w_col),
                pl.BlockSpec((None, D, tf), w_col),
                pl.BlockSpec((None, tf, D), w_row),
            ],
            out_specs=pl.BlockSpec(memory_space=pl.ANY),
            scratch_shapes=[pltpu.VMEM((issue_rows, D), F32), pltpu.VMEM((tm, D), BF16), pltpu.VMEM((tm, tf), BF16),
                            pltpu.VMEM((tm, D), F32), pltpu.VMEM((tm, D), F32),
                            pltpu.SemaphoreType.DMA, pltpu.SemaphoreType.DMA],
        ),
        compiler_params=_params(est, 2),
        name="expert_ffn",
    )(item_expert, item_start, item_rows, n_used, tok_win, tok_win, out_win, x, g, wg, wu, wd)


def _combine_kernel(x_ref, w_ref, fg_ref, y1_ref, y2_ref, o_ref, *, final_norm):
    w = w_ref[...]
    o_ref[...] = x_ref[...] + w[:, 0:1] * y1_ref[...] + w[:, 1:2] * y2_ref[...]
    if final_norm:
        _rmsnorm_rows(o_ref, fg_ref, o_ref)


def _combine(x, top_w, y, fg, final_norm):
    T, D = x.shape
    tc = _tile(T, 512)
    est = 8 * tc * D * 4 + 2 * tc * LANES * 4
    return pl.pallas_call(
        functools.partial(_combine_kernel, final_norm=final_norm),
        out_shape=jax.ShapeDtypeStruct((T, D), F32),
        grid=(T // tc,),
        in_specs=[
            pl.BlockSpec((tc, D), lambda i: (i, 0)),
            pl.BlockSpec((tc, TOP_K), lambda i: (i, 0)),
            pl.BlockSpec((1, D), lambda i: (0, 0)),
            pl.BlockSpec((tc, D), lambda i: (i, 0)),
            pl.BlockSpec((tc, D), lambda i: (i + T // tc, 0)),
        ],
        out_specs=pl.BlockSpec((tc, D), lambda i: (i, 0)),
        compiler_params=_params(est, 1),
        name="combine",
    )(x, top_w, fg, y, y)


def _moe_ffn(x, g, w_router, wg, wu, wd, fg, final_norm):
    T, D = x.shape
    E = w_router.shape[1]
    tm = _tile(T, 1024)
    part = tm // ITEM_PARTS
    assert (T * TOP_K) % tm == 0 and E <= part and part % 16 == 0
    wr = jnp.pad(w_router, ((0, 0), (0, LANES - E))).astype(BF16)
    mi, mf, cnt = _router(x, g, wr, E)
    experts = jnp.arange(E, dtype=jnp.int32)

    def per_expert(table, e):
        return jnp.sum(jnp.where(e[:, None] == experts[None, :], table[None, :], 0), axis=1)

    counts = cnt[:E, 0].astype(jnp.int32)
    padded = ((counts + part - 1) // part) * part
    cum_padded = jnp.cumsum(padded)
    pad_start = cum_padded - padded
    dest1 = per_expert(pad_start, mi[ROW_E1]) + mi[ROW_RANK1]
    dest2 = per_expert(pad_start, mi[ROW_E2]) + mi[ROW_RANK2]
    n_parts = (T * TOP_K) // part + E
    tok = jnp.arange(T, dtype=jnp.int32)
    slot_dst = jnp.full((n_parts * part,), -1, jnp.int32).at[jnp.concatenate([dest1, dest2])].set(
        jnp.concatenate([tok, tok + T]), unique_indices=True).reshape(n_parts, part)

    def windows(a):
        return jnp.concatenate([jnp.roll(a, -s, axis=0) for s in range(ITEM_PARTS)], axis=1).reshape(n_parts, 1, tm)

    tok_win = windows(jnp.where(slot_dst < 0, 0, jnp.where(slot_dst >= T, slot_dst - T, slot_dst)))
    out_win = windows(slot_dst)
    out_win = jnp.where(out_win < 0, 2 * T + jnp.arange(tm, dtype=jnp.int32), out_win)
    n_items = (T * TOP_K) // tm + E
    items_per_expert = (padded + tm - 1) // tm
    cum_items = jnp.cumsum(items_per_expert)
    item = jnp.arange(n_items, dtype=jnp.int32)
    item_expert = jnp.minimum(jnp.sum((item[:, None] >= cum_items[None, :]).astype(jnp.int32), axis=1), E - 1)
    k = item - per_expert(cum_items - items_per_expert, item_expert)
    n_used = cum_items[-1:]
    used = item < n_used[0]
    item_rows = jnp.where(used, jnp.clip(per_expert(padded, item_expert) - k * tm, 0, tm), 0)
    item_start = jnp.where(used, (per_expert(pad_start, item_expert) + k * tm) // part, 0)
    y = _expert_ffn(x, g, tok_win, out_win, item_expert, item_start, item_rows, n_used, wg, wu, wd)
    top_w = jnp.transpose(mf[jnp.array([ROW_W1, ROW_W2])])
    return _combine(x, top_w, y, fg, final_norm)


def kernel(x, mix_norm, w_in, w_pool, pool_scale, w_gate_up, b_gate, gla_norm, w_out, ffn_norm,
           dense_w_gate, dense_w_up, dense_w_down, w_router, exp_w_gate, exp_w_up, exp_w_down, final_norm):
    B, S, D = x.shape
    depth = w_in.shape[0]
    G, C = w_pool.shape[1], w_pool.shape[2]
    pool_w = G * C
    rank, key = w_gate_up.shape[1], w_gate_up.shape[2]
    width = gla_norm.shape[1]
    gate_off = pool_w + 2 * key + width
    assert B == 1 and rank <= LANES and w_in.shape[2] == gate_off + rank + width
    xt = x.reshape(S, D)
    fg = final_norm.reshape(1, D)
    for l in range(depth):
        w = w_in[l]
        w_main = jnp.concatenate([w[:, :gate_off], w[:, gate_off + rank:]], axis=1).astype(BF16)
        w_gl = jnp.pad(w[:, gate_off:gate_off + rank], ((0, 0), (0, LANES - rank))).astype(BF16)
        z, zg = _inproj(xt, mix_norm[l].reshape(1, D), w_main, w_gl)
        pool_out = _pool(z, w_pool[l].astype(BF16), pool_scale[l].reshape(1, pool_w))
        wgu = jnp.pad(w_gate_up[l], ((0, LANES - rank), (0, 0))).astype(BF16)
        gla_out = _gla(z, zg, wgu, b_gate[l].reshape(1, key), gla_norm[l].reshape(1, width),
                       q_off=pool_w, k_off=pool_w + key, v_off=pool_w + 2 * key, r_off=gate_off,
                       key=key, width=width)
        xt = _outproj(xt, pool_out, gla_out, w_out[l].astype(BF16))
        last = l == depth - 1
        i = l // 2
        if l % 2 == 0:
            xt = _dense_ffn(xt, ffn_norm[l].reshape(1, D), dense_w_gate[i].astype(BF16),
                            dense_w_up[i].astype(BF16), dense_w_down[i].astype(BF16), fg, last)
        else:
            xt = _moe_ffn(xt, ffn_norm[l].reshape(1, D), w_router[i], exp_w_gate[i], exp_w_up[i],
                          exp_w_down[i], fg, last)
    return xt.reshape(B, S, D)
```
